```python
import math
import jax, jax.numpy as jnp
from jax import lax
import numpy as np

D_MODEL = 4096
BATCH = 2
SEQ = 8192
DEPTH = 2

N_MIXERS = 2
N_S5_LAYERS = (DEPTH + 1) // 2
N_GM_LAYERS = DEPTH // 2
S5_WIDTH = D_MODEL
S5_GROUP = 16
S5_STATE = 64
S5_GROUPS = S5_WIDTH // S5_GROUP
DT_MIN = 1e-3
DT_MAX = 1e-1
GM_WIDTH = D_MODEL
GM_CHUNK = 128
GM_HEADS = 16
GM_HEAD_DIM = GM_WIDTH // GM_HEADS
N_EXPERTS = 16
N_EXPERT_GROUPS = 4
EXPERTS_PER_GROUP = N_EXPERTS // N_EXPERT_GROUPS
TOP_K = 2
D_EXPERT = 1024
EPS = 1e-6

kernel_name = 'hybrid_s5_gmlp_grouped_moe_adaln_encoder'


def _rms_modulate(x, g, shift, scale):
    xf = x.astype(jnp.float32)
    y = xf * lax.rsqrt(jnp.mean(xf * xf, axis=-1, keepdims=True) + EPS) * g.astype(jnp.float32)
    y = y * (1.0 + scale.astype(jnp.float32)[:, None, :]) + shift.astype(jnp.float32)[:, None, :]
    return y.astype(x.dtype)


def _rmsnorm(x, g):
    xf = x.astype(jnp.float32)
    y = xf * lax.rsqrt(jnp.mean(xf * xf, axis=-1, keepdims=True) + EPS) * g.astype(jnp.float32)
    return y.astype(x.dtype)


def _adaln(c, w, b):
    m = jax.nn.silu(c) @ w + b
    return jnp.split(m, 3, axis=-1)


def _ssm_combine(e1, e2):
    a1, b1 = e1
    a2, b2 = e2
    return a1 * a2, a2 * b1 + b2


def _s5_direction(ug, lam_re, lam_im, log_step, b_re, b_im, c_re, c_im, reverse):
    f32 = jnp.float32
    lam = lax.complex(lam_re.astype(f32), lam_im.astype(f32))
    step = jnp.exp(log_step.astype(f32))[:, None]
    lam_bar = jnp.exp(lam * step)
    b_mat = lax.complex(b_re.astype(f32), b_im.astype(f32))
    b_bar = ((lam_bar - 1.0) / lam)[..., None] * b_mat
    c_mat = lax.complex(c_re.astype(f32), c_im.astype(f32))
    bu = jnp.einsum('blgh,gph->blgp', ug.astype(jnp.complex64), b_bar)
    a = jnp.broadcast_to(lam_bar, bu.shape)
    _, states = lax.associative_scan(_ssm_combine, (a, bu), axis=1, reverse=reverse)
    return jnp.einsum('blgp,ghp->blgh', states, c_mat).real


def _s5_mixer(h, w_in, lam_re, lam_im, log_step, b_re, b_im, c_re, c_im, d, w_glu, b_glu, w_out):
    bsz, L, _ = h.shape
    u = h @ w_in
    ug = u.astype(jnp.float32).reshape(bsz, L, S5_GROUPS, S5_GROUP)
    y = (_s5_direction(ug, lam_re[0], lam_im[0], log_step[0], b_re[0], b_im[0], c_re[0], c_im[0], False)
         + _s5_direction(ug, lam_re[1], lam_im[1], log_step[1], b_re[1], b_im[1], c_re[1], c_im[1], True))
    y = y.reshape(bsz, L, S5_WIDTH) + d.astype(jnp.float32) * u.astype(jnp.float32)
    y = jax.nn.gelu(y).astype(h.dtype)
    y = y * jax.nn.sigmoid(y @ w_glu + b_glu)
    return y @ w_out


def _gmlp_mixer(h, w_in, b_in, ln_g, ln_b, ws, bs, w_out):
    bsz, L, _ = h.shape
    z = jax.nn.gelu(h @ w_in + b_in)
    u, v = jnp.split(z, 2, axis=-1)
    vf = v.astype(jnp.float32)
    mu = jnp.mean(vf, axis=-1, keepdims=True)
    var = jnp.mean(jnp.square(vf - mu), axis=-1, keepdims=True)
    vf = (vf - mu) * lax.rsqrt(var + EPS) * ln_g.astype(jnp.float32) + ln_b.astype(jnp.float32)
    vc = vf.reshape(bsz, L // GM_CHUNK, GM_CHUNK, GM_HEADS, GM_HEAD_DIM)
    sv = jnp.einsum('hij,bcjhd->bcihd', ws.astype(jnp.float32), vc)
    sv = sv + jnp.transpose(bs.astype(jnp.float32))[None, None, :, :, None]
    gated = u * sv.reshape(bsz, L, GM_WIDTH).astype(u.dtype)
    return gated @ w_out


def _moe(h, router_w, router_bias, w1, w3, w2):
    bsz, L, D = h.shape
    t = h.reshape(-1, D)
    scores = jax.nn.sigmoid((t @ router_w).astype(jnp.float32))
    sel = scores + router_bias.astype(jnp.float32)
    sel_g = sel.reshape(-1, N_EXPERT_GROUPS, EXPERTS_PER_GROUP)
    group_score = jnp.sum(lax.top_k(sel_g, TOP_K)[0], axis=-1)
    best_group = jnp.argmax(group_score, axis=-1)
    in_group = jnp.arange(N_EXPERT_GROUPS)[None, :] == best_group[:, None]
    masked = jnp.where(in_group[:, :, None], sel_g, -jnp.inf).reshape(-1, N_EXPERTS)
    _, idx = lax.top_k(masked, TOP_K)
    wsel = jnp.take_along_axis(scores, idx, axis=-1)
    wsel = wsel / jnp.sum(wsel, axis=-1, keepdims=True)
    gates = jnp.sum(jax.nn.one_hot(idx, N_EXPERTS, dtype=jnp.float32) * wsel[..., None], axis=1)
    out = jnp.zeros(t.shape, jnp.float32)
    for e in range(N_EXPERTS):
        he = jax.nn.silu(t @ w1[e]) * (t @ w3[e])
        out = out + gates[:, e:e + 1] * (he @ w2[e]).astype(jnp.float32)
    return out.astype(h.dtype).reshape(bsz, L, D)


def setup_inputs(seed: int = 0) -> dict:
    key = jax.random.key(seed)
    ks = jax.random.split(key, 40)
    f32 = jnp.float32

    def nrm(k, shape, scale):
        return jax.random.normal(k, shape, f32) * scale

    D, W, G, P, H = D_MODEL, S5_WIDTH, S5_GROUPS, S5_STATE, S5_GROUP
    nA, nB = N_S5_LAYERS, N_GM_LAYERS
    n_idx = jnp.arange(P, dtype=f32)
    return {
        'x': nrm(ks[0], (BATCH, SEQ, D), 1.0),
        'c': nrm(ks[1], (BATCH, D), 1.0),
        'norm_mix_g': 1.0 + nrm(ks[2], (DEPTH, D), 0.02),
        'norm_ffn_g': 1.0 + nrm(ks[3], (DEPTH, D), 0.02),
        'norm_final_g': 1.0 + nrm(ks[4], (D,), 0.02),
        'ada_mix_w': nrm(ks[5], (DEPTH, D, 3 * D), 0.5 * D ** -0.5),
        'ada_mix_b': nrm(ks[6], (DEPTH, 3 * D), 0.02),
        'ada_ffn_w': nrm(ks[7], (DEPTH, D, 3 * D), 0.5 * D ** -0.5),
        'ada_ffn_b': nrm(ks[8], (DEPTH, 3 * D), 0.02),
        's5_w_in': nrm(ks[9], (nA, D, W), D ** -0.5),
        's5_lam_re': -0.5 + nrm(ks[10], (nA, 2, G, P), 0.01),
        's5_lam_im': math.pi * n_idx + nrm(ks[11], (nA, 2, G, P), 0.01),
        's5_log_step': jax.random.uniform(ks[12], (nA, 2, G), f32, math.log(DT_MIN), math.log(DT_MAX)),
        's5_b_re': nrm(ks[13], (nA, 2, G, P, H), (2 * H) ** -0.5),
        's5_b_im': nrm(ks[14], (nA, 2, G, P, H), (2 * H) ** -0.5),
        's5_c_re': nrm(ks[15], (nA, 2, G, H, P), (2 * P) ** -0.5),
        's5_c_im': nrm(ks[16], (nA, 2, G, H, P), (2 * P) ** -0.5),
        's5_d': nrm(ks[17], (nA, W), 1.0),
        's5_w_glu': nrm(ks[18], (nA, W, W), W ** -0.5),
        's5_b_glu': nrm(ks[19], (nA, W), 0.02),
        's5_w_out': nrm(ks[20], (nA, W, D), W ** -0.5),
        'gm_w_in': nrm(ks[21], (nB, D, 2 * GM_WIDTH), D ** -0.5),
        'gm_b_in': nrm(ks[22], (nB, 2 * GM_WIDTH), 0.02),
        'gm_ln_g': 1.0 + nrm(ks[23], (nB, GM_WIDTH), 0.02),
        'gm_ln_b': nrm(ks[24], (nB, GM_WIDTH), 0.02),
        'gm_ws': nrm(ks[25], (nB, GM_HEADS, GM_CHUNK, GM_CHUNK), GM_CHUNK ** -0.5),
        'gm_bs': 1.0 + nrm(ks[26], (nB, GM_HEADS, GM_CHUNK), 0.1),
        'gm_w_out': nrm(ks[27], (nB, GM_WIDTH, D), GM_WIDTH ** -0.5),
        'router_w': nrm(ks[28], (D, N_EXPERTS), D ** -0.5),
        'router_bias': nrm(ks[29], (N_EXPERTS,), 0.01),
        'moe_w1': nrm(ks[30], (DEPTH, N_EXPERTS, D, D_EXPERT), D ** -0.5),
        'moe_w3': nrm(ks[31], (DEPTH, N_EXPERTS, D, D_EXPERT), D ** -0.5),
        'moe_w2': nrm(ks[32], (DEPTH, N_EXPERTS, D_EXPERT, D), D_EXPERT ** -0.5),
    }


def reference(x, c, norm_mix_g, norm_ffn_g, norm_final_g, ada_mix_w, ada_mix_b, ada_ffn_w, ada_ffn_b,
              s5_w_in, s5_lam_re, s5_lam_im, s5_log_step, s5_b_re, s5_b_im, s5_c_re, s5_c_im, s5_d,
              s5_w_glu, s5_b_glu, s5_w_out, gm_w_in, gm_b_in, gm_ln_g, gm_ln_b, gm_ws, gm_bs, gm_w_out,
              router_w, router_bias, moe_w1, moe_w3, moe_w2):
    for i in range(DEPTH):
        j = i // N_MIXERS
        shift, scale, gate = _adaln(c, ada_mix_w[i], ada_mix_b[i])
        hn = _rms_modulate(x, norm_mix_g[i], shift, scale)
        if i % N_MIXERS == 0:
            y = _s5_mixer(hn, s5_w_in[j], s5_lam_re[j], s5_lam_im[j], s5_log_step[j], s5_b_re[j], s5_b_im[j],
                          s5_c_re[j], s5_c_im[j], s5_d[j], s5_w_glu[j], s5_b_glu[j], s5_w_out[j])
        else:
            y = _gmlp_mixer(hn, gm_w_in[j], gm_b_in[j], gm_ln_g[j], gm_ln_b[j], gm_ws[j], gm_bs[j], gm_w_out[j])
        x = x + gate[:, None, :] * y
        shift, scale, gate = _adaln(c, ada_ffn_w[i], ada_ffn_b[i])
        hn = _rms_modulate(x, norm_ffn_g[i], shift, scale)
        x = x + gate[:, None, :] * _moe(hn, router_w, router_bias, moe_w1[i], moe_w3[i], moe_w2[i])
    return _rmsnorm(x, norm_final_g)
```

```python
import functools

import jax
import jax.numpy as jnp
from jax import lax
from jax.experimental import pallas as pl
from jax.experimental.pallas import tpu as pltpu

EPS = 1e-6
CHUNK = 8
LANES = 128
SEGMENTS = 8
SEG_PITCH_PAD = 8
N_GROUPS_MOE = 4
TOP_K = 2
MOE_TILE = 512
VMEM_LIMIT = 56 * 1024 * 1024

F32 = jnp.float32
BF16 = jnp.bfloat16


def _params(sem):
    return pltpu.CompilerParams(dimension_semantics=sem, vmem_limit_bytes=VMEM_LIMIT)


def _rms_mod(x, geff, shift):
    ms = jnp.mean(x * x, axis=-1, keepdims=True)
    return x * lax.rsqrt(ms + EPS) * geff + shift


def _ada_body(c_ref, w_ref, b_ref, o_ref):
    c = c_ref[...]
    s = c * jax.nn.sigmoid(c)
    o_ref[...] = jnp.dot(s, w_ref[...], preferred_element_type=F32, precision=lax.Precision.HIGHEST) + b_ref[...]


def _adaln(c, w, b):
    depth, d, d3 = w.shape
    bsz = c.shape[0]
    rows = 8
    cp = jnp.zeros((rows, d), F32).at[:bsz].set(c)
    tn = 1024 if d3 % 1024 == 0 else 512
    assert d3 % tn == 0
    out = pl.pallas_call(
        _ada_body,
        grid=(depth, d3 // tn),
        in_specs=[
            pl.BlockSpec((rows, d), lambda l, j: (0, 0)),
            pl.BlockSpec((None, d, tn), lambda l, j: (l, 0, j)),
            pl.BlockSpec((None, 1, tn), lambda l, j: (l, 0, j)),
        ],
        out_specs=pl.BlockSpec((None, rows, tn), lambda l, j: (l, 0, j)),
        out_shape=jax.ShapeDtypeStruct((depth, rows, d3), F32),
        compiler_params=_params(("arbitrary", "arbitrary")),
        name="adaln",
    )(cp, w, b.reshape(depth, 1, d3))
    return out[:, :bsz]


def _mm_body(*refs, epilogue, n_extra):
    a_ref, w_ref = refs[0], refs[1]
    extra = refs[2:2 + n_extra]
    o_ref = refs[2 + n_extra]
    acc = jnp.dot(a_ref[...], w_ref[...], preferred_element_type=F32)
    o_ref[...] = epilogue(acc, *extra).astype(o_ref.dtype)


def _matmul(a, w, *, tm, tn, out_dtype, epilogue, extras=(), name):
    p, m, k = a.shape
    n = w.shape[1]
    tm, tn = min(tm, m), min(tn, n)
    in_specs = [
        pl.BlockSpec((None, tm, k), lambda q, i, j: (q, i, 0)),
        pl.BlockSpec((k, tn), lambda q, i, j: (0, j)),
    ] + [s for _, s in extras]
    return pl.pallas_call(
        functools.partial(_mm_body, epilogue=epilogue, n_extra=len(extras)),
        grid=(p, m // tm, n // tn),
        in_specs=in_specs,
        out_specs=pl.BlockSpec((None, tm, tn), lambda q, i, j: (q, i, j)),
        out_shape=jax.ShapeDtypeStruct((p, m, n), out_dtype),
        compiler_params=_params(("arbitrary", "arbitrary", "arbitrary")),
        name=name,
    )(a, w, *[x for x, _ in extras])


def _ep_identity(acc):
    return acc


def _ep_glu(acc, y_ref, b_ref):
    return y_ref[...].astype(F32) * jax.nn.sigmoid(acc + b_ref[...])


def _ep_bias_gelu(acc, b_ref):
    return jax.nn.gelu(acc + b_ref[...])


def _ep_residual(acc, x_ref, g_ref):
    return x_ref[...] + g_ref[...] * acc


def _norm_plane_body(x_ref, g_ref, s_ref, o_ref):
    for t in range(CHUNK):
        o_ref[t] = _rms_mod(x_ref[:, t, :], g_ref[...], s_ref[...]).astype(o_ref.dtype)


def _norm_to_planes(x3, geff, shift, *, chunks_per_batch):
    nc, _, d = x3.shape
    tc = min(64, chunks_per_batch)
    per = chunks_per_batch // tc
    vec = pl.BlockSpec((None, 1, d), lambda i: (i // per, 0, 0))
    return pl.pallas_call(
        _norm_plane_body,
        grid=(nc // tc,),
        in_specs=[pl.BlockSpec((tc, CHUNK, d), lambda i: (i, 0, 0)), vec, vec],
        out_specs=pl.BlockSpec((CHUNK, tc, d), lambda i: (0, i, 0)),
        out_shape=jax.ShapeDtypeStruct((CHUNK, nc, d), BF16),
        compiler_params=_params(("arbitrary",)),
        name="s5_norm_planes",
    )(x3, geff, shift)


def _cmul_add(ar, ai, xr, xi, vr, vi):
    return ar * xr - ai * xi + vr, ar * xi + ai * xr + vi


def _ssm_body(u_ref, wa_ref, wc_ref, a1_ref, aseg_ref, d_ref, y_ref, xf_ref, xb_ref, *, rows_per_dot):
    cb = u_ref.shape[1]
    ks = cb // SEGMENTS
    half = xf_ref.shape[-1]
    q = half // 2

    def u_rows(r0, nr):
        return jnp.concatenate([u_ref[t, pl.ds(r0, nr), :] for t in range(CHUNK)], axis=1)

    segs_per_dot = min(SEGMENTS, max(1, rows_per_dot // ks))
    for s0 in range(0, SEGMENTS, segs_per_dot):
        res = jnp.dot(u_rows(s0 * ks, segs_per_dot * ks), wa_ref[...], preferred_element_type=F32)
        for s in range(segs_per_dot):
            xf_ref[s0 + s, 0:ks, :] = res[s * ks:(s + 1) * ks, :half]
            xb_ref[s0 + s, 0:ks, :] = res[s * ks:(s + 1) * ks, half:]

    afr, afi, abr, abi = (a1_ref[i:i + 1, :] for i in range(4))

    def p1(j, carry):
        cfr, cfi, cbr, cbi = carry
        jb = ks - 1 - j
        vf = xf_ref[:, j, :]
        vb = xb_ref[:, jb, :]
        xf_ref[:, j, :] = jnp.concatenate([cfr, cfi], axis=1)
        xb_ref[:, jb, :] = jnp.concatenate([cbr, cbi], axis=1)
        nfr, nfi = _cmul_add(afr, afi, cfr, cfi, vf[:, :q], vf[:, q:])
        nbr, nbi = _cmul_add(abr, abi, cbr, cbi, vb[:, :q], vb[:, q:])
        return nfr, nfi, nbr, nbi

    z = jnp.zeros((SEGMENTS, q), F32)
    efr, efi, ebr, ebi = lax.fori_loop(0, ks, p1, (z, z, z, z))

    sfr, sfi, sbr, sbi = (aseg_ref[i:i + 1, :] for i in range(4))
    z1 = jnp.zeros((1, q), F32)
    gfr, gfi = [z1], [z1]
    for s in range(SEGMENTS - 1):
        r, i = _cmul_add(sfr, sfi, gfr[-1], gfi[-1], efr[s:s + 1], efi[s:s + 1])
        gfr.append(r)
        gfi.append(i)
    gbr, gbi = [z1], [z1]
    for s in range(SEGMENTS - 1, 0, -1):
        r, i = _cmul_add(sbr, sbi, gbr[0], gbi[0], ebr[s:s + 1], ebi[s:s + 1])
        gbr.insert(0, r)
        gbi.insert(0, i)
    gfr, gfi, gbr, gbi = (jnp.concatenate(g, axis=0) for g in (gfr, gfi, gbr, gbi))

    def p2(j, carry):
        pfr, pfi, pbr, pbi = carry
        jb = ks - 1 - j
        cf = xf_ref[:, j, :]
        cb_ = xb_ref[:, jb, :]
        xf_ref[:, j, :] = jnp.concatenate([cf[:, :q] + pfr * gfr - pfi * gfi, cf[:, q:] + pfr * gfi + pfi * gfr], axis=1)
        xb_ref[:, jb, :] = jnp.concatenate([cb_[:, :q] + pbr * gbr - pbi * gbi, cb_[:, q:] + pbr * gbi + pbi * gbr], axis=1)
        return (pfr * afr - pfi * afi, pfr * afi + pfi * afr, pbr * abr - pbi * abi, pbr * abi + pbi * abr)

    one = jnp.ones((1, q), F32)
    lax.fori_loop(0, ks, p2, (one, jnp.zeros((1, q), F32), one, jnp.zeros((1, q), F32)))

    lw = d_ref.shape[-1]
    for s0 in range(0, SEGMENTS, segs_per_dot):
        r0, nr = s0 * ks, segs_per_dot * ks
        xf = jnp.concatenate([xf_ref[s0 + s, 0:ks, :] for s in range(segs_per_dot)], axis=0)
        xb = jnp.concatenate([xb_ref[s0 + s, 0:ks, :] for s in range(segs_per_dot)], axis=0)
        lhs = jnp.concatenate([u_rows(r0, nr), xf.astype(BF16), xb.astype(BF16)], axis=1)
        acc = jnp.dot(lhs, wc_ref[...], preferred_element_type=F32)
        for t in range(CHUNK):
            yt = acc[:, t * lw:(t + 1) * lw] + d_ref[...] * u_ref[t, pl.ds(r0, nr), :].astype(F32)
            y_ref[t, pl.ds(r0, nr), :] = jax.nn.gelu(yt).astype(y_ref.dtype)


def _s5_ssm(u8, wa, wc, a1, aseg, dvec, *, chunks_per_batch):
    _, nc, w = u8.shape
    nslab = w // LANES
    nb = nc // chunks_per_batch
    cb = chunks_per_batch
    ks = cb // SEGMENTS
    half = wa.shape[-1] // 2
    body = functools.partial(_ssm_body, rows_per_dot=512)
    return pl.pallas_call(
        body,
        grid=(nslab, nb),
        in_specs=[
            pl.BlockSpec((CHUNK, cb, LANES), lambda s, b: (0, b, s)),
            pl.BlockSpec((None,) + wa.shape[1:], lambda s, b: (s, 0, 0)),
            pl.BlockSpec((None,) + wc.shape[1:], lambda s, b: (s, 0, 0)),
            pl.BlockSpec((None,) + a1.shape[1:], lambda s, b: (s, 0, 0)),
            pl.BlockSpec((None,) + aseg.shape[1:], lambda s, b: (s, 0, 0)),
            pl.BlockSpec((1, LANES), lambda s, b: (0, s)),
        ],
        out_specs=pl.BlockSpec((CHUNK, cb, LANES), lambda s, b: (0, b, s)),
        out_shape=jax.ShapeDtypeStruct(u8.shape, BF16),
        scratch_shapes=[
            pltpu.VMEM((SEGMENTS, ks + SEG_PITCH_PAD, half), F32),
            pltpu.VMEM((SEGMENTS, ks + SEG_PITCH_PAD, half), F32),
        ],
        compiler_params=_params(("arbitrary", "arbitrary")),
        name="s5_ssm",
    )(u8, wa, wc, a1, aseg, dvec)


def _s5_weights(lam_re, lam_im, log_step, b_re, b_im, c_re, c_im, *, seg_rows):
    _, g, p = lam_re.shape
    h = b_re.shape[-1]
    gl = LANES // h
    s = g // gl
    t = CHUNK
    lam = lax.complex(lam_re.astype(F32), lam_im.astype(F32))
    zed = lam * jnp.exp(log_step.astype(F32))[..., None]
    lam_bar = jnp.exp(zed)
    bbar = ((lam_bar - 1.0) / lam)[..., None] * lax.complex(b_re.astype(F32), b_im.astype(F32))
    cmat = lax.complex(c_re.astype(F32), c_im.astype(F32))
    pw = jnp.exp(zed[:, None] * jnp.arange(t + 1, dtype=F32)[None, :, None, None])
    pw_down = jnp.exp(zed[:, None] * (t - jnp.arange(t + 1, dtype=F32))[None, :, None, None])
    eye = jnp.eye(gl, dtype=F32)

    def slab(x):
        return x.reshape(x.shape[:1] + (s, gl) + x.shape[2:])

    ma_f = pw_down[0, 1:, :, :, None] * bbar[0][None]
    ma_b = pw[1, :t, :, :, None] * bbar[1][None]
    parts = [slab(x) for x in (ma_f.real, ma_f.imag, ma_b.real, ma_b.imag)]
    wa = jnp.stack([jnp.einsum('tmgph,gk->mtghkp', x, eye) for x in parts], axis=4)
    wa = wa.reshape(s, t * gl * h, 4 * gl * p)

    kf = jnp.einsum('gyp,dgp,gph->dgyh', cmat[0], pw[0, :t], bbar[0]).real
    kb = jnp.einsum('gyp,dgp,gph->dgyh', cmat[1], pw[1, :t], bbar[1]).real
    st = jnp.arange(t)
    dlt = st[None, :] - st[:, None]
    coef = (jnp.where((dlt >= 0)[..., None, None, None], kf[jnp.clip(dlt, 0, t - 1)], 0.0)
            + jnp.where((dlt <= 0)[..., None, None, None], kb[jnp.clip(-dlt, 0, t - 1)], 0.0))
    coef = coef.reshape(t, t, s, gl, h, h)
    w_u = jnp.einsum('atmgyh,gk->maghtky', coef, eye).reshape(s, t * gl * h, t * gl * h)

    cl_f = cmat[0][None] * pw[0, 1:][:, :, None, :]
    cl_b = cmat[1][None] * pw_down[1, :t][:, :, None, :]
    def state_rows(x):
        x = x.reshape(t, s, gl, h, p)
        return jnp.einsum('tmgyp,gk->mgptky', x, eye).reshape(s, gl * p, t * gl * h)
    wc = jnp.concatenate([w_u, state_rows(cl_f.real), state_rows(-cl_f.imag),
                          state_rows(cl_b.real), state_rows(-cl_b.imag)], axis=1)

    def lanes(x):
        x = x.reshape(2, s, gl * p)
        return jnp.stack([x[0].real, x[0].imag, x[1].real, x[1].imag], axis=1)
    a1 = lanes(pw[:, t])
    aseg = lanes(jnp.exp(zed * float(t * seg_rows)))
    return wa.astype(BF16), wc.astype(BF16), a1, aseg


def _s5_out_body(a_ref, w_ref, x_ref, g_ref, o_ref):
    tm = a_ref.shape[1]
    a = a_ref[...].reshape(CHUNK * tm, a_ref.shape[2])
    acc = jnp.dot(a, w_ref[...], preferred_element_type=F32)
    for t in range(CHUNK):
        o_ref[:, t, :] = x_ref[:, t, :] + g_ref[...] * acc[t * tm:(t + 1) * tm]


def _s5_out(z8, w, x3, gate, *, chunks_per_batch):
    _, nc, k = z8.shape
    d = w.shape[1]
    tm = min(128, chunks_per_batch)
    tn = min(512, d)
    per = chunks_per_batch // tm
    return pl.pallas_call(
        _s5_out_body,
        grid=(nc // tm, d // tn),
        in_specs=[
            pl.BlockSpec((CHUNK, tm, k), lambda i, j: (0, i, 0)),
            pl.BlockSpec((k, tn), lambda i, j: (0, j)),
            pl.BlockSpec((tm, CHUNK, tn), lambda i, j: (i, 0, j)),
            pl.BlockSpec((None, 1, tn), lambda i, j: (i // per, 0, j)),
        ],
        out_specs=pl.BlockSpec((tm, CHUNK, tn), lambda i, j: (i, 0, j)),
        out_shape=jax.ShapeDtypeStruct(x3.shape, F32),
        compiler_params=_params(("arbitrary", "arbitrary")),
        name="s5_out",
    )(z8, w, x3, gate)


def _gm_gate_body(z_ref, lng_ref, lnb_ref, ws_ref, bs_ref, o_ref, *, chunk):
    tm, w2 = z_ref.shape
    w = w2 // 2
    heads = ws_ref.shape[0]
    hd = w // heads
    v = z_ref[:, w:].astype(F32)
    mu = jnp.mean(v, axis=-1, keepdims=True)
    vc = v - mu
    var = jnp.mean(vc * vc, axis=-1, keepdims=True)
    vn = (vc * lax.rsqrt(var + EPS) * lng_ref[...] + lnb_ref[...]).astype(BF16)
    for c in range(tm // chunk):
        rows = slice(c * chunk, (c + 1) * chunk)
        for hh in range(heads):
            cols = slice(hh * hd, (hh + 1) * hd)
            sv = jnp.dot(ws_ref[hh], vn[rows, cols], preferred_element_type=F32) + bs_ref[:, hh:hh + 1]
            o_ref[rows, cols] = (z_ref[rows, cols].astype(F32) * sv).astype(o_ref.dtype)


def _gm_gate(z, ln_g, ln_b, ws, bs, *, chunk):
    n, w2 = z.shape
    w = w2 // 2
    tm = 2 * chunk
    full = lambda a: pl.BlockSpec(a.shape, lambda i: (0,) * a.ndim)
    return pl.pallas_call(
        functools.partial(_gm_gate_body, chunk=chunk),
        grid=(n // tm,),
        in_specs=[pl.BlockSpec((tm, w2), lambda i: (i, 0)), full(ln_g), full(ln_b), full(ws), full(bs)],
        out_specs=pl.BlockSpec((tm, w), lambda i: (i, 0)),
        out_shape=jax.ShapeDtypeStruct((n, w), BF16),
        compiler_params=_params(("arbitrary",)),
        name="gmlp_gate",
    )(z, ln_g, ln_b, ws, bs)


def _top2_sum(a, b, c, d):
    m1, n1 = jnp.maximum(a, b), jnp.minimum(a, b)
    m2, n2 = jnp.maximum(c, d), jnp.minimum(c, d)
    return jnp.maximum(m1, m2) + jnp.maximum(jnp.minimum(m1, m2), jnp.maximum(n1, n2))


def _first_argmax(vals):
    best, idx = vals[0], jnp.zeros(vals[0].shape, jnp.int32)
    for j in range(1, len(vals)):
        upd = vals[j] > best
        idx = jnp.where(upd, j, idx)
        best = jnp.where(upd, vals[j], best)
    return idx, best


def _ffn_prep_body(x_ref, g_ref, s_ref, rwt_ref, rb_ref, hn_ref, e_ref, gw_ref, rank_ref, cnt_ref, carry_ref):
    i = pl.program_id(0)
    n_e = rwt_ref.shape[0]
    epg = n_e // N_GROUPS_MOE
    tm = x_ref.shape[0]

    @pl.when(i == 0)
    def _():
        carry_ref[...] = jnp.zeros_like(carry_ref)

    hn = _rms_mod(x_ref[...], g_ref[...], s_ref[...])
    hn_ref[...] = hn
    logits = lax.dot_general(rwt_ref[...], hn, (((1,), (1,)), ((), ())),
                             preferred_element_type=F32, precision=lax.Precision.HIGHEST)
    score = jax.nn.sigmoid(logits)
    sel = score + rb_ref[...]

    row = lambda a, r: a[r:r + 1, :]
    gscore = [_top2_sum(*[row(sel, g * epg + j) for j in range(epg)]) for g in range(N_GROUPS_MOE)]
    bg, _ = _first_argmax(gscore)
    pick = lambda a, j: functools.reduce(
        lambda acc, g: jnp.where(bg == g, row(a, g * epg + j), acc), range(1, N_GROUPS_MOE), row(a, j))
    sel_in = [pick(sel, j) for j in range(epg)]
    sc_in = [pick(score, j) for j in range(epg)]
    i1, _ = _first_argmax(sel_in)
    neg = jnp.full_like(sel_in[0], -jnp.inf)
    i2, _ = _first_argmax([jnp.where(i1 == j, neg, sel_in[j]) for j in range(epg)])
    take = lambda vals, idx: functools.reduce(lambda acc, j: jnp.where(idx == j, vals[j], acc), range(1, epg), vals[0])
    w1, w2 = take(sc_in, i1), take(sc_in, i2)
    den = w1 + w2
    e1, e2 = bg * epg + i1, bg * epg + i2
    e_ref[...] = jnp.concatenate([e1, e2], axis=0)
    gw_ref[...] = jnp.concatenate([w1 / den, w2 / den], axis=0)

    eid = lax.broadcasted_iota(jnp.int32, (n_e, tm), 0)
    oh1 = (eid == e1).astype(F32)
    oh2 = (eid == e2).astype(F32)
    oh = oh1 + oh2
    tri = (lax.broadcasted_iota(jnp.int32, (tm, tm), 0) <= lax.broadcasted_iota(jnp.int32, (tm, tm), 1)).astype(BF16)
    incl = jnp.dot(oh.astype(BF16), tri, preferred_element_type=F32)
    before = carry_ref[:, 0:1] + incl - oh
    r1 = jnp.sum(oh1 * before, axis=0, keepdims=True)
    r2 = jnp.sum(oh2 * before, axis=0, keepdims=True)
    rank_ref[...] = jnp.concatenate([r1, r2], axis=0).astype(jnp.int32)
    carry_ref[...] = carry_ref[...] + incl[:, tm - 1:tm]
    cnt_ref[...] = carry_ref[...].astype(jnp.int32)


def _ffn_prep(x, geff, shift, rwt, rb, *, seq):
    n, d = x.shape
    n_e = rwt.shape[0]
    tm = min(256, seq)
    per = seq // tm
    vec = pl.BlockSpec((None, 1, d), lambda i: (i // per, 0, 0))
    row2 = pl.BlockSpec((TOP_K, tm), lambda i: (0, i))
    hn, e, gw, rank, cnt = pl.pallas_call(
        _ffn_prep_body,
        grid=(n // tm,),
        in_specs=[pl.BlockSpec((tm, d), lambda i: (i, 0)), vec, vec,
                  pl.BlockSpec((n_e, d), lambda i: (0, 0)), pl.BlockSpec((n_e, 1), lambda i: (0, 0))],
        out_specs=[pl.BlockSpec((tm, d), lambda i: (i, 0)), row2, row2, row2,
                   pl.BlockSpec((n_e, LANES), lambda i: (0, 0))],
        out_shape=[jax.ShapeDtypeStruct((n, d), F32), jax.ShapeDtypeStruct((TOP_K, n), jnp.int32),
                   jax.ShapeDtypeStruct((TOP_K, n), F32), jax.ShapeDtypeStruct((TOP_K, n), jnp.int32),
                   jax.ShapeDtypeStruct((n_e, LANES), jnp.int32)],
        scratch_shapes=[pltpu.VMEM((n_e, LANES), F32)],
        compiler_params=_params(("arbitrary",)),
        name="ffn_prep",
    )(x, geff, shift, rwt, rb)
    return hn, e, gw, rank, cnt[:, 0]


def _moe_body(te_ref, nrows_ref, src_ref, dst_ref, hn_hbm, w1_ref, w3_ref, w2_ref, y_hbm,
              xbuf, xb_ref, acc_ref, gsem, ssem):
    i = pl.program_id(0)
    f = pl.program_id(1)
    nf = pl.num_programs(1)
    nt = pl.num_programs(0)
    tm = xb_ref.shape[0]
    slot = i % 2
    n_rows = nrows_ref[i]
    used = n_rows > 0

    def gather_copy(row, r, sl):
        return pltpu.make_async_copy(hn_hbm.at[pl.ds(row, 1)], xbuf.at[sl, pl.ds(r, 1)], gsem.at[sl])

    def start_gather(tile, sl):
        def body(r, c):
            gather_copy(src_ref[tile * tm + r], r, sl).start()
            return c
        lax.fori_loop(0, tm, body, 0)

    def for_rows(n, fn):
        def body(r, c):
            fn(r)
            return c
        lax.fori_loop(0, n, body, 0)

    @pl.when((i == 0) & (f == 0) & used)
    def _():
        start_gather(0, 0)

    @pl.when((f == 0) & used)
    def _():
        for_rows(tm, lambda r: gather_copy(0, r, slot).wait())
        xb_ref[...] = xbuf[slot].astype(BF16)

    @pl.when((f == 0) & (nrows_ref[jnp.minimum(i + 1, nt - 1)] > 0) & (i + 1 < nt))
    def _():
        start_gather(i + 1, 1 - slot)

    @pl.when(used)
    def _():
        xb = xb_ref[...]
        h1 = jnp.dot(xb, w1_ref[...], preferred_element_type=F32)
        h3 = jnp.dot(xb, w3_ref[...], preferred_element_type=F32)
        hh = (h1 * jax.nn.sigmoid(h1) * h3).astype(BF16)
        part = jnp.dot(hh, w2_ref[...], preferred_element_type=F32)

        @pl.when(f == 0)
        def _():
            acc_ref[...] = part

        @pl.when(f > 0)
        def _():
            acc_ref[...] += part

    @pl.when((f == nf - 1) & used)
    def _():
        def scatter_copy(r):
            return pltpu.make_async_copy(acc_ref.at[pl.ds(r, 1)], y_hbm.at[pl.ds(dst_ref[i * tm + r], 1)], ssem)

        for_rows(n_rows, lambda r: scatter_copy(r).start())
        for_rows(n_rows, lambda r: scatter_copy(r).wait())


def _moe_experts(hn, w1, w3, w2, tile_expert, tile_rows, src, dst, *, out_rows):
    n, d = hn.shape
    n_e, _, fdim = w1.shape
    tm = MOE_TILE
    n_tiles = src.shape[0] // tm
    fc = min(256, fdim)
    grid_spec = pltpu.PrefetchScalarGridSpec(
        num_scalar_prefetch=4,
        grid=(n_tiles, fdim // fc),
        in_specs=[
            pl.BlockSpec(memory_space=pl.ANY),
            pl.BlockSpec((None, d, fc), lambda i, f, te, nr, s, t: (te[i], 0, f * (nr[i] > 0))),
            pl.BlockSpec((None, d, fc), lambda i, f, te, nr, s, t: (te[i], 0, f * (nr[i] > 0))),
            pl.BlockSpec((None, fc, d), lambda i, f, te, nr, s, t: (te[i], f * (nr[i] > 0), 0)),
        ],
        out_specs=pl.BlockSpec(memory_space=pl.ANY),
        scratch_shapes=[
            pltpu.VMEM((2, tm, d), F32),
            pltpu.VMEM((tm, d), BF16),
            pltpu.VMEM((tm, d), F32),
            pltpu.SemaphoreType.DMA((2,)),
            pltpu.SemaphoreType.DMA(()),
        ],
    )
    return pl.pallas_call(
        _moe_body,
        grid_spec=grid_spec,
        out_shape=jax.ShapeDtypeStruct((out_rows, d), F32),
        compiler_params=_params(("arbitrary", "arbitrary")),
        name="moe_experts",
    )(tile_expert, tile_rows, src, dst, hn, w1, w3, w2)


def _moe_plan(e, rank, cnt, n_tiles_max):
    tm = MOE_TILE
    n = e.shape[1]
    tiles_per = (cnt + tm - 1) // tm
    ends = jnp.cumsum(tiles_per)
    off = (ends - tiles_per) * tm
    pos = off[e] + rank
    tile = jnp.arange(n_tiles_max, dtype=jnp.int32)
    tile_expert = jnp.minimum(jnp.searchsorted(ends, tile, side='right'), cnt.shape[0] - 1).astype(jnp.int32)
    tile_rows = jnp.clip(off[tile_expert] + cnt[tile_expert] - tile * tm, 0, tm)
    tile_rows = jnp.where(tile < ends[-1], tile_rows, 0).astype(jnp.int32)
    tok = jnp.broadcast_to(jnp.arange(n, dtype=jnp.int32)[None], (TOP_K, n))
    rows = n_tiles_max * tm
    src = jnp.zeros((rows,), jnp.int32).at[pos.reshape(-1)].set(tok.reshape(-1))
    home = tok + jnp.arange(TOP_K, dtype=jnp.int32)[:, None] * n
    dst = jnp.zeros((rows,), jnp.int32).at[pos.reshape(-1)].set(home.reshape(-1))
    return tile_expert, tile_rows, src, dst


def _combine(x_ref, ya_ref, yb_ref, gw_ref, gate_ref):
    gw = gw_ref[...]
    return x_ref[...] + gate_ref[...] * (gw[:, 0:1] * ya_ref[...] + gw[:, 1:2] * yb_ref[...])


def _combine_norm_body(x_ref, ya_ref, yb_ref, gw_ref, gate_ref, g_ref, s_ref, xo_ref, hn_ref):
    xn = _combine(x_ref, ya_ref, yb_ref, gw_ref, gate_ref)
    xo_ref[...] = xn
    hn_ref[...] = _rms_mod(xn, g_ref[...], s_ref[...]).astype(hn_ref.dtype)


def _combine_final_body(x_ref, ya_ref, yb_ref, gw_ref, gate_ref, g_ref, o_ref):
    xn = _combine(x_ref, ya_ref, yb_ref, gw_ref, gate_ref)
    ms = jnp.mean(xn * xn, axis=-1, keepdims=True)
    o_ref[...] = xn * lax.rsqrt(ms + EPS) * g_ref[...]


def _ffn_combine(x, y2, gw_t, gate, norm_args, *, seq, half_rows, final):
    n, d = x.shape
    tm = min(256, seq)
    per = seq // tm
    hoff = half_rows // tm
    tile = pl.BlockSpec((tm, d), lambda i: (i, 0))
    vec = pl.BlockSpec((None, 1, d), lambda i: (i // per, 0, 0))
    in_specs = [tile, tile, pl.BlockSpec((tm, d), lambda i: (i + hoff, 0)),
                pl.BlockSpec((tm, TOP_K), lambda i: (i, 0)), vec]
    if final:
        in_specs.append(pl.BlockSpec((1, d), lambda i: (0, 0)))
        return pl.pallas_call(
            _combine_final_body, grid=(n // tm,), in_specs=in_specs, out_specs=tile,
            out_shape=jax.ShapeDtypeStruct((n, d), F32),
            compiler_params=_params(("arbitrary",)), name="ffn_combine_final",
        )(x, y2, y2, gw_t, gate, *norm_args)
    in_specs += [vec, vec]
    return pl.pallas_call(
        _combine_norm_body, grid=(n // tm,), in_specs=in_specs, out_specs=[tile, tile],
        out_shape=[jax.ShapeDtypeStruct((n, d), F32), jax.ShapeDtypeStruct((n, d), BF16)],
        compiler_params=_params(("arbitrary",)), name="ffn_combine_norm",
    )(x, y2, y2, gw_t, gate, *norm_args)


def _moe_layer(x, geff, shift, rwt, rb, w1, w3, w2, *, seq):
    n, _ = x.shape
    n_e = rwt.shape[0]
    hn, e, gw, rank, cnt = _ffn_prep(x, geff, shift, rwt, rb, seq=seq)
    n_tiles_max = (TOP_K * n) // MOE_TILE + n_e
    tile_expert, tile_rows, src, dst = _moe_plan(e, rank, cnt, n_tiles_max)
    y2 = _moe_experts(hn, w1, w3, w2, tile_expert, tile_rows, src, dst, out_rows=TOP_K * n)
    return y2, gw.T, n


def kernel(x, c, norm_mix_g, norm_ffn_g, norm_final_g, ada_mix_w, ada_mix_b, ada_ffn_w, ada_ffn_b, s5_w_in, s5_lam_re, s5_lam_im, s5_log_step, s5_b_re, s5_b_im, s5_c_re, s5_c_im, s5_d, s5_w_glu, s5_b_glu, s5_w_out, gm_w_in, gm_b_in, gm_ln_g, gm_ln_b, gm_ws, gm_bs, gm_w_out, router_w, router_bias, moe_w1, moe_w3, moe_w2):
    bsz, seq, d = x.shape
    n = bsz * seq
    depth = norm_mix_g.shape[0]
    assert depth == 2 and seq % (CHUNK * SEGMENTS) == 0
    cb = seq // CHUNK
    nc = n // CHUNK

    mods_mix = _adaln(c, ada_mix_w, ada_mix_b)
    mods_ffn = _adaln(c, ada_ffn_w, ada_ffn_b)

    def split(m, g):
        shift, scale, gate = jnp.split(m, 3, axis=-1)
        return (g[None] * (1.0 + scale))[:, None], shift[:, None], gate[:, None]

    rwt = router_w.T
    rb = router_bias.reshape(-1, 1)
    xf = x.reshape(n, d)

    geff, shift, gate = split(mods_mix[0], norm_mix_g[0])
    x3 = xf.reshape(nc, CHUNK, d)
    hn8 = _norm_to_planes(x3, geff, shift, chunks_per_batch=cb)
    u8 = _matmul(hn8, s5_w_in[0].astype(BF16), tm=1024, tn=1024, out_dtype=BF16, epilogue=_ep_identity, name="s5_in")
    wa, wc, a1, aseg = _s5_weights(s5_lam_re[0], s5_lam_im[0], s5_log_step[0], s5_b_re[0], s5_b_im[0],
                                   s5_c_re[0], s5_c_im[0], seg_rows=cb // SEGMENTS)
    y8 = _s5_ssm(u8, wa, wc, a1, aseg, s5_d[0].reshape(1, -1), chunks_per_batch=cb)
    w = y8.shape[-1]
    tmg, tng = min(1024, nc), min(1024, w)
    z8 = _matmul(y8, s5_w_glu[0].astype(BF16), tm=tmg, tn=tng, out_dtype=BF16, epilogue=_ep_glu,
                 extras=[(y8, pl.BlockSpec((None, tmg, tng), lambda q, i, j: (q, i, j))),
                         (s5_b_glu[0].reshape(1, -1), pl.BlockSpec((1, tng), lambda q, i, j: (0, j)))],
                 name="s5_glu")
    x1 = _s5_out(z8, s5_w_out[0].astype(BF16), x3, gate, chunks_per_batch=cb).reshape(n, d)

    geff, shift, gate = split(mods_ffn[0], norm_ffn_g[0])
    y2, gw_t, half_rows = _moe_layer(x1, geff, shift, rwt, rb, moe_w1[0].astype(BF16), moe_w3[0].astype(BF16),
                                     moe_w2[0].astype(BF16), seq=seq)
    geff1, shift1, gate1 = split(mods_mix[1], norm_mix_g[1])
    x2, hn = _ffn_combine(x1, y2, gw_t, gate, (geff1, shift1), seq=seq, half_rows=half_rows, final=False)

    w2n = gm_w_in.shape[-1]
    tm_in, tn_in = min(1024, n), min(1024, w2n)
    zz = _matmul(hn[None], gm_w_in[0].astype(BF16), tm=tm_in, tn=tn_in, out_dtype=BF16, epilogue=_ep_bias_gelu,
                 extras=[(gm_b_in[0].reshape(1, -1), pl.BlockSpec((1, tn_in), lambda q, i, j: (0, j)))],
                 name="gmlp_in")[0]
    chunk = gm_ws.shape[-1]
    gated = _gm_gate(zz, gm_ln_g[0].reshape(1, -1), gm_ln_b[0].reshape(1, -1), gm_ws[0].astype(BF16),
                     gm_bs[0].T, chunk=chunk)
    tm_o, tn_o = min(1024, seq), min(512, d)
    per = seq // tm_o
    x3_ = _matmul(gated[None], gm_w_out[0].astype(BF16), tm=tm_o, tn=tn_o, out_dtype=F32, epilogue=_ep_residual,
                  extras=[(x2[None], pl.BlockSpec((None, tm_o, tn_o), lambda q, i, j: (q, i, j))),
                          (gate1, pl.BlockSpec((None, 1, tn_o), lambda q, i, j: (i // per, 0, j)))],
                  name="gmlp_out")[0]

    geff, shift, gate = split(mods_ffn[1], norm_ffn_g[1])
    y2, gw_t, half_rows = _moe_layer(x3_, geff, shift, rwt, rb, moe_w1[1].astype(BF16), moe_w3[1].astype(BF16),
                                     moe_w2[1].astype(BF16), seq=seq)
    out = _ffn_combine(x3_, y2, gw_t, gate, (norm_final_g.reshape(1, -1),), seq=seq, half_rows=half_rows, final=True)
    return out.reshape(bsz, seq, d)
```

```python
import functools

import jax
import jax.numpy as jnp
from jax import lax
from jax.experimental import pallas as pl
from jax.experimental.pallas import tpu as pltpu

EPS = 1e-6
CHUNK = 8
LANES = 128
SEGMENTS = 8
SEG_PITCH_PAD = 8
N_GROUPS_MOE = 4
TOP_K = 2
MOE_TILE = 256
VMEM_LIMIT = 56 * 1024 * 1024

F32 = jnp.float32
BF16 = jnp.bfloat16


def _params(sem):
    return pltpu.CompilerParams(dimension_semantics=sem, vmem_limit_bytes=VMEM_LIMIT)


def _rms_mod(x, geff, shift):
    ms = jnp.mean(x * x, axis=-1, keepdims=True)
    return x * lax.rsqrt(ms + EPS) * geff + shift


def _ada_body(c_ref, w_ref, b_ref, o_ref):
    c = c_ref[...]
    s = c * jax.nn.sigmoid(c)
    o_ref[...] = jnp.dot(s, w_ref[...], preferred_element_type=F32, precision=lax.Precision.HIGHEST) + b_ref[...]


def _adaln(c, w, b):
    depth, d, d3 = w.shape
    bsz = c.shape[0]
    rows = 8
    cp = jnp.zeros((rows, d), F32).at[:bsz].set(c)
    tn = 1024 if d3 % 1024 == 0 else 512
    assert d3 % tn == 0
    out = pl.pallas_call(
        _ada_body,
        grid=(depth, d3 // tn),
        in_specs=[
            pl.BlockSpec((rows, d), lambda l, j: (0, 0)),
            pl.BlockSpec((None, d, tn), lambda l, j: (l, 0, j)),
            pl.BlockSpec((None, 1, tn), lambda l, j: (l, 0, j)),
        ],
        out_specs=pl.BlockSpec((None, rows, tn), lambda l, j: (l, 0, j)),
        out_shape=jax.ShapeDtypeStruct((depth, rows, d3), F32),
        compiler_params=_params(("arbitrary", "arbitrary")),
        name="adaln",
    )(cp, w, b.reshape(depth, 1, d3))
    return out[:, :bsz]


def _mm_body(*refs, epilogue, n_extra):
    a_ref, w_ref = refs[0], refs[1]
    extra = refs[2:2 + n_extra]
    o_ref = refs[2 + n_extra]
    acc = jnp.dot(a_ref[...], w_ref[...], preferred_element_type=F32)
    o_ref[...] = epilogue(acc, *extra).astype(o_ref.dtype)


def _matmul(a, w, *, tm, tn, out_dtype, epilogue, extras=(), name):
    p, m, k = a.shape
    n = w.shape[1]
    tm, tn = min(tm, m), min(tn, n)
    in_specs = [
        pl.BlockSpec((None, tm, k), lambda q, i, j: (q, i, 0)),
        pl.BlockSpec((k, tn), lambda q, i, j: (0, j)),
    ] + [s for _, s in extras]
    return pl.pallas_call(
        functools.partial(_mm_body, epilogue=epilogue, n_extra=len(extras)),
        grid=(p, m // tm, n // tn),
        in_specs=in_specs,
        out_specs=pl.BlockSpec((None, tm, tn), lambda q, i, j: (q, i, j)),
        out_shape=jax.ShapeDtypeStruct((p, m, n), out_dtype),
        compiler_params=_params(("arbitrary", "arbitrary", "arbitrary")),
        name=name,
    )(a, w, *[x for x, _ in extras])


def _ep_identity(acc):
    return acc


def _ep_glu(acc, y_ref, b_ref):
    return y_ref[...].astype(F32) * jax.nn.sigmoid(acc + b_ref[...])


def _ep_bias_gelu(acc, b_ref):
    return jax.nn.gelu(acc + b_ref[...])


def _ep_residual(acc, x_ref, g_ref):
    return x_ref[...] + g_ref[...] * acc


def _norm_plane_body(x_hbm, g_ref, s_ref, o_ref, buf, sem):
    i = pl.program_id(0)
    nsteps = pl.num_programs(0)
    tc = buf.shape[2]
    slot = i % 2

    def fetch(step, sl, t):
        return pltpu.make_async_copy(x_hbm.at[pl.ds(step * tc, tc), t], buf.at[sl, t], sem.at[sl, t])

    @pl.when(i == 0)
    def _():
        for t in range(CHUNK):
            fetch(0, 0, t).start()

    @pl.when(i + 1 < nsteps)
    def _():
        for t in range(CHUNK):
            fetch(i + 1, 1 - slot, t).start()

    for t in range(CHUNK):
        fetch(i, slot, t).wait()
        o_ref[t] = _rms_mod(buf[slot, t], g_ref[...], s_ref[...]).astype(o_ref.dtype)


def _norm_to_planes(x3, geff, shift, *, chunks_per_batch):
    nc, _, d = x3.shape
    tc = min(32, chunks_per_batch)
    per = chunks_per_batch // tc
    vec = pl.BlockSpec((None, 1, d), lambda i: (i // per, 0, 0))
    return pl.pallas_call(
        _norm_plane_body,
        grid=(nc // tc,),
        in_specs=[pl.BlockSpec(memory_space=pl.ANY), vec, vec],
        out_specs=pl.BlockSpec((CHUNK, tc, d), lambda i: (0, i, 0)),
        out_shape=jax.ShapeDtypeStruct((CHUNK, nc, d), BF16),
        scratch_shapes=[pltpu.VMEM((2, CHUNK, tc, d), F32), pltpu.SemaphoreType.DMA((2, CHUNK))],
        compiler_params=_params(("arbitrary",)),
        name="s5_norm_planes",
    )(x3, geff, shift)


def _cmul_add(ar, ai, xr, xi, vr, vi):
    return ar * xr - ai * xi + vr, ar * xi + ai * xr + vi


def _ssm_body(u_ref, wac_ref, wcc_ref, a1_ref, aseg_ref, d_ref, y_ref, xf_ref, xb_ref, wa_ref, wc_ref, *,
              rows_per_dot, gl, hsz, psz):
    cb = u_ref.shape[1]
    ks = cb // SEGMENTS
    half = xf_ref.shape[-1]
    q = half // 2

    @pl.when(pl.program_id(1) == 0)
    def _():
        iota = lambda shape, ax: lax.broadcasted_iota(jnp.int32, shape, ax)
        ca, na = wac_ref.shape[1], wa_ref.shape[1]
        dcol = iota((ca, na), 1)
        spread = (iota((ca, na), 0) == (dcol // (gl * psz)) * psz + dcol % psz).astype(BF16)
        rep = jnp.dot(wac_ref[...], spread, preferred_element_type=F32)
        shp = rep.shape
        keep = (iota(shp, 0) // hsz) % gl == (iota(shp, 1) // psz) % gl
        wa_ref[...] = jnp.where(keep, rep, 0.0).astype(wa_ref.dtype)
        cc, nc_ = wcc_ref.shape[1], wc_ref.shape[1]
        dcol = iota((cc, nc_), 1)
        spread = (iota((cc, nc_), 0) == (dcol // (gl * hsz)) * hsz + dcol % hsz).astype(BF16)
        rep = jnp.dot(wcc_ref[...], spread, preferred_element_type=F32)
        shp = rep.shape
        n_tok_rows = CHUNK * gl * hsz
        r = iota(shp, 0)
        row_g = jnp.where(r < n_tok_rows, (r // hsz) % gl, ((r - n_tok_rows) // psz) % gl)
        keep = row_g == (iota(shp, 1) // hsz) % gl
        wc_ref[...] = jnp.where(keep, rep, 0.0).astype(wc_ref.dtype)

    def u_rows(r0, nr):
        return jnp.concatenate([u_ref[t, pl.ds(r0, nr), :] for t in range(CHUNK)], axis=1)

    segs_per_dot = min(SEGMENTS, max(1, rows_per_dot // ks))
    for s0 in range(0, SEGMENTS, segs_per_dot):
        res = jnp.dot(u_rows(s0 * ks, segs_per_dot * ks), wa_ref[...], preferred_element_type=F32)
        for s in range(segs_per_dot):
            xf_ref[s0 + s, 0:ks, :] = res[s * ks:(s + 1) * ks, :half]
            xb_ref[s0 + s, 0:ks, :] = res[s * ks:(s + 1) * ks, half:]

    afr, afi, abr, abi = (a1_ref[i:i + 1, :] for i in range(4))

    def p1(j, carry):
        cfr, cfi, cbr, cbi = carry
        jb = ks - 1 - j
        vf = xf_ref[:, j, :]
        vb = xb_ref[:, jb, :]
        xf_ref[:, j, :] = jnp.concatenate([cfr, cfi], axis=1)
        xb_ref[:, jb, :] = jnp.concatenate([cbr, cbi], axis=1)
        nfr, nfi = _cmul_add(afr, afi, cfr, cfi, vf[:, :q], vf[:, q:])
        nbr, nbi = _cmul_add(abr, abi, cbr, cbi, vb[:, :q], vb[:, q:])
        return nfr, nfi, nbr, nbi

    z = jnp.zeros((SEGMENTS, q), F32)
    efr, efi, ebr, ebi = lax.fori_loop(0, ks, p1, (z, z, z, z))

    sfr, sfi, sbr, sbi = (aseg_ref[i:i + 1, :] for i in range(4))
    z1 = jnp.zeros((1, q), F32)
    gfr, gfi = [z1], [z1]
    for s in range(SEGMENTS - 1):
        r, i = _cmul_add(sfr, sfi, gfr[-1], gfi[-1], efr[s:s + 1], efi[s:s + 1])
        gfr.append(r)
        gfi.append(i)
    gbr, gbi = [z1], [z1]
    for s in range(SEGMENTS - 1, 0, -1):
        r, i = _cmul_add(sbr, sbi, gbr[0], gbi[0], ebr[s:s + 1], ebi[s:s + 1])
        gbr.insert(0, r)
        gbi.insert(0, i)
    gfr, gfi, gbr, gbi = (jnp.concatenate(g, axis=0) for g in (gfr, gfi, gbr, gbi))

    def p2(j, carry):
        pfr, pfi, pbr, pbi = carry
        jb = ks - 1 - j
        cf = xf_ref[:, j, :]
        cb_ = xb_ref[:, jb, :]
        xf_ref[:, j, :] = jnp.concatenate([cf[:, :q] + pfr * gfr - pfi * gfi, cf[:, q:] + pfr * gfi + pfi * gfr], axis=1)
        xb_ref[:, jb, :] = jnp.concatenate([cb_[:, :q] + pbr * gbr - pbi * gbi, cb_[:, q:] + pbr * gbi + pbi * gbr], axis=1)
        return (pfr * afr - pfi * afi, pfr * afi + pfi * afr, pbr * abr - pbi * abi, pbr * abi + pbi * abr)

    one = jnp.ones((1, q), F32)
    lax.fori_loop(0, ks, p2, (one, jnp.zeros((1, q), F32), one, jnp.zeros((1, q), F32)))

    lw = d_ref.shape[-1]
    for s0 in range(0, SEGMENTS, segs_per_dot):
        r0, nr = s0 * ks, segs_per_dot * ks
        xf = jnp.concatenate([xf_ref[s0 + s, 0:ks, :] for s in range(segs_per_dot)], axis=0)
        xb = jnp.concatenate([xb_ref[s0 + s, 0:ks, :] for s in range(segs_per_dot)], axis=0)
        lhs = jnp.concatenate([u_rows(r0, nr), xf.astype(BF16), xb.astype(BF16)], axis=1)
        acc = jnp.dot(lhs, wc_ref[...], preferred_element_type=F32)
        for t in range(CHUNK):
            yt = acc[:, t * lw:(t + 1) * lw] + d_ref[...] * u_ref[t, pl.ds(r0, nr), :].astype(F32)
            y_ref[t, pl.ds(r0, nr), :] = jax.nn.gelu(yt).astype(y_ref.dtype)


def _s5_ssm(u8, wa, wc, a1, aseg, dvec, *, chunks_per_batch, group, state):
    _, nc, w = u8.shape
    nslab = w // LANES
    nb = nc // chunks_per_batch
    cb = chunks_per_batch
    ks = cb // SEGMENTS
    gl = LANES // group
    half = 2 * gl * state
    tok = CHUNK * LANES
    body = functools.partial(_ssm_body, rows_per_dot=512, gl=gl, hsz=group, psz=state)
    return pl.pallas_call(
        body,
        grid=(nslab, nb),
        in_specs=[
            pl.BlockSpec((CHUNK, cb, LANES), lambda s, b: (0, b, s)),
            pl.BlockSpec((None,) + wa.shape[1:], lambda s, b: (s, 0, 0)),
            pl.BlockSpec((None,) + wc.shape[1:], lambda s, b: (s, 0, 0)),
            pl.BlockSpec((None,) + a1.shape[1:], lambda s, b: (s, 0, 0)),
            pl.BlockSpec((None,) + aseg.shape[1:], lambda s, b: (s, 0, 0)),
            pl.BlockSpec((1, LANES), lambda s, b: (0, s)),
        ],
        out_specs=pl.BlockSpec((CHUNK, cb, LANES), lambda s, b: (0, b, s)),
        out_shape=jax.ShapeDtypeStruct(u8.shape, BF16),
        scratch_shapes=[
            pltpu.VMEM((SEGMENTS, ks + SEG_PITCH_PAD, half), F32),
            pltpu.VMEM((SEGMENTS, ks + SEG_PITCH_PAD, half), F32),
            pltpu.VMEM((tok, 2 * half), BF16),
            pltpu.VMEM((tok + 2 * half, tok), BF16),
        ],
        compiler_params=_params(("arbitrary", "arbitrary")),
        name="s5_ssm",
    )(u8, wa, wc, a1, aseg, dvec)


def _s5_weights(lam_re, lam_im, log_step, b_re, b_im, c_re, c_im, *, seg_rows):
    _, g, p = lam_re.shape
    h = b_re.shape[-1]
    gl = LANES // h
    s = g // gl
    t = CHUNK
    lam = lax.complex(lam_re.astype(F32), lam_im.astype(F32))
    zed = lam * jnp.exp(log_step.astype(F32))[..., None]
    lam_bar = jnp.exp(zed)
    bbar = ((lam_bar - 1.0) / lam)[..., None] * lax.complex(b_re.astype(F32), b_im.astype(F32))
    cmat = lax.complex(c_re.astype(F32), c_im.astype(F32))
    pw = jnp.exp(zed[:, None] * jnp.arange(t + 1, dtype=F32)[None, :, None, None])
    pw_down = jnp.exp(zed[:, None] * (t - jnp.arange(t + 1, dtype=F32))[None, :, None, None])

    ma_f = pw_down[0, 1:, :, :, None] * bbar[0][None]
    ma_b = pw[1, :t, :, :, None] * bbar[1][None]
    wa = jnp.stack([ma_f.real, ma_f.imag, ma_b.real, ma_b.imag], axis=0)
    wa = wa.reshape(4, t, s, gl, p, h).transpose(2, 1, 3, 5, 0, 4).reshape(s, t * gl * h, 4 * p)

    kf = jnp.einsum('gyp,dgp,gph->dgyh', cmat[0], pw[0, :t], bbar[0]).real
    kb = jnp.einsum('gyp,dgp,gph->dgyh', cmat[1], pw[1, :t], bbar[1]).real
    st = jnp.arange(t)
    lag = st[None, None, :] - st[None, :, None]
    sel_f = (lag == st[:, None, None]).astype(F32)
    sel_b = (-lag == st[:, None, None]).astype(F32)
    coef = jnp.einsum('dat,dgyh->atgyh', sel_f, kf) + jnp.einsum('dat,dgyh->atgyh', sel_b, kb)
    w_u = coef.reshape(t, t, s, gl, h, h).transpose(2, 0, 3, 5, 1, 4).reshape(s, t * gl * h, t * h)

    cl_f = cmat[0][None] * pw[0, 1:][:, :, None, :]
    cl_b = cmat[1][None] * pw_down[1, :t][:, :, None, :]
    def state_rows(x):
        return x.reshape(t, s, gl, h, p).transpose(1, 2, 4, 0, 3).reshape(s, gl * p, t * h)
    wc = jnp.concatenate([w_u, state_rows(cl_f.real), state_rows(-cl_f.imag),
                          state_rows(cl_b.real), state_rows(-cl_b.imag)], axis=1)

    def lanes(x):
        x = x.reshape(2, s, gl * p)
        return jnp.stack([x[0].real, x[0].imag, x[1].real, x[1].imag], axis=1)
    a1 = lanes(pw[:, t])
    aseg = lanes(jnp.exp(zed * float(t * seg_rows)))
    return wa.astype(BF16), wc.astype(BF16), a1, aseg


def _s5_out_body(a_ref, w_ref, x_ref, g_ref, o_ref):
    tm = a_ref.shape[1]
    a = a_ref[...].reshape(CHUNK * tm, a_ref.shape[2])
    acc = jnp.dot(a, w_ref[...], preferred_element_type=F32)
    for t in range(CHUNK):
        o_ref[:, t, :] = x_ref[:, t, :] + g_ref[...] * acc[t * tm:(t + 1) * tm]


def _s5_out(z8, w, x3, gate, *, chunks_per_batch):
    _, nc, k = z8.shape
    d = w.shape[1]
    tm = min(128, chunks_per_batch)
    tn = min(512, d)
    per = chunks_per_batch // tm
    return pl.pallas_call(
        _s5_out_body,
        grid=(nc // tm, d // tn),
        in_specs=[
            pl.BlockSpec((CHUNK, tm, k), lambda i, j: (0, i, 0)),
            pl.BlockSpec((k, tn), lambda i, j: (0, j)),
            pl.BlockSpec((tm, CHUNK, tn), lambda i, j: (i, 0, j)),
            pl.BlockSpec((None, 1, tn), lambda i, j: (i // per, 0, j)),
        ],
        out_specs=pl.BlockSpec((tm, CHUNK, tn), lambda i, j: (i, 0, j)),
        out_shape=jax.ShapeDtypeStruct(x3.shape, F32),
        compiler_params=_params(("arbitrary", "arbitrary")),
        name="s5_out",
    )(z8, w, x3, gate)


def _gm_gate_body(z_ref, lng_ref, lnb_ref, ws_ref, bs_ref, o_ref, *, chunk):
    tm, w2 = z_ref.shape
    w = w2 // 2
    heads = ws_ref.shape[0]
    hd = w // heads
    v = z_ref[:, w:].astype(F32)
    mu = jnp.mean(v, axis=-1, keepdims=True)
    vc = v - mu
    var = jnp.mean(vc * vc, axis=-1, keepdims=True)
    vn = (vc * lax.rsqrt(var + EPS) * lng_ref[...] + lnb_ref[...]).astype(BF16)
    for c in range(tm // chunk):
        rows = slice(c * chunk, (c + 1) * chunk)
        for hh in range(heads):
            cols = slice(hh * hd, (hh + 1) * hd)
            sv = jnp.dot(ws_ref[hh], vn[rows, cols], preferred_element_type=F32) + bs_ref[:, hh:hh + 1]
            o_ref[rows, cols] = (z_ref[rows, cols].astype(F32) * sv).astype(o_ref.dtype)


def _gm_gate(z, ln_g, ln_b, ws, bs, *, chunk):
    n, w2 = z.shape
    w = w2 // 2
    tm = 2 * chunk
    full = lambda a: pl.BlockSpec(a.shape, lambda i: (0,) * a.ndim)
    return pl.pallas_call(
        functools.partial(_gm_gate_body, chunk=chunk),
        grid=(n // tm,),
        in_specs=[pl.BlockSpec((tm, w2), lambda i: (i, 0)), full(ln_g), full(ln_b), full(ws), full(bs)],
        out_specs=pl.BlockSpec((tm, w), lambda i: (i, 0)),
        out_shape=jax.ShapeDtypeStruct((n, w), BF16),
        compiler_params=_params(("arbitrary",)),
        name="gmlp_gate",
    )(z, ln_g, ln_b, ws, bs)


def _pack_bf16_pairs(x):
    half = x.shape[1] // 2
    bits = lambda v: lax.bitcast_convert_type(v.astype(BF16).astype(F32), jnp.int32)
    return bits(x[:, half:]) | lax.shift_right_logical(bits(x[:, :half]), 16)


def _unpack_bf16_pairs(w):
    lo = lax.bitcast_convert_type(lax.shift_left(w, 16), F32)
    hi = lax.bitcast_convert_type(w & jnp.int32(-65536), F32)
    return jnp.concatenate([lo.astype(BF16), hi.astype(BF16)], axis=1)


def _top2_sum(a, b, c, d):
    m1, n1 = jnp.maximum(a, b), jnp.minimum(a, b)
    m2, n2 = jnp.maximum(c, d), jnp.minimum(c, d)
    return jnp.maximum(m1, m2) + jnp.maximum(jnp.minimum(m1, m2), jnp.maximum(n1, n2))


def _first_argmax(vals):
    best, idx = vals[0], jnp.zeros(vals[0].shape, jnp.int32)
    for j in range(1, len(vals)):
        upd = vals[j] > best
        idx = jnp.where(upd, j, idx)
        best = jnp.where(upd, vals[j], best)
    return idx, best


def _ffn_prep_body(x_ref, g_ref, s_ref, rwt_ref, rb_ref, hn_ref, e_ref, gw_ref, rank_ref, cnt_ref, carry_ref):
    i = pl.program_id(0)
    n_e = rwt_ref.shape[0]
    epg = n_e // N_GROUPS_MOE
    tm = x_ref.shape[0]

    @pl.when(i == 0)
    def _():
        carry_ref[...] = jnp.zeros_like(carry_ref)

    hn = _rms_mod(x_ref[...], g_ref[...], s_ref[...])
    hn_ref[...] = _pack_bf16_pairs(hn)
    logits = lax.dot_general(rwt_ref[...], hn, (((1,), (1,)), ((), ())),
                             preferred_element_type=F32, precision=lax.Precision.HIGHEST)
    score = jax.nn.sigmoid(logits)
    sel = score + rb_ref[...]

    row = lambda a, r: a[r:r + 1, :]
    gscore = [_top2_sum(*[row(sel, g * epg + j) for j in range(epg)]) for g in range(N_GROUPS_MOE)]
    bg, _ = _first_argmax(gscore)
    pick = lambda a, j: functools.reduce(
        lambda acc, g: jnp.where(bg == g, row(a, g * epg + j), acc), range(1, N_GROUPS_MOE), row(a, j))
    sel_in = [pick(sel, j) for j in range(epg)]
    sc_in = [pick(score, j) for j in range(epg)]
    i1, _ = _first_argmax(sel_in)
    neg = jnp.full_like(sel_in[0], -jnp.inf)
    i2, _ = _first_argmax([jnp.where(i1 == j, neg, sel_in[j]) for j in range(epg)])
    take = lambda vals, idx: functools.reduce(lambda acc, j: jnp.where(idx == j, vals[j], acc), range(1, epg), vals[0])
    w1, w2 = take(sc_in, i1), take(sc_in, i2)
    den = w1 + w2
    e1, e2 = bg * epg + i1, bg * epg + i2
    e_ref[...] = jnp.concatenate([e1, e2], axis=0)
    gw_ref[...] = jnp.concatenate([w1 / den, w2 / den], axis=0)

    eid = lax.broadcasted_iota(jnp.int32, (n_e, tm), 0)
    oh1 = (eid == e1).astype(F32)
    oh2 = (eid == e2).astype(F32)
    oh = oh1 + oh2
    tri = (lax.broadcasted_iota(jnp.int32, (tm, tm), 0) <= lax.broadcasted_iota(jnp.int32, (tm, tm), 1)).astype(BF16)
    incl = jnp.dot(oh.astype(BF16), tri, preferred_element_type=F32)
    before = carry_ref[:, 0:1] + incl - oh
    r1 = jnp.sum(oh1 * before, axis=0, keepdims=True)
    r2 = jnp.sum(oh2 * before, axis=0, keepdims=True)
    rank_ref[...] = jnp.concatenate([r1, r2], axis=0).astype(jnp.int32)
    carry_ref[...] = carry_ref[...] + incl[:, tm - 1:tm]
    cnt_ref[...] = carry_ref[...].astype(jnp.int32)


def _ffn_prep(x, geff, shift, rwt, rb, *, seq):
    n, d = x.shape
    n_e = rwt.shape[0]
    tm = min(256, seq)
    per = seq // tm
    vec = pl.BlockSpec((None, 1, d), lambda i: (i // per, 0, 0))
    row2 = pl.BlockSpec((TOP_K, tm), lambda i: (0, i))
    hn, e, gw, rank, cnt = pl.pallas_call(
        _ffn_prep_body,
        grid=(n // tm,),
        in_specs=[pl.BlockSpec((tm, d), lambda i: (i, 0)), vec, vec,
                  pl.BlockSpec((n_e, d), lambda i: (0, 0)), pl.BlockSpec((n_e, 1), lambda i: (0, 0))],
        out_specs=[pl.BlockSpec((tm, d // 2), lambda i: (i, 0)), row2, row2, row2,
                   pl.BlockSpec((n_e, LANES), lambda i: (0, 0))],
        out_shape=[jax.ShapeDtypeStruct((n, d // 2), jnp.int32), jax.ShapeDtypeStruct((TOP_K, n), jnp.int32),
                   jax.ShapeDtypeStruct((TOP_K, n), F32), jax.ShapeDtypeStruct((TOP_K, n), jnp.int32),
                   jax.ShapeDtypeStruct((n_e, LANES), jnp.int32)],
        scratch_shapes=[pltpu.VMEM((n_e, LANES), F32)],
        compiler_params=_params(("arbitrary",)),
        name="ffn_prep",
    )(x, geff, shift, rwt, rb)
    return hn, e, gw, rank, cnt[:, 0]


def _for_rows(lo, hi, fn):
    def body(r, c):
        fn(r)
        return c
    lax.fori_loop(lo, hi, body, 0)


def _dispatch_body(pos_ref, fill_ref, hn_ref, xs_hbm, zero_ref, sem, zsem, *, n_tokens):
    i = pl.program_id(0)
    tm = hn_ref.shape[0]

    def row_copy(r, k):
        return pltpu.make_async_copy(hn_ref.at[pl.ds(r, 1)], xs_hbm.at[pl.ds(pos_ref[k * n_tokens + i * tm + r], 1)], sem)

    def start(r):
        for k in range(TOP_K):
            row_copy(r, k).start()

    def wait(r):
        for k in range(TOP_K):
            row_copy(r, k).wait()

    _for_rows(0, tm, start)

    @pl.when(i == 0)
    def _():
        zero_ref[...] = jnp.zeros_like(zero_ref)
        n_fill = fill_ref.shape[0] // 2

        def zero_copy(row):
            return pltpu.make_async_copy(zero_ref, xs_hbm.at[pl.ds(row, 1)], zsem)

        for e in range(n_fill):
            _for_rows(fill_ref[2 * e], fill_ref[2 * e + 1], lambda row: zero_copy(row).start())
        for e in range(n_fill):
            _for_rows(fill_ref[2 * e], fill_ref[2 * e + 1], lambda row: zero_copy(row).wait())

    _for_rows(0, tm, wait)


def _moe_dispatch(hn, pos_flat, fill, *, rows):
    n, dh = hn.shape
    tm = min(512, n)
    grid_spec = pltpu.PrefetchScalarGridSpec(
        num_scalar_prefetch=2,
        grid=(n // tm,),
        in_specs=[pl.BlockSpec((tm, dh), lambda i, p, f: (i, 0))],
        out_specs=pl.BlockSpec(memory_space=pl.ANY),
        scratch_shapes=[pltpu.VMEM((1, dh), jnp.int32), pltpu.SemaphoreType.DMA(()), pltpu.SemaphoreType.DMA(())],
    )
    return pl.pallas_call(
        functools.partial(_dispatch_body, n_tokens=n),
        grid_spec=grid_spec,
        out_shape=jax.ShapeDtypeStruct((rows, dh), jnp.int32),
        compiler_params=_params(("arbitrary",)),
        name="moe_dispatch",
    )(pos_flat, fill, hn)


def _moe_body(te_ref, nrows_ref, x_ref, w1_ref, w3_ref, w2_ref, o_ref):
    i = pl.program_id(0)
    used = nrows_ref[i] > 0

    @pl.when(used)
    def _():
        xb = _unpack_bf16_pairs(x_ref[...])
        h1 = jnp.dot(xb, w1_ref[...], preferred_element_type=F32)
        h3 = jnp.dot(xb, w3_ref[...], preferred_element_type=F32)
        hh = (h1 * jax.nn.sigmoid(h1) * h3).astype(BF16)
        o_ref[...] = jnp.dot(hh, w2_ref[...], preferred_element_type=F32)

    @pl.when(jnp.logical_not(used))
    def _():
        o_ref[...] = jnp.zeros_like(o_ref)


def _moe_experts(xs, w1, w3, w2, tile_expert, tile_rows, *, layer):
    rows, dh = xs.shape
    d = 2 * dh
    fdim = w1.shape[-1]
    tm = MOE_TILE
    resident = pl.Buffered(1)
    grid_spec = pltpu.PrefetchScalarGridSpec(
        num_scalar_prefetch=2,
        grid=(rows // tm,),
        in_specs=[
            pl.BlockSpec((tm, dh), lambda i, te, nr: (i, 0)),
            pl.BlockSpec((None, None, d, fdim), lambda i, te, nr: (layer, te[i], 0, 0), pipeline_mode=resident),
            pl.BlockSpec((None, None, d, fdim), lambda i, te, nr: (layer, te[i], 0, 0), pipeline_mode=resident),
            pl.BlockSpec((None, None, fdim, d), lambda i, te, nr: (layer, te[i], 0, 0), pipeline_mode=resident),
        ],
        out_specs=pl.BlockSpec((tm, d), lambda i, te, nr: (i, 0)),
    )
    return pl.pallas_call(
        _moe_body,
        grid_spec=grid_spec,
        out_shape=jax.ShapeDtypeStruct((rows, d), F32),
        compiler_params=_params(("arbitrary",)),
        name="moe_experts",
    )(tile_expert, tile_rows, xs, w1, w3, w2)


def _moe_plan(e, rank, cnt, n_tiles_max):
    tm = MOE_TILE
    n_e = cnt.shape[0]
    tiles_per = (cnt + tm - 1) // tm
    ends = jnp.cumsum(tiles_per)
    off = (ends - tiles_per) * tm
    onehot = e[..., None] == jnp.arange(n_e, dtype=jnp.int32)
    pos = jnp.sum(jnp.where(onehot, off, 0), axis=-1) + rank
    tile = jnp.arange(n_tiles_max, dtype=jnp.int32)
    tile_expert = jnp.minimum(jnp.sum(tile[:, None] >= ends[None, :], axis=1), n_e - 1).astype(jnp.int32)
    mine = tile_expert[:, None] == jnp.arange(n_e, dtype=jnp.int32)
    tile_rows = jnp.clip(jnp.sum(jnp.where(mine, off + cnt, 0), axis=1) - tile * tm, 0, tm)
    tile_rows = jnp.where(tile < ends[-1], tile_rows, 0).astype(jnp.int32)
    fill_end = jnp.where(jnp.arange(n_e) == n_e - 1, n_tiles_max * tm, ends * tm)
    fill = jnp.stack([off + cnt, fill_end], axis=1).reshape(-1).astype(jnp.int32)
    return pos.astype(jnp.int32), tile_expert, tile_rows, fill


def _combine_gather(pos_ref, x_ref, y_hbm, gw_ref, gate_ref, ybuf, sem, *, n_tokens):
    i = pl.program_id(0)
    nsteps = pl.num_programs(0)
    tm = x_ref.shape[0]
    slot = i % 2

    def row_copy(step, sl, r, k):
        row = pos_ref[k * n_tokens + step * tm + r]
        return pltpu.make_async_copy(y_hbm.at[pl.ds(row, 1)], ybuf.at[sl, k, pl.ds(r, 1)], sem.at[sl])

    def start(step, sl):
        def one(r):
            for k in range(TOP_K):
                row_copy(step, sl, r, k).start()
        _for_rows(0, tm, one)

    @pl.when(i == 0)
    def _():
        start(0, 0)

    @pl.when(i + 1 < nsteps)
    def _():
        start(i + 1, 1 - slot)

    def wait_one(r):
        for k in range(TOP_K):
            row_copy(i, slot, r, k).wait()
    _for_rows(0, tm, wait_one)

    gw = gw_ref[...]
    return x_ref[...] + gate_ref[...] * (gw[:, 0:1] * ybuf[slot, 0] + gw[:, 1:2] * ybuf[slot, 1])


def _combine_norm_body(pos_ref, x_ref, y_hbm, gw_ref, gate_ref, g_ref, s_ref, xo_ref, hn_ref, ybuf, sem, *, n_tokens):
    xn = _combine_gather(pos_ref, x_ref, y_hbm, gw_ref, gate_ref, ybuf, sem, n_tokens=n_tokens)
    xo_ref[...] = xn
    hn_ref[...] = _rms_mod(xn, g_ref[...], s_ref[...]).astype(hn_ref.dtype)


def _combine_final_body(pos_ref, x_ref, y_hbm, gw_ref, gate_ref, g_ref, o_ref, ybuf, sem, *, n_tokens):
    xn = _combine_gather(pos_ref, x_ref, y_hbm, gw_ref, gate_ref, ybuf, sem, n_tokens=n_tokens)
    ms = jnp.mean(xn * xn, axis=-1, keepdims=True)
    o_ref[...] = xn * lax.rsqrt(ms + EPS) * g_ref[...]


def _ffn_combine(x, ys, pos_flat, gw_t, gate, norm_args, *, seq, final):
    n, d = x.shape
    tm = min(256, seq)
    per = seq // tm
    tile = pl.BlockSpec((tm, d), lambda i, p: (i, 0))
    vec = pl.BlockSpec((None, 1, d), lambda i, p: (i // per, 0, 0))
    in_specs = [tile, pl.BlockSpec(memory_space=pl.ANY), pl.BlockSpec((tm, TOP_K), lambda i, p: (i, 0)), vec]
    scratch = [pltpu.VMEM((2, TOP_K, tm, d), F32), pltpu.SemaphoreType.DMA((2,))]
    if final:
        body, name = _combine_final_body, "ffn_combine_final"
        in_specs.append(pl.BlockSpec((1, d), lambda i, p: (0, 0)))
        out_specs, out_shape = tile, jax.ShapeDtypeStruct((n, d), F32)
    else:
        body, name = _combine_norm_body, "ffn_combine_norm"
        in_specs += [vec, vec]
        out_specs = [tile, tile]
        out_shape = [jax.ShapeDtypeStruct((n, d), F32), jax.ShapeDtypeStruct((n, d), BF16)]
    grid_spec = pltpu.PrefetchScalarGridSpec(num_scalar_prefetch=1, grid=(n // tm,), in_specs=in_specs,
                                             out_specs=out_specs, scratch_shapes=scratch)
    return pl.pallas_call(
        functools.partial(body, n_tokens=n), grid_spec=grid_spec, out_shape=out_shape,
        compiler_params=_params(("arbitrary",)), name=name,
    )(pos_flat, x, ys, gw_t, gate, *norm_args)


def _moe_layer(x, geff, shift, rwt, rb, w1, w3, w2, *, seq, layer):
    n, _ = x.shape
    n_e = rwt.shape[0]
    hn, e, gw, rank, cnt = _ffn_prep(x, geff, shift, rwt, rb, seq=seq)
    n_tiles_max = (TOP_K * n) // MOE_TILE + n_e
    pos, tile_expert, tile_rows, fill = _moe_plan(e, rank, cnt, n_tiles_max)
    pos_flat = pos.reshape(-1)
    xs = _moe_dispatch(hn, pos_flat, fill, rows=n_tiles_max * MOE_TILE)
    ys = _moe_experts(xs, w1, w3, w2, tile_expert, tile_rows, layer=layer)
    return ys, pos_flat, gw.T


def kernel(x, c, norm_mix_g, norm_ffn_g, norm_final_g, ada_mix_w, ada_mix_b, ada_ffn_w, ada_ffn_b, s5_w_in, s5_lam_re, s5_lam_im, s5_log_step, s5_b_re, s5_b_im, s5_c_re, s5_c_im, s5_d, s5_w_glu, s5_b_glu, s5_w_out, gm_w_in, gm_b_in, gm_ln_g, gm_ln_b, gm_ws, gm_bs, gm_w_out, router_w, router_bias, moe_w1, moe_w3, moe_w2):
    bsz, seq, d = x.shape
    n = bsz * seq
    depth = norm_mix_g.shape[0]
    assert depth == 2 and seq % (CHUNK * SEGMENTS) == 0
    cb = seq // CHUNK
    nc = n // CHUNK

    mods_mix = _adaln(c, ada_mix_w, ada_mix_b)
    mods_ffn = _adaln(c, ada_ffn_w, ada_ffn_b)

    def split(m, g):
        shift, scale, gate = jnp.split(m, 3, axis=-1)
        return (g[None] * (1.0 + scale))[:, None], shift[:, None], gate[:, None]

    rwt = router_w.T
    rb = router_bias.reshape(-1, 1)
    xf = x.reshape(n, d)

    geff, shift, gate = split(mods_mix[0], norm_mix_g[0])
    x3 = xf.reshape(nc, CHUNK, d)
    hn8 = _norm_to_planes(x3, geff, shift, chunks_per_batch=cb)
    u8 = _matmul(hn8, s5_w_in[0].astype(BF16), tm=1024, tn=1024, out_dtype=BF16, epilogue=_ep_identity, name="s5_in")
    wa, wc, a1, aseg = _s5_weights(s5_lam_re[0], s5_lam_im[0], s5_log_step[0], s5_b_re[0], s5_b_im[0],
                                   s5_c_re[0], s5_c_im[0], seg_rows=cb // SEGMENTS)
    y8 = _s5_ssm(u8, wa, wc, a1, aseg, s5_d[0].reshape(1, -1), chunks_per_batch=cb,
                 group=s5_b_re.shape[-1], state=s5_b_re.shape[-2])
    w = y8.shape[-1]
    tmg, tng = min(1024, nc), min(1024, w)
    z8 = _matmul(y8, s5_w_glu[0].astype(BF16), tm=tmg, tn=tng, out_dtype=BF16, epilogue=_ep_glu,
                 extras=[(y8, pl.BlockSpec((None, tmg, tng), lambda q, i, j: (q, i, j))),
                         (s5_b_glu[0].reshape(1, -1), pl.BlockSpec((1, tng), lambda q, i, j: (0, j)))],
                 name="s5_glu")
    x1 = _s5_out(z8, s5_w_out[0].astype(BF16), x3, gate, chunks_per_batch=cb).reshape(n, d)

    geff, shift, gate = split(mods_ffn[0], norm_ffn_g[0])
    w1b, w3b, w2b = moe_w1.astype(BF16), moe_w3.astype(BF16), moe_w2.astype(BF16)
    ys, pos, gw_t = _moe_layer(x1, geff, shift, rwt, rb, w1b, w3b, w2b, seq=seq, layer=0)
    geff1, shift1, gate1 = split(mods_mix[1], norm_mix_g[1])
    x2, hn = _ffn_combine(x1, ys, pos, gw_t, gate, (geff1, shift1), seq=seq, final=False)

    w2n = gm_w_in.shape[-1]
    tm_in, tn_in = min(1024, n), min(1024, w2n)
    zz = _matmul(hn[None], gm_w_in[0].astype(BF16), tm=tm_in, tn=tn_in, out_dtype=BF16, epilogue=_ep_bias_gelu,
                 extras=[(gm_b_in[0].reshape(1, -1), pl.BlockSpec((1, tn_in), lambda q, i, j: (0, j)))],
                 name="gmlp_in")[0]
    chunk = gm_ws.shape[-1]
    gated = _gm_gate(zz, gm_ln_g[0].reshape(1, -1), gm_ln_b[0].reshape(1, -1), gm_ws[0].astype(BF16),
                     gm_bs[0].T, chunk=chunk)
    tm_o, tn_o = min(1024, seq), min(512, d)
    per = seq // tm_o
    x3_ = _matmul(gated[None], gm_w_out[0].astype(BF16), tm=tm_o, tn=tn_o, out_dtype=F32, epilogue=_ep_residual,
                  extras=[(x2[None], pl.BlockSpec((None, tm_o, tn_o), lambda q, i, j: (q, i, j))),
                          (gate1, pl.BlockSpec((None, 1, tn_o), lambda q, i, j: (i // per, 0, j)))],
                  name="gmlp_out")[0]

    geff, shift, gate = split(mods_ffn[1], norm_ffn_g[1])
    ys, pos, gw_t = _moe_layer(x3_, geff, shift, rwt, rb, w1b, w3b, w2b, seq=seq, layer=1)
    out = _ffn_combine(x3_, ys, pos, gw_t, gate, (norm_final_g.reshape(1, -1),), seq=seq, final=True)
    return out.reshape(bsz, seq, d)
```

```python
import functools

import jax
import jax.numpy as jnp
from jax import lax
from jax.experimental import pallas as pl
from jax.experimental.pallas import tpu as pltpu

EPS = 1e-6
CHUNK = 8
LANES = 128
SEGMENTS = 8
N_GROUPS_MOE = 4
TOP_K = 2
MOE_TILE = 256
VMEM_LIMIT = 56 * 1024 * 1024

F32 = jnp.float32
BF16 = jnp.bfloat16


def _params(sem):
    return pltpu.CompilerParams(dimension_semantics=sem, vmem_limit_bytes=VMEM_LIMIT)


def _rms_mod(x, geff, shift):
    ms = jnp.mean(x * x, axis=-1, keepdims=True)
    return x * lax.rsqrt(ms + EPS) * geff + shift


def _ada_body(c_ref, w_ref, b_ref, o_ref):
    c = c_ref[...]
    s = c * jax.nn.sigmoid(c)
    o_ref[...] = jnp.dot(s, w_ref[...], preferred_element_type=F32, precision=lax.Precision.HIGHEST) + b_ref[...]


def _adaln(c, w, b):
    depth, d, d3 = w.shape
    bsz = c.shape[0]
    rows = 8
    cp = jnp.zeros((rows, d), F32).at[:bsz].set(c)
    tn = 1024 if d3 % 1024 == 0 else 512
    assert d3 % tn == 0
    out = pl.pallas_call(
        _ada_body,
        grid=(depth, d3 // tn),
        in_specs=[
            pl.BlockSpec((rows, d), lambda l, j: (0, 0)),
            pl.BlockSpec((None, d, tn), lambda l, j: (l, 0, j)),
            pl.BlockSpec((None, 1, tn), lambda l, j: (l, 0, j)),
        ],
        out_specs=pl.BlockSpec((None, rows, tn), lambda l, j: (l, 0, j)),
        out_shape=jax.ShapeDtypeStruct((depth, rows, d3), F32),
        compiler_params=_params(("arbitrary", "arbitrary")),
        name="adaln",
    )(cp, w, b.reshape(depth, 1, d3))
    return out[:, :bsz]


def _mm_body(*refs, epilogue, n_extra):
    a_ref, w_ref = refs[0], refs[1]
    extra = refs[2:2 + n_extra]
    o_ref = refs[2 + n_extra]
    acc = jnp.dot(a_ref[...], w_ref[...], preferred_element_type=F32)
    o_ref[...] = epilogue(acc, *extra).astype(o_ref.dtype)


def _matmul(a, w, *, tm, tn, out_dtype, epilogue, extras=(), name):
    p, m, k = a.shape
    n = w.shape[1]
    tm, tn = min(tm, m), min(tn, n)
    in_specs = [
        pl.BlockSpec((None, tm, k), lambda q, i, j: (q, i, 0)),
        pl.BlockSpec((k, tn), lambda q, i, j: (0, j)),
    ] + [s for _, s in extras]
    return pl.pallas_call(
        functools.partial(_mm_body, epilogue=epilogue, n_extra=len(extras)),
        grid=(p, m // tm, n // tn),
        in_specs=in_specs,
        out_specs=pl.BlockSpec((None, tm, tn), lambda q, i, j: (q, i, j)),
        out_shape=jax.ShapeDtypeStruct((p, m, n), out_dtype),
        compiler_params=_params(("arbitrary", "arbitrary", "arbitrary")),
        name=name,
    )(a, w, *[x for x, _ in extras])


def _ep_identity(acc):
    return acc


def _ep_glu(acc, y_ref, b_ref):
    return y_ref[...].astype(F32) * jax.nn.sigmoid(acc + b_ref[...])


def _ep_bias_gelu(acc, b_ref):
    return jax.nn.gelu(acc + b_ref[...])


def _ep_residual(acc, x_ref, g_ref):
    return x_ref[...] + g_ref[...] * acc


def _plane_rows(block, seg, *, cb, kb):
    per = cb // (kb * SEGMENTS)
    return pl.ds((block // per) * cb + seg * (cb // SEGMENTS) + (block % per) * kb, kb)


def _norm_plane_body(x_hbm, g_ref, s_ref, o_ref, buf, sem, *, cb):
    i = pl.program_id(0)
    nsteps = pl.num_programs(0)
    kb = buf.shape[2]
    slot = i % 2

    def fetch(step, sl, t, s):
        return pltpu.make_async_copy(x_hbm.at[_plane_rows(step, s, cb=cb, kb=kb), t], buf.at[sl, t, :, s], sem.at[sl, t])

    def start_all(step, sl):
        for t in range(CHUNK):
            for s in range(SEGMENTS):
                fetch(step, sl, t, s).start()

    @pl.when(i == 0)
    def _():
        start_all(0, 0)

    @pl.when(i + 1 < nsteps)
    def _():
        start_all(i + 1, 1 - slot)

    for t in range(CHUNK):
        for s in range(SEGMENTS):
            fetch(i, slot, t, s).wait()
        xt = buf[slot, t].reshape(kb * SEGMENTS, buf.shape[-1])
        o_ref[t] = _rms_mod(xt, g_ref[...], s_ref[...]).astype(o_ref.dtype)


def _norm_to_planes(x3, geff, shift, *, chunks_per_batch):
    nc, _, d = x3.shape
    kb = 4
    tc = kb * SEGMENTS
    per = chunks_per_batch // tc
    vec = pl.BlockSpec((None, 1, d), lambda i: (i // per, 0, 0))
    return pl.pallas_call(
        functools.partial(_norm_plane_body, cb=chunks_per_batch),
        grid=(nc // tc,),
        in_specs=[pl.BlockSpec(memory_space=pl.ANY), vec, vec],
        out_specs=pl.BlockSpec((CHUNK, tc, d), lambda i: (0, i, 0)),
        out_shape=jax.ShapeDtypeStruct((CHUNK, nc, d), BF16),
        scratch_shapes=[pltpu.VMEM((2, CHUNK, kb, SEGMENTS, d), F32), pltpu.SemaphoreType.DMA((2, CHUNK))],
        compiler_params=_params(("arbitrary",)),
        name="s5_norm_planes",
    )(x3, geff, shift)


def _cmul_add(ar, ai, xr, xi, vr, vi):
    return ar * xr - ai * xi + vr, ar * xi + ai * xr + vi


def _ssm_body(u_ref, wac_ref, wcc_ref, a1_ref, aseg_ref, d_ref, y_ref, xf_ref, xb_ref, wa_ref, wc_ref, *,
              rows_per_dot, gl, hsz, psz):
    cb = u_ref.shape[1]
    ks = cb // SEGMENTS
    half = xf_ref.shape[-1]
    q = half // 2

    @pl.when(pl.program_id(1) == 0)
    def _():
        iota = lambda shape, ax: lax.broadcasted_iota(jnp.int32, shape, ax)
        ca, na = wac_ref.shape[1], wa_ref.shape[1]
        dcol = iota((ca, na), 1)
        spread = (iota((ca, na), 0) == (dcol // (gl * psz)) * psz + dcol % psz).astype(BF16)
        rep = jnp.dot(wac_ref[...], spread, preferred_element_type=F32)
        shp = rep.shape
        keep = (iota(shp, 0) // hsz) % gl == (iota(shp, 1) // psz) % gl
        wa_ref[...] = jnp.where(keep, rep, 0.0).astype(wa_ref.dtype)
        cc, nc_ = wcc_ref.shape[1], wc_ref.shape[1]
        dcol = iota((cc, nc_), 1)
        spread = (iota((cc, nc_), 0) == (dcol // (gl * hsz)) * hsz + dcol % hsz).astype(BF16)
        rep = jnp.dot(wcc_ref[...], spread, preferred_element_type=F32)
        shp = rep.shape
        n_tok_rows = CHUNK * gl * hsz
        r = iota(shp, 0)
        row_g = jnp.where(r < n_tok_rows, (r // hsz) % gl, ((r - n_tok_rows) // psz) % gl)
        keep = row_g == (iota(shp, 1) // hsz) % gl
        wc_ref[...] = jnp.where(keep, rep, 0.0).astype(wc_ref.dtype)

    def u_rows(r0, nr):
        return jnp.concatenate([u_ref[t, pl.ds(r0, nr), :] for t in range(CHUNK)], axis=1)

    nr = min(cb, rows_per_dot)
    for r0 in range(0, cb, nr):
        res = jnp.dot(u_rows(r0, nr), wa_ref[...], preferred_element_type=F32)
        xf_ref[pl.ds(r0, nr), :] = res[:, :half]
        xb_ref[pl.ds(r0, nr), :] = res[:, half:]

    afr, afi, abr, abi = (a1_ref[i:i + 1, :] for i in range(4))
    group = lambda j: pl.ds(pl.multiple_of(j * SEGMENTS, SEGMENTS), SEGMENTS)

    def p1(j, carry):
        cfr, cfi, cbr, cbi = carry
        jb = ks - 1 - j
        vf = xf_ref[group(j), :]
        vb = xb_ref[group(jb), :]
        xf_ref[group(j), :] = jnp.concatenate([cfr, cfi], axis=1)
        xb_ref[group(jb), :] = jnp.concatenate([cbr, cbi], axis=1)
        nfr, nfi = _cmul_add(afr, afi, cfr, cfi, vf[:, :q], vf[:, q:])
        nbr, nbi = _cmul_add(abr, abi, cbr, cbi, vb[:, :q], vb[:, q:])
        return nfr, nfi, nbr, nbi

    z = jnp.zeros((SEGMENTS, q), F32)
    efr, efi, ebr, ebi = lax.fori_loop(0, ks, p1, (z, z, z, z))

    sfr, sfi, sbr, sbi = (aseg_ref[i:i + 1, :] for i in range(4))
    z1 = jnp.zeros((1, q), F32)
    gfr, gfi = [z1], [z1]
    for s in range(SEGMENTS - 1):
        r, i = _cmul_add(sfr, sfi, gfr[-1], gfi[-1], efr[s:s + 1], efi[s:s + 1])
        gfr.append(r)
        gfi.append(i)
    gbr, gbi = [z1], [z1]
    for s in range(SEGMENTS - 1, 0, -1):
        r, i = _cmul_add(sbr, sbi, gbr[0], gbi[0], ebr[s:s + 1], ebi[s:s + 1])
        gbr.insert(0, r)
        gbi.insert(0, i)
    gfr, gfi, gbr, gbi = (jnp.concatenate(g, axis=0) for g in (gfr, gfi, gbr, gbi))

    def p2(j, carry):
        pfr, pfi, pbr, pbi = carry
        jb = ks - 1 - j
        cf = xf_ref[group(j), :]
        cb_ = xb_ref[group(jb), :]
        xf_ref[group(j), :] = jnp.concatenate([cf[:, :q] + pfr * gfr - pfi * gfi, cf[:, q:] + pfr * gfi + pfi * gfr], axis=1)
        xb_ref[group(jb), :] = jnp.concatenate([cb_[:, :q] + pbr * gbr - pbi * gbi, cb_[:, q:] + pbr * gbi + pbi * gbr], axis=1)
        return (pfr * afr - pfi * afi, pfr * afi + pfi * afr, pbr * abr - pbi * abi, pbr * abi + pbi * abr)

    one = jnp.ones((1, q), F32)
    lax.fori_loop(0, ks, p2, (one, jnp.zeros((1, q), F32), one, jnp.zeros((1, q), F32)))

    lw = d_ref.shape[-1]
    for r0 in range(0, cb, nr):
        xf = xf_ref[pl.ds(r0, nr), :]
        xb = xb_ref[pl.ds(r0, nr), :]
        lhs = jnp.concatenate([u_rows(r0, nr), xf.astype(BF16), xb.astype(BF16)], axis=1)
        acc = jnp.dot(lhs, wc_ref[...], preferred_element_type=F32)
        for t in range(CHUNK):
            yt = acc[:, t * lw:(t + 1) * lw] + d_ref[...] * u_ref[t, pl.ds(r0, nr), :].astype(F32)
            y_ref[t, pl.ds(r0, nr), :] = jax.nn.gelu(yt).astype(y_ref.dtype)


def _s5_ssm(u8, wa, wc, a1, aseg, dvec, *, chunks_per_batch, group, state):
    _, nc, w = u8.shape
    nslab = w // LANES
    nb = nc // chunks_per_batch
    cb = chunks_per_batch
    ks = cb // SEGMENTS
    gl = LANES // group
    half = 2 * gl * state
    tok = CHUNK * LANES
    body = functools.partial(_ssm_body, rows_per_dot=512, gl=gl, hsz=group, psz=state)
    return pl.pallas_call(
        body,
        grid=(nslab, nb),
        in_specs=[
            pl.BlockSpec((CHUNK, cb, LANES), lambda s, b: (0, b, s)),
            pl.BlockSpec((None,) + wa.shape[1:], lambda s, b: (s, 0, 0)),
            pl.BlockSpec((None,) + wc.shape[1:], lambda s, b: (s, 0, 0)),
            pl.BlockSpec((None,) + a1.shape[1:], lambda s, b: (s, 0, 0)),
            pl.BlockSpec((None,) + aseg.shape[1:], lambda s, b: (s, 0, 0)),
            pl.BlockSpec((1, LANES), lambda s, b: (0, s)),
        ],
        out_specs=pl.BlockSpec((CHUNK, cb, LANES), lambda s, b: (0, b, s)),
        out_shape=jax.ShapeDtypeStruct(u8.shape, BF16),
        scratch_shapes=[
            pltpu.VMEM((cb, half), F32),
            pltpu.VMEM((cb, half), F32),
            pltpu.VMEM((tok, 2 * half), BF16),
            pltpu.VMEM((tok + 2 * half, tok), BF16),
        ],
        compiler_params=_params(("arbitrary", "arbitrary")),
        name="s5_ssm",
    )(u8, wa, wc, a1, aseg, dvec)


def _s5_weights(lam_re, lam_im, log_step, b_re, b_im, c_re, c_im, *, seg_rows):
    _, g, p = lam_re.shape
    h = b_re.shape[-1]
    gl = LANES // h
    s = g // gl
    t = CHUNK
    lam = lax.complex(lam_re.astype(F32), lam_im.astype(F32))
    zed = lam * jnp.exp(log_step.astype(F32))[..., None]
    lam_bar = jnp.exp(zed)
    bbar = ((lam_bar - 1.0) / lam)[..., None] * lax.complex(b_re.astype(F32), b_im.astype(F32))
    cmat = lax.complex(c_re.astype(F32), c_im.astype(F32))
    pw = jnp.exp(zed[:, None] * jnp.arange(t + 1, dtype=F32)[None, :, None, None])
    pw_down = jnp.exp(zed[:, None] * (t - jnp.arange(t + 1, dtype=F32))[None, :, None, None])

    ma_f = pw_down[0, 1:, :, :, None] * bbar[0][None]
    ma_b = pw[1, :t, :, :, None] * bbar[1][None]
    wa = jnp.stack([ma_f.real, ma_f.imag, ma_b.real, ma_b.imag], axis=0)
    wa = wa.reshape(4, t, s, gl, p, h).transpose(2, 1, 3, 5, 0, 4).reshape(s, t * gl * h, 4 * p)

    kf = jnp.einsum('gyp,dgp,gph->dgyh', cmat[0], pw[0, :t], bbar[0]).real
    kb = jnp.einsum('gyp,dgp,gph->dgyh', cmat[1], pw[1, :t], bbar[1]).real
    st = jnp.arange(t)
    lag = st[None, None, :] - st[None, :, None]
    sel_f = (lag == st[:, None, None]).astype(F32)
    sel_b = (-lag == st[:, None, None]).astype(F32)
    coef = jnp.einsum('dat,dgyh->atgyh', sel_f, kf) + jnp.einsum('dat,dgyh->atgyh', sel_b, kb)
    w_u = coef.reshape(t, t, s, gl, h, h).transpose(2, 0, 3, 5, 1, 4).reshape(s, t * gl * h, t * h)

    cl_f = cmat[0][None] * pw[0, 1:][:, :, None, :]
    cl_b = cmat[1][None] * pw_down[1, :t][:, :, None, :]
    def state_rows(x):
        return x.reshape(t, s, gl, h, p).transpose(1, 2, 4, 0, 3).reshape(s, gl * p, t * h)
    wc = jnp.concatenate([w_u, state_rows(cl_f.real), state_rows(-cl_f.imag),
                          state_rows(cl_b.real), state_rows(-cl_b.imag)], axis=1)

    def lanes(x):
        x = x.reshape(2, s, gl * p)
        return jnp.stack([x[0].real, x[0].imag, x[1].real, x[1].imag], axis=1)
    a1 = lanes(pw[:, t])
    aseg = lanes(jnp.exp(zed * float(t * seg_rows)))
    return wa.astype(BF16), wc.astype(BF16), a1, aseg


def _s5_out_body(a_ref, w_ref, g_ref, x_hbm, o_hbm, xin, xout, isem, osem, *, cb):
    nj = pl.num_programs(1)
    step = pl.program_id(0) * nj + pl.program_id(1)
    total = pl.num_programs(0) * nj
    tm, tn = a_ref.shape[1], w_ref.shape[1]
    kb = tm // SEGMENTS
    slot = step % 2

    def window(st, t, s):
        cols = pl.ds(pl.multiple_of((st % nj) * tn, tn), tn)
        return (_plane_rows(st // nj, s, cb=cb, kb=kb), t, cols)

    def fetch(st, sl, t, s):
        return pltpu.make_async_copy(x_hbm.at[window(st, t, s)], xin.at[sl, t, :, s], isem.at[sl, t])

    def put(st, sl, t, s):
        return pltpu.make_async_copy(xout.at[sl, t, :, s], o_hbm.at[window(st, t, s)], osem.at[sl, t])

    def each(fn):
        for t in range(CHUNK):
            for s in range(SEGMENTS):
                fn(t, s)

    @pl.when(step == 0)
    def _():
        each(lambda t, s: fetch(0, 0, t, s).start())

    @pl.when(step + 1 < total)
    def _():
        each(lambda t, s: fetch(step + 1, 1 - slot, t, s).start())

    a = a_ref[...].reshape(CHUNK * tm, a_ref.shape[2])
    acc = jnp.dot(a, w_ref[...], preferred_element_type=F32)

    @pl.when(step >= 2)
    def _():
        each(lambda t, s: put(step - 2, slot, t, s).wait())

    for t in range(CHUNK):
        for s in range(SEGMENTS):
            fetch(step, slot, t, s).wait()
        xt = xin[slot, t].reshape(tm, tn) + g_ref[...] * acc[t * tm:(t + 1) * tm]
        xout[slot, t] = xt.reshape(kb, SEGMENTS, tn)
    each(lambda t, s: put(step, slot, t, s).start())

    @pl.when(step == total - 1)
    def _():
        each(lambda t, s: put(step, slot, t, s).wait())

        @pl.when(total >= 2)
        def _():
            each(lambda t, s: put(step - 1, 1 - slot, t, s).wait())


def _s5_out(z8, w, x3, gate, *, chunks_per_batch):
    _, nc, k = z8.shape
    d = w.shape[1]
    tm = min(128, chunks_per_batch)
    tn = min(512, d)
    per = chunks_per_batch // tm
    kb = tm // SEGMENTS
    return pl.pallas_call(
        functools.partial(_s5_out_body, cb=chunks_per_batch),
        grid=(nc // tm, d // tn),
        in_specs=[
            pl.BlockSpec((CHUNK, tm, k), lambda i, j: (0, i, 0)),
            pl.BlockSpec((k, tn), lambda i, j: (0, j)),
            pl.BlockSpec((None, 1, tn), lambda i, j: (i // per, 0, j)),
            pl.BlockSpec(memory_space=pl.ANY),
        ],
        out_specs=pl.BlockSpec(memory_space=pl.ANY),
        out_shape=jax.ShapeDtypeStruct(x3.shape, F32),
        scratch_shapes=[
            pltpu.VMEM((2, CHUNK, kb, SEGMENTS, tn), F32),
            pltpu.VMEM((2, CHUNK, kb, SEGMENTS, tn), F32),
            pltpu.SemaphoreType.DMA((2, CHUNK)),
            pltpu.SemaphoreType.DMA((2, CHUNK)),
        ],
        compiler_params=_params(("arbitrary", "arbitrary")),
        name="s5_out",
    )(z8, w, gate, x3)


def _gm_gate_body(z_ref, lng_ref, lnb_ref, ws_ref, bs_ref, o_ref, *, chunk):
    tm, w2 = z_ref.shape
    w = w2 // 2
    heads = ws_ref.shape[0]
    hd = w // heads
    v = z_ref[:, w:].astype(F32)
    mu = jnp.mean(v, axis=-1, keepdims=True)
    vc = v - mu
    var = jnp.mean(vc * vc, axis=-1, keepdims=True)
    vn = (vc * lax.rsqrt(var + EPS) * lng_ref[...] + lnb_ref[...]).astype(BF16)
    for c in range(tm // chunk):
        rows = slice(c * chunk, (c + 1) * chunk)
        for hh in range(heads):
            cols = slice(hh * hd, (hh + 1) * hd)
            sv = jnp.dot(ws_ref[hh], vn[rows, cols], preferred_element_type=F32) + bs_ref[:, hh:hh + 1]
            o_ref[rows, cols] = (z_ref[rows, cols].astype(F32) * sv).astype(o_ref.dtype)


def _gm_gate(z, ln_g, ln_b, ws, bs, *, chunk):
    n, w2 = z.shape
    w = w2 // 2
    tm = 2 * chunk
    full = lambda a: pl.BlockSpec(a.shape, lambda i: (0,) * a.ndim)
    return pl.pallas_call(
        functools.partial(_gm_gate_body, chunk=chunk),
        grid=(n // tm,),
        in_specs=[pl.BlockSpec((tm, w2), lambda i: (i, 0)), full(ln_g), full(ln_b), full(ws), full(bs)],
        out_specs=pl.BlockSpec((tm, w), lambda i: (i, 0)),
        out_shape=jax.ShapeDtypeStruct((n, w), BF16),
        compiler_params=_params(("arbitrary",)),
        name="gmlp_gate",
    )(z, ln_g, ln_b, ws, bs)


def _pack_bf16_pairs(x):
    half = x.shape[1] // 2
    bits = lambda v: lax.bitcast_convert_type(v.astype(BF16).astype(F32), jnp.int32)
    return bits(x[:, half:]) | lax.shift_right_logical(bits(x[:, :half]), 16)


def _unpack_bf16_pairs(w):
    lo = lax.bitcast_convert_type(lax.shift_left(w, 16), F32)
    hi = lax.bitcast_convert_type(w & jnp.int32(-65536), F32)
    return jnp.concatenate([lo.astype(BF16), hi.astype(BF16)], axis=1)


def _top2_sum(a, b, c, d):
    m1, n1 = jnp.maximum(a, b), jnp.minimum(a, b)
    m2, n2 = jnp.maximum(c, d), jnp.minimum(c, d)
    return jnp.maximum(m1, m2) + jnp.maximum(jnp.minimum(m1, m2), jnp.maximum(n1, n2))


def _first_argmax(vals):
    best, idx = vals[0], jnp.zeros(vals[0].shape, jnp.int32)
    for j in range(1, len(vals)):
        upd = vals[j] > best
        idx = jnp.where(upd, j, idx)
        best = jnp.where(upd, vals[j], best)
    return idx, best


def _ffn_prep_body(x_ref, g_ref, s_ref, rwt_ref, rb_ref, hn_ref, e_ref, gw_ref, rank_ref, cnt_ref, carry_ref):
    i = pl.program_id(0)
    n_e = rwt_ref.shape[0]
    epg = n_e // N_GROUPS_MOE
    tm = x_ref.shape[0]

    @pl.when(i == 0)
    def _():
        carry_ref[...] = jnp.zeros_like(carry_ref)

    hn = _rms_mod(x_ref[...], g_ref[...], s_ref[...])
    hn_ref[...] = _pack_bf16_pairs(hn)
    logits = lax.dot_general(rwt_ref[...], hn, (((1,), (1,)), ((), ())),
                             preferred_element_type=F32, precision=lax.Precision.HIGHEST)
    score = jax.nn.sigmoid(logits)
    sel = score + rb_ref[...]

    row = lambda a, r: a[r:r + 1, :]
    gscore = [_top2_sum(*[row(sel, g * epg + j) for j in range(epg)]) for g in range(N_GROUPS_MOE)]
    bg, _ = _first_argmax(gscore)
    pick = lambda a, j: functools.reduce(
        lambda acc, g: jnp.where(bg == g, row(a, g * epg + j), acc), range(1, N_GROUPS_MOE), row(a, j))
    sel_in = [pick(sel, j) for j in range(epg)]
    sc_in = [pick(score, j) for j in range(epg)]
    i1, _ = _first_argmax(sel_in)
    neg = jnp.full_like(sel_in[0], -jnp.inf)
    i2, _ = _first_argmax([jnp.where(i1 == j, neg, sel_in[j]) for j in range(epg)])
    take = lambda vals, idx: functools.reduce(lambda acc, j: jnp.where(idx == j, vals[j], acc), range(1, epg), vals[0])
    w1, w2 = take(sc_in, i1), take(sc_in, i2)
    den = w1 + w2
    e1, e2 = bg * epg + i1, bg * epg + i2
    e_ref[...] = jnp.concatenate([e1, e2], axis=0)
    gw_ref[...] = jnp.concatenate([w1 / den, w2 / den], axis=0)

    eid = lax.broadcasted_iota(jnp.int32, (n_e, tm), 0)
    oh1 = (eid == e1).astype(F32)
    oh2 = (eid == e2).astype(F32)
    oh = oh1 + oh2
    tri = (lax.broadcasted_iota(jnp.int32, (tm, tm), 0) <= lax.broadcasted_iota(jnp.int32, (tm, tm), 1)).astype(BF16)
    incl = jnp.dot(oh.astype(BF16), tri, preferred_element_type=F32)
    before = carry_ref[:, 0:1] + incl - oh
    r1 = jnp.sum(oh1 * before, axis=0, keepdims=True)
    r2 = jnp.sum(oh2 * before, axis=0, keepdims=True)
    rank_ref[...] = jnp.concatenate([r1, r2], axis=0).astype(jnp.int32)
    carry_ref[...] = carry_ref[...] + incl[:, tm - 1:tm]
    cnt_ref[...] = carry_ref[...].astype(jnp.int32)


def _ffn_prep(x, geff, shift, rwt, rb, *, seq):
    n, d = x.shape
    n_e = rwt.shape[0]
    tm = min(256, seq)
    per = seq // tm
    vec = pl.BlockSpec((None, 1, d), lambda i: (i // per, 0, 0))
    row2 = pl.BlockSpec((TOP_K, tm), lambda i: (0, i))
    hn, e, gw, rank, cnt = pl.pallas_call(
        _ffn_prep_body,
        grid=(n // tm,),
        in_specs=[pl.BlockSpec((tm, d), lambda i: (i, 0)), vec, vec,
                  pl.BlockSpec((n_e, d), lambda i: (0, 0)), pl.BlockSpec((n_e, 1), lambda i: (0, 0))],
        out_specs=[pl.BlockSpec((tm, d // 2), lambda i: (i, 0)), row2, row2, row2,
                   pl.BlockSpec((n_e, LANES), lambda i: (0, 0))],
        out_shape=[jax.ShapeDtypeStruct((n, d // 2), jnp.int32), jax.ShapeDtypeStruct((TOP_K, n), jnp.int32),
                   jax.ShapeDtypeStruct((TOP_K, n), F32), jax.ShapeDtypeStruct((TOP_K, n), jnp.int32),
                   jax.ShapeDtypeStruct((n_e, LANES), jnp.int32)],
        scratch_shapes=[pltpu.VMEM((n_e, LANES), F32)],
        compiler_params=_params(("arbitrary",)),
        name="ffn_prep",
    )(x, geff, shift, rwt, rb)
    return hn, e, gw, rank, cnt[:, 0]


def _for_rows(lo, hi, fn):
    def body(r, c):
        fn(r)
        return c
    lax.fori_loop(lo, hi, body, 0)


def _dispatch_body(pos_ref, fill_ref, hn_ref, xs_hbm, zero_ref, sem, zsem, *, n_tokens):
    i = pl.program_id(0)
    tm = hn_ref.shape[0]

    def row_copy(r, k):
        return pltpu.make_async_copy(hn_ref.at[pl.ds(r, 1)], xs_hbm.at[pl.ds(pos_ref[k * n_tokens + i * tm + r], 1)], sem)

    def start(r):
        for k in range(TOP_K):
            row_copy(r, k).start()

    def wait(r):
        for k in range(TOP_K):
            row_copy(r, k).wait()

    _for_rows(0, tm, start)

    @pl.when(i == 0)
    def _():
        zero_ref[...] = jnp.zeros_like(zero_ref)
        n_fill = fill_ref.shape[0] // 2

        def zero_copy(row):
            return pltpu.make_async_copy(zero_ref, xs_hbm.at[pl.ds(row, 1)], zsem)

        for e in range(n_fill):
            _for_rows(fill_ref[2 * e], fill_ref[2 * e + 1], lambda row: zero_copy(row).start())
        for e in range(n_fill):
            _for_rows(fill_ref[2 * e], fill_ref[2 * e + 1], lambda row: zero_copy(row).wait())

    _for_rows(0, tm, wait)


def _moe_dispatch(hn, pos_flat, fill, *, rows):
    n, dh = hn.shape
    tm = min(512, n)
    grid_spec = pltpu.PrefetchScalarGridSpec(
        num_scalar_prefetch=2,
        grid=(n // tm,),
        in_specs=[pl.BlockSpec((tm, dh), lambda i, p, f: (i, 0))],
        out_specs=pl.BlockSpec(memory_space=pl.ANY),
        scratch_shapes=[pltpu.VMEM((1, dh), jnp.int32), pltpu.SemaphoreType.DMA(()), pltpu.SemaphoreType.DMA(())],
    )
    return pl.pallas_call(
        functools.partial(_dispatch_body, n_tokens=n),
        grid_spec=grid_spec,
        out_shape=jax.ShapeDtypeStruct((rows, dh), jnp.int32),
        compiler_params=_params(("arbitrary",)),
        name="moe_dispatch",
    )(pos_flat, fill, hn)


def _moe_body(te_ref, nrows_ref, x_ref, w1_ref, w3_ref, w2_ref, o_ref):
    i = pl.program_id(0)
    used = nrows_ref[i] > 0

    @pl.when(used)
    def _():
        xb = _unpack_bf16_pairs(x_ref[...])
        h1 = jnp.dot(xb, w1_ref[...], preferred_element_type=F32)
        h3 = jnp.dot(xb, w3_ref[...], preferred_element_type=F32)
        hh = (h1 * jax.nn.sigmoid(h1) * h3).astype(BF16)
        o_ref[...] = jnp.dot(hh, w2_ref[...], preferred_element_type=F32)

    @pl.when(jnp.logical_not(used))
    def _():
        o_ref[...] = jnp.zeros_like(o_ref)


def _moe_experts(xs, w1, w3, w2, tile_expert, tile_rows, *, layer):
    rows, dh = xs.shape
    d = 2 * dh
    fdim = w1.shape[-1]
    tm = MOE_TILE
    resident = pl.Buffered(1)
    grid_spec = pltpu.PrefetchScalarGridSpec(
        num_scalar_prefetch=2,
        grid=(rows // tm,),
        in_specs=[
            pl.BlockSpec((tm, dh), lambda i, te, nr: (i, 0)),
            pl.BlockSpec((None, None, d, fdim), lambda i, te, nr: (layer, te[i], 0, 0), pipeline_mode=resident),
            pl.BlockSpec((None, None, d, fdim), lambda i, te, nr: (layer, te[i], 0, 0), pipeline_mode=resident),
            pl.BlockSpec((None, None, fdim, d), lambda i, te, nr: (layer, te[i], 0, 0), pipeline_mode=resident),
        ],
        out_specs=pl.BlockSpec((tm, d), lambda i, te, nr: (i, 0)),
    )
    return pl.pallas_call(
        _moe_body,
        grid_spec=grid_spec,
        out_shape=jax.ShapeDtypeStruct((rows, d), F32),
        compiler_params=_params(("arbitrary",)),
        name="moe_experts",
    )(tile_expert, tile_rows, xs, w1, w3, w2)


def _moe_plan(e, rank, cnt, n_tiles_max):
    tm = MOE_TILE
    n_e = cnt.shape[0]
    tiles_per = (cnt + tm - 1) // tm
    ends = jnp.cumsum(tiles_per)
    off = (ends - tiles_per) * tm
    onehot = e[..., None] == jnp.arange(n_e, dtype=jnp.int32)
    pos = jnp.sum(jnp.where(onehot, off, 0), axis=-1) + rank
    tile = jnp.arange(n_tiles_max, dtype=jnp.int32)
    tile_expert = jnp.minimum(jnp.sum(tile[:, None] >= ends[None, :], axis=1), n_e - 1).astype(jnp.int32)
    mine = tile_expert[:, None] == jnp.arange(n_e, dtype=jnp.int32)
    tile_rows = jnp.clip(jnp.sum(jnp.where(mine, off + cnt, 0), axis=1) - tile * tm, 0, tm)
    tile_rows = jnp.where(tile < ends[-1], tile_rows, 0).astype(jnp.int32)
    fill_end = jnp.where(jnp.arange(n_e) == n_e - 1, n_tiles_max * tm, ends * tm)
    fill = jnp.stack([off + cnt, fill_end], axis=1).reshape(-1).astype(jnp.int32)
    return pos.astype(jnp.int32), tile_expert, tile_rows, fill


def _combine_gather(pos_ref, x_ref, y_hbm, gw_ref, gate_ref, ybuf, sem, *, n_tokens):
    i = pl.program_id(0)
    nsteps = pl.num_programs(0)
    tm = x_ref.shape[0]
    slot = i % 2

    def row_copy(step, sl, r, k):
        row = pos_ref[k * n_tokens + step * tm + r]
        return pltpu.make_async_copy(y_hbm.at[pl.ds(row, 1)], ybuf.at[sl, k, pl.ds(r, 1)], sem.at[sl])

    def start(step, sl):
        def one(r):
            for k in range(TOP_K):
                row_copy(step, sl, r, k).start()
        _for_rows(0, tm, one)

    @pl.when(i == 0)
    def _():
        start(0, 0)

    @pl.when(i + 1 < nsteps)
    def _():
        start(i + 1, 1 - slot)

    def wait_one(r):
        for k in range(TOP_K):
            row_copy(i, slot, r, k).wait()
    _for_rows(0, tm, wait_one)

    gw = gw_ref[...]
    return x_ref[...] + gate_ref[...] * (gw[:, 0:1] * ybuf[slot, 0] + gw[:, 1:2] * ybuf[slot, 1])


def _combine_norm_body(pos_ref, x_ref, y_hbm, gw_ref, gate_ref, g_ref, s_ref, xo_ref, hn_ref, ybuf, sem, *, n_tokens):
    xn = _combine_gather(pos_ref, x_ref, y_hbm, gw_ref, gate_ref, ybuf, sem, n_tokens=n_tokens)
    xo_ref[...] = xn
    hn_ref[...] = _rms_mod(xn, g_ref[...], s_ref[...]).astype(hn_ref.dtype)


def _combine_final_body(pos_ref, x_ref, y_hbm, gw_ref, gate_ref, g_ref, o_ref, ybuf, sem, *, n_tokens):
    xn = _combine_gather(pos_ref, x_ref, y_hbm, gw_ref, gate_ref, ybuf, sem, n_tokens=n_tokens)
    ms = jnp.mean(xn * xn, axis=-1, keepdims=True)
    o_ref[...] = xn * lax.rsqrt(ms + EPS) * g_ref[...]


def _ffn_combine(x, ys, pos_flat, gw_t, gate, norm_args, *, seq, final):
    n, d = x.shape
    tm = min(256, seq)
    per = seq // tm
    tile = pl.BlockSpec((tm, d), lambda i, p: (i, 0))
    vec = pl.BlockSpec((None, 1, d), lambda i, p: (i // per, 0, 0))
    in_specs = [tile, pl.BlockSpec(memory_space=pl.ANY), pl.BlockSpec((tm, TOP_K), lambda i, p: (i, 0)), vec]
    scratch = [pltpu.VMEM((2, TOP_K, tm, d), F32), pltpu.SemaphoreType.DMA((2,))]
    if final:
        body, name = _combine_final_body, "ffn_combine_final"
        in_specs.append(pl.BlockSpec((1, d), lambda i, p: (0, 0)))
        out_specs, out_shape = tile, jax.ShapeDtypeStruct((n, d), F32)
    else:
        body, name = _combine_norm_body, "ffn_combine_norm"
        in_specs += [vec, vec]
        out_specs = [tile, tile]
        out_shape = [jax.ShapeDtypeStruct((n, d), F32), jax.ShapeDtypeStruct((n, d), BF16)]
    grid_spec = pltpu.PrefetchScalarGridSpec(num_scalar_prefetch=1, grid=(n // tm,), in_specs=in_specs,
                                             out_specs=out_specs, scratch_shapes=scratch)
    return pl.pallas_call(
        functools.partial(body, n_tokens=n), grid_spec=grid_spec, out_shape=out_shape,
        compiler_params=_params(("arbitrary",)), name=name,
    )(pos_flat, x, ys, gw_t, gate, *norm_args)


def _moe_layer(x, geff, shift, rwt, rb, w1, w3, w2, *, seq, layer):
    n, _ = x.shape
    n_e = rwt.shape[0]
    hn, e, gw, rank, cnt = _ffn_prep(x, geff, shift, rwt, rb, seq=seq)
    n_tiles_max = (TOP_K * n) // MOE_TILE + n_e
    pos, tile_expert, tile_rows, fill = _moe_plan(e, rank, cnt, n_tiles_max)
    pos_flat = pos.reshape(-1)
    xs = _moe_dispatch(hn, pos_flat, fill, rows=n_tiles_max * MOE_TILE)
    ys = _moe_experts(xs, w1, w3, w2, tile_expert, tile_rows, layer=layer)
    return ys, pos_flat, gw.T


def kernel(x, c, norm_mix_g, norm_ffn_g, norm_final_g, ada_mix_w, ada_mix_b, ada_ffn_w, ada_ffn_b, s5_w_in, s5_lam_re, s5_lam_im, s5_log_step, s5_b_re, s5_b_im, s5_c_re, s5_c_im, s5_d, s5_w_glu, s5_b_glu, s5_w_out, gm_w_in, gm_b_in, gm_ln_g, gm_ln_b, gm_ws, gm_bs, gm_w_out, router_w, router_bias, moe_w1, moe_w3, moe_w2):
    bsz, seq, d = x.shape
    n = bsz * seq
    depth = norm_mix_g.shape[0]
    assert depth == 2 and seq % (CHUNK * SEGMENTS) == 0
    cb = seq // CHUNK
    nc = n // CHUNK

    mods_mix = _adaln(c, ada_mix_w, ada_mix_b)
    mods_ffn = _adaln(c, ada_ffn_w, ada_ffn_b)

    def split(m, g):
        shift, scale, gate = jnp.split(m, 3, axis=-1)
        return (g[None] * (1.0 + scale))[:, None], shift[:, None], gate[:, None]

    rwt = router_w.T
    rb = router_bias.reshape(-1, 1)
    xf = x.reshape(n, d)

    geff, shift, gate = split(mods_mix[0], norm_mix_g[0])
    x3 = xf.reshape(nc, CHUNK, d)
    hn8 = _norm_to_planes(x3, geff, shift, chunks_per_batch=cb)
    u8 = _matmul(hn8, s5_w_in[0].astype(BF16), tm=1024, tn=1024, out_dtype=BF16, epilogue=_ep_identity, name="s5_in")
    wa, wc, a1, aseg = _s5_weights(s5_lam_re[0], s5_lam_im[0], s5_log_step[0], s5_b_re[0], s5_b_im[0],
                                   s5_c_re[0], s5_c_im[0], seg_rows=cb // SEGMENTS)
    y8 = _s5_ssm(u8, wa, wc, a1, aseg, s5_d[0].reshape(1, -1), chunks_per_batch=cb,
                 group=s5_b_re.shape[-1], state=s5_b_re.shape[-2])
    w = y8.shape[-1]
    tmg, tng = min(1024, nc), min(1024, w)
    z8 = _matmul(y8, s5_w_glu[0].astype(BF16), tm=tmg, tn=tng, out_dtype=BF16, epilogue=_ep_glu,
                 extras=[(y8, pl.BlockSpec((None, tmg, tng), lambda q, i, j: (q, i, j))),
                         (s5_b_glu[0].reshape(1, -1), pl.BlockSpec((1, tng), lambda q, i, j: (0, j)))],
                 name="s5_glu")
    x1 = _s5_out(z8, s5_w_out[0].astype(BF16), x3, gate, chunks_per_batch=cb).reshape(n, d)

    geff, shift, gate = split(mods_ffn[0], norm_ffn_g[0])
    w1b, w3b, w2b = moe_w1.astype(BF16), moe_w3.astype(BF16), moe_w2.astype(BF16)
    ys, pos, gw_t = _moe_layer(x1, geff, shift, rwt, rb, w1b, w3b, w2b, seq=seq, layer=0)
    geff1, shift1, gate1 = split(mods_mix[1], norm_mix_g[1])
    x2, hn = _ffn_combine(x1, ys, pos, gw_t, gate, (geff1, shift1), seq=seq, final=False)

    w2n = gm_w_in.shape[-1]
    tm_in, tn_in = min(1024, n), min(1024, w2n)
    zz = _matmul(hn[None], gm_w_in[0].astype(BF16), tm=tm_in, tn=tn_in, out_dtype=BF16, epilogue=_ep_bias_gelu,
                 extras=[(gm_b_in[0].reshape(1, -1), pl.BlockSpec((1, tn_in), lambda q, i, j: (0, j)))],
                 name="gmlp_in")[0]
    chunk = gm_ws.shape[-1]
    gated = _gm_gate(zz, gm_ln_g[0].reshape(1, -1), gm_ln_b[0].reshape(1, -1), gm_ws[0].astype(BF16),
                     gm_bs[0].T, chunk=chunk)
    tm_o, tn_o = min(1024, seq), min(512, d)
    per = seq // tm_o
    x3_ = _matmul(gated[None], gm_w_out[0].astype(BF16), tm=tm_o, tn=tn_o, out_dtype=F32, epilogue=_ep_residual,
                  extras=[(x2[None], pl.BlockSpec((None, tm_o, tn_o), lambda q, i, j: (q, i, j))),
                          (gate1, pl.BlockSpec((None, 1, tn_o), lambda q, i, j: (i // per, 0, j)))],
                  name="gmlp_out")[0]

    geff, shift, gate = split(mods_ffn[1], norm_ffn_g[1])
    ys, pos, gw_t = _moe_layer(x3_, geff, shift, rwt, rb, w1b, w3b, w2b, seq=seq, layer=1)
    out = _ffn_combine(x3_, ys, pos, gw_t, gate, (norm_final_g.reshape(1, -1),), seq=seq, final=True)
    return out.reshape(bsz, seq, d)
```

```python
import functools

import jax
import jax.numpy as jnp
from jax import lax
from jax.experimental import pallas as pl
from jax.experimental.pallas import tpu as pltpu

EPS = 1e-6
CHUNK = 8
LANES = 128
SEGMENTS = 8
N_GROUPS_MOE = 4
TOP_K = 2
MOE_TILE = 256
VMEM_LIMIT = 56 * 1024 * 1024

F32 = jnp.float32
BF16 = jnp.bfloat16


def _params(sem):
    return pltpu.CompilerParams(dimension_semantics=sem, vmem_limit_bytes=VMEM_LIMIT)


def _rms_mod(x, geff, shift):
    ms = jnp.mean(x * x, axis=-1, keepdims=True)
    return x * lax.rsqrt(ms + EPS) * geff + shift


def _ada_body(ct_ref, w_ref, b_ref, o_ref):
    ct = ct_ref[...]
    st = ct * jax.nn.sigmoid(ct)
    w = w_ref[...]
    rows = [jnp.sum(st[:, b:b + 1] * w, axis=0, keepdims=True) for b in range(ct.shape[1])]
    o_ref[...] = jnp.concatenate(rows, axis=0) + b_ref[...]


def _adaln(c, w, b):
    depth, d, d3 = w.shape
    bsz = c.shape[0]
    tn = 1024 if d3 % 1024 == 0 else 512
    assert d3 % tn == 0
    return pl.pallas_call(
        _ada_body,
        grid=(depth, d3 // tn),
        in_specs=[
            pl.BlockSpec((d, bsz), lambda l, j: (0, 0)),
            pl.BlockSpec((None, d, tn), lambda l, j: (l, 0, j)),
            pl.BlockSpec((None, 1, tn), lambda l, j: (l, 0, j)),
        ],
        out_specs=pl.BlockSpec((None, bsz, tn), lambda l, j: (l, 0, j)),
        out_shape=jax.ShapeDtypeStruct((depth, bsz, d3), F32),
        compiler_params=_params(("arbitrary", "arbitrary")),
        name="adaln",
    )(c.T, w, b.reshape(depth, 1, d3))


def _mm_body(*refs, epilogue, n_extra):
    a_ref, w_ref = refs[0], refs[1]
    extra = refs[2:2 + n_extra]
    o_ref = refs[2 + n_extra]
    acc = jnp.dot(a_ref[...], w_ref[...], preferred_element_type=F32)
    o_ref[...] = epilogue(acc, *extra).astype(o_ref.dtype)


def _matmul(a, w, *, tm, tn, out_dtype, epilogue, extras=(), name):
    p, m, k = a.shape
    n = w.shape[1]
    tm, tn = min(tm, m), min(tn, n)
    in_specs = [
        pl.BlockSpec((None, tm, k), lambda q, i, j: (q, i, 0)),
        pl.BlockSpec((k, tn), lambda q, i, j: (0, j)),
    ] + [s for _, s in extras]
    return pl.pallas_call(
        functools.partial(_mm_body, epilogue=epilogue, n_extra=len(extras)),
        grid=(p, m // tm, n // tn),
        in_specs=in_specs,
        out_specs=pl.BlockSpec((None, tm, tn), lambda q, i, j: (q, i, j)),
        out_shape=jax.ShapeDtypeStruct((p, m, n), out_dtype),
        compiler_params=_params(("arbitrary", "arbitrary", "arbitrary")),
        name=name,
    )(a, w, *[x for x, _ in extras])


def _ep_identity(acc):
    return acc


def _ep_glu(acc, y_ref, b_ref):
    return y_ref[...].astype(F32) * jax.nn.sigmoid(acc + b_ref[...])


def _ep_bias_gelu(acc, b_ref):
    return jax.nn.gelu(acc + b_ref[...])


def _ep_residual(acc, x_ref, g_ref):
    return x_ref[...] + g_ref[...] * acc


def _plane_rows(block, seg, *, cb, kb):
    per = cb // (kb * SEGMENTS)
    return pl.ds((block // per) * cb + seg * (cb // SEGMENTS) + (block % per) * kb, kb)


def _norm_plane_body(x_hbm, g_ref, s_ref, o_ref, buf, sem, *, cb):
    i = pl.program_id(0)
    nsteps = pl.num_programs(0)
    kb = buf.shape[2]
    slot = i % 2

    def fetch(step, sl, t, s):
        return pltpu.make_async_copy(x_hbm.at[_plane_rows(step, s, cb=cb, kb=kb), t], buf.at[sl, t, :, s], sem.at[sl, t])

    def start_all(step, sl):
        for t in range(CHUNK):
            for s in range(SEGMENTS):
                fetch(step, sl, t, s).start()

    @pl.when(i == 0)
    def _():
        start_all(0, 0)

    @pl.when(i + 1 < nsteps)
    def _():
        start_all(i + 1, 1 - slot)

    for t in range(CHUNK):
        for s in range(SEGMENTS):
            fetch(i, slot, t, s).wait()
        xt = buf[slot, t].reshape(kb * SEGMENTS, buf.shape[-1])
        o_ref[t] = _rms_mod(xt, g_ref[...], s_ref[...]).astype(o_ref.dtype)


def _norm_to_planes(x3, geff, shift, *, chunks_per_batch):
    nc, _, d = x3.shape
    kb = 4
    tc = kb * SEGMENTS
    per = chunks_per_batch // tc
    vec = pl.BlockSpec((None, 1, d), lambda i: (i // per, 0, 0))
    return pl.pallas_call(
        functools.partial(_norm_plane_body, cb=chunks_per_batch),
        grid=(nc // tc,),
        in_specs=[pl.BlockSpec(memory_space=pl.ANY), vec, vec],
        out_specs=pl.BlockSpec((CHUNK, tc, d), lambda i: (0, i, 0)),
        out_shape=jax.ShapeDtypeStruct((CHUNK, nc, d), BF16),
        scratch_shapes=[pltpu.VMEM((2, CHUNK, kb, SEGMENTS, d), F32), pltpu.SemaphoreType.DMA((2, CHUNK))],
        compiler_params=_params(("arbitrary",)),
        name="s5_norm_planes",
    )(x3, geff, shift)


def _cmul_add(ar, ai, xr, xi, vr, vi):
    return ar * xr - ai * xi + vr, ar * xi + ai * xr + vi


def _ssm_body(u_ref, wac_ref, wcc_ref, a1_ref, aseg_ref, d_ref, y_ref, xf_ref, xb_ref, wa_ref, wc_ref, *,
              rows_per_dot, gl, hsz, psz):
    cb = u_ref.shape[1]
    ks = cb // SEGMENTS
    half = xf_ref.shape[-1]
    q = half // 2

    @pl.when(pl.program_id(1) == 0)
    def _():
        iota = lambda shape, ax: lax.broadcasted_iota(jnp.int32, shape, ax)
        ca, na = wac_ref.shape[1], wa_ref.shape[1]
        dcol = iota((ca, na), 1)
        spread = (iota((ca, na), 0) == (dcol // (gl * psz)) * psz + dcol % psz).astype(BF16)
        rep = jnp.dot(wac_ref[...], spread, preferred_element_type=F32)
        shp = rep.shape
        keep = (iota(shp, 0) // hsz) % gl == (iota(shp, 1) // psz) % gl
        wa_ref[...] = jnp.where(keep, rep, 0.0).astype(wa_ref.dtype)
        cc, nc_ = wcc_ref.shape[1], wc_ref.shape[1]
        dcol = iota((cc, nc_), 1)
        spread = (iota((cc, nc_), 0) == (dcol // (gl * hsz)) * hsz + dcol % hsz).astype(BF16)
        rep = jnp.dot(wcc_ref[...], spread, preferred_element_type=F32)
        shp = rep.shape
        n_tok_rows = CHUNK * gl * hsz
        r = iota(shp, 0)
        row_g = jnp.where(r < n_tok_rows, (r // hsz) % gl, ((r - n_tok_rows) // psz) % gl)
        keep = row_g == (iota(shp, 1) // hsz) % gl
        wc_ref[...] = jnp.where(keep, rep, 0.0).astype(wc_ref.dtype)

    def u_rows(r0, nr):
        return jnp.concatenate([u_ref[t, pl.ds(r0, nr), :] for t in range(CHUNK)], axis=1)

    nr = min(cb, rows_per_dot)
    for r0 in range(0, cb, nr):
        res = jnp.dot(u_rows(r0, nr), wa_ref[...], preferred_element_type=F32)
        xf_ref[pl.ds(r0, nr), :] = res[:, :half]
        xb_ref[pl.ds(r0, nr), :] = res[:, half:]

    afr, afi, abr, abi = (a1_ref[i:i + 1, :] for i in range(4))
    group = lambda j: pl.ds(pl.multiple_of(j * SEGMENTS, SEGMENTS), SEGMENTS)

    def p1(j, carry):
        cfr, cfi, cbr, cbi = carry
        jb = ks - 1 - j
        vf = xf_ref[group(j), :]
        vb = xb_ref[group(jb), :]
        xf_ref[group(j), :] = jnp.concatenate([cfr, cfi], axis=1)
        xb_ref[group(jb), :] = jnp.concatenate([cbr, cbi], axis=1)
        nfr, nfi = _cmul_add(afr, afi, cfr, cfi, vf[:, :q], vf[:, q:])
        nbr, nbi = _cmul_add(abr, abi, cbr, cbi, vb[:, :q], vb[:, q:])
        return nfr, nfi, nbr, nbi

    z = jnp.zeros((SEGMENTS, q), F32)
    efr, efi, ebr, ebi = lax.fori_loop(0, ks, p1, (z, z, z, z))

    sfr, sfi, sbr, sbi = (aseg_ref[i:i + 1, :] for i in range(4))
    z1 = jnp.zeros((1, q), F32)
    gfr, gfi = [z1], [z1]
    for s in range(SEGMENTS - 1):
        r, i = _cmul_add(sfr, sfi, gfr[-1], gfi[-1], efr[s:s + 1], efi[s:s + 1])
        gfr.append(r)
        gfi.append(i)
    gbr, gbi = [z1], [z1]
    for s in range(SEGMENTS - 1, 0, -1):
        r, i = _cmul_add(sbr, sbi, gbr[0], gbi[0], ebr[s:s + 1], ebi[s:s + 1])
        gbr.insert(0, r)
        gbi.insert(0, i)
    gfr, gfi, gbr, gbi = (jnp.concatenate(g, axis=0) for g in (gfr, gfi, gbr, gbi))

    def p2(j, carry):
        pfr, pfi, pbr, pbi = carry
        jb = ks - 1 - j
        cf = xf_ref[group(j), :]
        cb_ = xb_ref[group(jb), :]
        xf_ref[group(j), :] = jnp.concatenate([cf[:, :q] + pfr * gfr - pfi * gfi, cf[:, q:] + pfr * gfi + pfi * gfr], axis=1)
        xb_ref[group(jb), :] = jnp.concatenate([cb_[:, :q] + pbr * gbr - pbi * gbi, cb_[:, q:] + pbr * gbi + pbi * gbr], axis=1)
        return (pfr * afr - pfi * afi, pfr * afi + pfi * afr, pbr * abr - pbi * abi, pbr * abi + pbi * abr)

    one = jnp.ones((1, q), F32)
    lax.fori_loop(0, ks, p2, (one, jnp.zeros((1, q), F32), one, jnp.zeros((1, q), F32)))

    lw = d_ref.shape[-1]
    for r0 in range(0, cb, nr):
        xf = xf_ref[pl.ds(r0, nr), :]
        xb = xb_ref[pl.ds(r0, nr), :]
        lhs = jnp.concatenate([u_rows(r0, nr), xf.astype(BF16), xb.astype(BF16)], axis=1)
        acc = jnp.dot(lhs, wc_ref[...], preferred_element_type=F32)
        for t in range(CHUNK):
            yt = acc[:, t * lw:(t + 1) * lw] + d_ref[...] * u_ref[t, pl.ds(r0, nr), :].astype(F32)
            y_ref[t, pl.ds(r0, nr), :] = jax.nn.gelu(yt).astype(y_ref.dtype)


def _s5_ssm(u8, wa, wc, a1, aseg, dvec, *, chunks_per_batch, group, state):
    _, nc, w = u8.shape
    nslab = w // LANES
    nb = nc // chunks_per_batch
    cb = chunks_per_batch
    ks = cb // SEGMENTS
    gl = LANES // group
    half = 2 * gl * state
    tok = CHUNK * LANES
    body = functools.partial(_ssm_body, rows_per_dot=512, gl=gl, hsz=group, psz=state)
    return pl.pallas_call(
        body,
        grid=(nslab, nb),
        in_specs=[
            pl.BlockSpec((CHUNK, cb, LANES), lambda s, b: (0, b, s)),
            pl.BlockSpec((None,) + wa.shape[1:], lambda s, b: (s, 0, 0)),
            pl.BlockSpec((None,) + wc.shape[1:], lambda s, b: (s, 0, 0)),
            pl.BlockSpec((None,) + a1.shape[1:], lambda s, b: (s, 0, 0)),
            pl.BlockSpec((None,) + aseg.shape[1:], lambda s, b: (s, 0, 0)),
            pl.BlockSpec((1, LANES), lambda s, b: (0, s)),
        ],
        out_specs=pl.BlockSpec((CHUNK, cb, LANES), lambda s, b: (0, b, s)),
        out_shape=jax.ShapeDtypeStruct(u8.shape, BF16),
        scratch_shapes=[
            pltpu.VMEM((cb, half), F32),
            pltpu.VMEM((cb, half), F32),
            pltpu.VMEM((tok, 2 * half), BF16),
            pltpu.VMEM((tok + 2 * half, tok), BF16),
        ],
        compiler_params=_params(("arbitrary", "arbitrary")),
        name="s5_ssm",
    )(u8, wa, wc, a1, aseg, dvec)


def _s5_weights(lam_re, lam_im, log_step, b_re, b_im, c_re, c_im, *, seg_rows):
    _, g, p = lam_re.shape
    h = b_re.shape[-1]
    gl = LANES // h
    s = g // gl
    t = CHUNK
    lam = lax.complex(lam_re.astype(F32), lam_im.astype(F32))
    zed = lam * jnp.exp(log_step.astype(F32))[..., None]
    lam_bar = jnp.exp(zed)
    bbar = ((lam_bar - 1.0) / lam)[..., None] * lax.complex(b_re.astype(F32), b_im.astype(F32))
    cmat = lax.complex(c_re.astype(F32), c_im.astype(F32))
    pw = jnp.exp(zed[:, None] * jnp.arange(t + 1, dtype=F32)[None, :, None, None])
    pw_down = jnp.exp(zed[:, None] * (t - jnp.arange(t + 1, dtype=F32))[None, :, None, None])

    ma_f = pw_down[0, 1:, :, :, None] * bbar[0][None]
    ma_b = pw[1, :t, :, :, None] * bbar[1][None]
    wa = jnp.stack([ma_f.real, ma_f.imag, ma_b.real, ma_b.imag], axis=0)
    wa = wa.reshape(4, t, s, gl, p, h).transpose(2, 1, 3, 5, 0, 4).reshape(s, t * gl * h, 4 * p)

    kf = jnp.einsum('gyp,dgp,gph->dgyh', cmat[0], pw[0, :t], bbar[0]).real
    kb = jnp.einsum('gyp,dgp,gph->dgyh', cmat[1], pw[1, :t], bbar[1]).real
    st = jnp.arange(t)
    lag = st[None, None, :] - st[None, :, None]
    sel_f = (lag == st[:, None, None]).astype(F32)
    sel_b = (-lag == st[:, None, None]).astype(F32)
    coef = jnp.einsum('dat,dgyh->atgyh', sel_f, kf) + jnp.einsum('dat,dgyh->atgyh', sel_b, kb)
    w_u = coef.reshape(t, t, s, gl, h, h).transpose(2, 0, 3, 5, 1, 4).reshape(s, t * gl * h, t * h)

    cl_f = cmat[0][None] * pw[0, 1:][:, :, None, :]
    cl_b = cmat[1][None] * pw_down[1, :t][:, :, None, :]
    def state_rows(x):
        return x.reshape(t, s, gl, h, p).transpose(1, 2, 4, 0, 3).reshape(s, gl * p, t * h)
    wc = jnp.concatenate([w_u, state_rows(cl_f.real), state_rows(-cl_f.imag),
                          state_rows(cl_b.real), state_rows(-cl_b.imag)], axis=1)

    def lanes(x):
        x = x.reshape(2, s, gl * p)
        return jnp.stack([x[0].real, x[0].imag, x[1].real, x[1].imag], axis=1)
    a1 = lanes(pw[:, t])
    aseg = lanes(jnp.exp(zed * float(t * seg_rows)))
    return wa.astype(BF16), wc.astype(BF16), a1, aseg


def _s5_out_body(a_ref, w_ref, g_ref, x_hbm, o_hbm, xin, xout, isem, osem, *, cb):
    nj = pl.num_programs(1)
    step = pl.program_id(0) * nj + pl.program_id(1)
    total = pl.num_programs(0) * nj
    tm, tn = a_ref.shape[1], w_ref.shape[1]
    kb = tm // SEGMENTS
    slot = step % 2

    def window(st, t, s):
        cols = pl.ds(pl.multiple_of((st % nj) * tn, tn), tn)
        return (_plane_rows(st // nj, s, cb=cb, kb=kb), t, cols)

    def fetch(st, sl, t, s):
        return pltpu.make_async_copy(x_hbm.at[window(st, t, s)], xin.at[sl, t, :, s], isem.at[sl, t])

    def put(st, sl, t, s):
        return pltpu.make_async_copy(xout.at[sl, t, :, s], o_hbm.at[window(st, t, s)], osem.at[sl, t])

    def each(fn):
        for t in range(CHUNK):
            for s in range(SEGMENTS):
                fn(t, s)

    @pl.when(step == 0)
    def _():
        each(lambda t, s: fetch(0, 0, t, s).start())

    @pl.when(step + 1 < total)
    def _():
        each(lambda t, s: fetch(step + 1, 1 - slot, t, s).start())

    a = a_ref[...].reshape(CHUNK * tm, a_ref.shape[2])
    acc = jnp.dot(a, w_ref[...], preferred_element_type=F32)

    @pl.when(step >= 2)
    def _():
        each(lambda t, s: put(step - 2, slot, t, s).wait())

    for t in range(CHUNK):
        for s in range(SEGMENTS):
            fetch(step, slot, t, s).wait()
        xt = xin[slot, t].reshape(tm, tn) + g_ref[...] * acc[t * tm:(t + 1) * tm]
        xout[slot, t] = xt.reshape(kb, SEGMENTS, tn)
    each(lambda t, s: put(step, slot, t, s).start())

    @pl.when(step == total - 1)
    def _():
        each(lambda t, s: put(step, slot, t, s).wait())

        @pl.when(total >= 2)
        def _():
            each(lambda t, s: put(step - 1, 1 - slot, t, s).wait())


def _s5_out(z8, w, x3, gate, *, chunks_per_batch):
    _, nc, k = z8.shape
    d = w.shape[1]
    tm = min(128, chunks_per_batch)
    tn = min(512, d)
    per = chunks_per_batch // tm
    kb = tm // SEGMENTS
    return pl.pallas_call(
        functools.partial(_s5_out_body, cb=chunks_per_batch),
        grid=(nc // tm, d // tn),
        in_specs=[
            pl.BlockSpec((CHUNK, tm, k), lambda i, j: (0, i, 0)),
            pl.BlockSpec((k, tn), lambda i, j: (0, j)),
            pl.BlockSpec((None, 1, tn), lambda i, j: (i // per, 0, j)),
            pl.BlockSpec(memory_space=pl.ANY),
        ],
        out_specs=pl.BlockSpec(memory_space=pl.ANY),
        out_shape=jax.ShapeDtypeStruct(x3.shape, F32),
        scratch_shapes=[
            pltpu.VMEM((2, CHUNK, kb, SEGMENTS, tn), F32),
            pltpu.VMEM((2, CHUNK, kb, SEGMENTS, tn), F32),
            pltpu.SemaphoreType.DMA((2, CHUNK)),
            pltpu.SemaphoreType.DMA((2, CHUNK)),
        ],
        compiler_params=_params(("arbitrary", "arbitrary")),
        name="s5_out",
    )(z8, w, gate, x3)


def _gm_gate_body(z_ref, lng_ref, lnb_ref, ws_ref, bs_ref, o_ref, *, chunk):
    tm, w2 = z_ref.shape
    w = w2 // 2
    heads = ws_ref.shape[0]
    hd = w // heads
    v = z_ref[:, w:].astype(F32)
    mu = jnp.mean(v, axis=-1, keepdims=True)
    vc = v - mu
    var = jnp.mean(vc * vc, axis=-1, keepdims=True)
    vn = (vc * lax.rsqrt(var + EPS) * lng_ref[...] + lnb_ref[...]).astype(BF16)
    for c in range(tm // chunk):
        rows = slice(c * chunk, (c + 1) * chunk)
        for hh in range(heads):
            cols = slice(hh * hd, (hh + 1) * hd)
            sv = jnp.dot(ws_ref[hh], vn[rows, cols], preferred_element_type=F32) + bs_ref[:, hh:hh + 1]
            o_ref[rows, cols] = (z_ref[rows, cols].astype(F32) * sv).astype(o_ref.dtype)


def _gm_gate(z, ln_g, ln_b, ws, bs, *, chunk):
    n, w2 = z.shape
    w = w2 // 2
    tm = 2 * chunk
    full = lambda a: pl.BlockSpec(a.shape, lambda i: (0,) * a.ndim)
    return pl.pallas_call(
        functools.partial(_gm_gate_body, chunk=chunk),
        grid=(n // tm,),
        in_specs=[pl.BlockSpec((tm, w2), lambda i: (i, 0)), full(ln_g), full(ln_b), full(ws), full(bs)],
        out_specs=pl.BlockSpec((tm, w), lambda i: (i, 0)),
        out_shape=jax.ShapeDtypeStruct((n, w), BF16),
        compiler_params=_params(("arbitrary",)),
        name="gmlp_gate",
    )(z, ln_g, ln_b, ws, bs)


def _pack_bf16_pairs(x):
    half = x.shape[1] // 2
    bits = lambda v: lax.bitcast_convert_type(v.astype(F32), jnp.int32)
    return bits(x[:, half:]) | lax.shift_right_logical(bits(x[:, :half]), 16)


def _unpack_bf16_pairs(w):
    lo = lax.bitcast_convert_type(lax.shift_left(w, 16), F32)
    hi = lax.bitcast_convert_type(w & jnp.int32(-65536), F32)
    return jnp.concatenate([lo.astype(BF16), hi.astype(BF16)], axis=1)


def _top2_sum(a, b, c, d):
    m1, n1 = jnp.maximum(a, b), jnp.minimum(a, b)
    m2, n2 = jnp.maximum(c, d), jnp.minimum(c, d)
    return jnp.maximum(m1, m2) + jnp.maximum(jnp.minimum(m1, m2), jnp.maximum(n1, n2))


def _first_argmax(vals):
    best, idx = vals[0], jnp.zeros(vals[0].shape, jnp.int32)
    for j in range(1, len(vals)):
        upd = vals[j] > best
        idx = jnp.where(upd, j, idx)
        best = jnp.where(upd, vals[j], best)
    return idx, best


def _ffn_prep_body(x_ref, g_ref, s_ref, rwt_ref, rb_ref, hn_ref, e_ref, gw_ref, rank_ref, cnt_ref, carry_ref):
    i = pl.program_id(0)
    n_e = rwt_ref.shape[1]
    epg = n_e // N_GROUPS_MOE
    tm = x_ref.shape[0]

    @pl.when(i == 0)
    def _():
        carry_ref[...] = jnp.zeros_like(carry_ref)

    hn = _rms_mod(x_ref[...], g_ref[...], s_ref[...])
    hn_hi = hn.astype(BF16)
    hn_lo = (hn - hn_hi.astype(F32)).astype(BF16)
    hn_ref[...] = _pack_bf16_pairs(hn_hi)
    nt_dot = lambda a, b: lax.dot_general(a, b, (((1,), (1,)), ((), ())), preferred_element_type=F32)
    logits = nt_dot(rwt_ref[0], hn_hi) + (nt_dot(rwt_ref[1], hn_hi) + nt_dot(rwt_ref[0], hn_lo))
    score = jax.nn.sigmoid(logits)
    sel = score + rb_ref[...]

    row = lambda a, r: a[r:r + 1, :]
    gscore = [_top2_sum(*[row(sel, g * epg + j) for j in range(epg)]) for g in range(N_GROUPS_MOE)]
    bg, _ = _first_argmax(gscore)
    pick = lambda a, j: functools.reduce(
        lambda acc, g: jnp.where(bg == g, row(a, g * epg + j), acc), range(1, N_GROUPS_MOE), row(a, j))
    sel_in = [pick(sel, j) for j in range(epg)]
    sc_in = [pick(score, j) for j in range(epg)]
    i1, _ = _first_argmax(sel_in)
    neg = jnp.full_like(sel_in[0], -jnp.inf)
    i2, _ = _first_argmax([jnp.where(i1 == j, neg, sel_in[j]) for j in range(epg)])
    take = lambda vals, idx: functools.reduce(lambda acc, j: jnp.where(idx == j, vals[j], acc), range(1, epg), vals[0])
    w1, w2 = take(sc_in, i1), take(sc_in, i2)
    den = w1 + w2
    e1, e2 = bg * epg + i1, bg * epg + i2
    e_ref[...] = jnp.concatenate([e1, e2], axis=0)
    gw_ref[...] = jnp.concatenate([w1 / den, w2 / den], axis=0)

    eid = lax.broadcasted_iota(jnp.int32, (n_e, tm), 0)
    oh1 = (eid == e1).astype(F32)
    oh2 = (eid == e2).astype(F32)
    oh = oh1 + oh2
    tri = (lax.broadcasted_iota(jnp.int32, (tm, tm), 0) <= lax.broadcasted_iota(jnp.int32, (tm, tm), 1)).astype(BF16)
    incl = jnp.dot(oh.astype(BF16), tri, preferred_element_type=F32)
    before = carry_ref[:, 0:1] + incl - oh
    r1 = jnp.sum(oh1 * before, axis=0, keepdims=True)
    r2 = jnp.sum(oh2 * before, axis=0, keepdims=True)
    rank_ref[...] = jnp.concatenate([r1, r2], axis=0).astype(jnp.int32)
    carry_ref[...] = carry_ref[...] + incl[:, tm - 1:tm]
    cnt_ref[...] = carry_ref[...].astype(jnp.int32)


def _ffn_prep(x, geff, shift, rwt, rb, *, seq):
    n, d = x.shape
    n_e = rwt.shape[1]
    tm = min(256, seq)
    per = seq // tm
    vec = pl.BlockSpec((None, 1, d), lambda i: (i // per, 0, 0))
    row2 = pl.BlockSpec((TOP_K, tm), lambda i: (0, i))
    hn, e, gw, rank, cnt = pl.pallas_call(
        _ffn_prep_body,
        grid=(n // tm,),
        in_specs=[pl.BlockSpec((tm, d), lambda i: (i, 0)), vec, vec,
                  pl.BlockSpec((2, n_e, d), lambda i: (0, 0, 0)), pl.BlockSpec((n_e, 1), lambda i: (0, 0))],
        out_specs=[pl.BlockSpec((tm, d // 2), lambda i: (i, 0)), row2, row2, row2,
                   pl.BlockSpec((n_e, LANES), lambda i: (0, 0))],
        out_shape=[jax.ShapeDtypeStruct((n, d // 2), jnp.int32), jax.ShapeDtypeStruct((TOP_K, n), jnp.int32),
                   jax.ShapeDtypeStruct((TOP_K, n), F32), jax.ShapeDtypeStruct((TOP_K, n), jnp.int32),
                   jax.ShapeDtypeStruct((n_e, LANES), jnp.int32)],
        scratch_shapes=[pltpu.VMEM((n_e, LANES), F32)],
        compiler_params=_params(("arbitrary",)),
        name="ffn_prep",
    )(x, geff, shift, rwt, rb)
    return hn, e, gw, rank, cnt[:, 0]


def _for_rows(lo, hi, fn):
    def body(r, c):
        fn(r)
        return c
    lax.fori_loop(lo, hi, body, 0)


def _dispatch_body(pos_ref, fill_ref, hn_ref, xs_hbm, zero_ref, sem, zsem, *, n_tokens):
    i = pl.program_id(0)
    tm = hn_ref.shape[0]

    def row_copy(r, k):
        return pltpu.make_async_copy(hn_ref.at[pl.ds(r, 1)], xs_hbm.at[pl.ds(pos_ref[k * n_tokens + i * tm + r], 1)], sem)

    def start(r):
        for k in range(TOP_K):
            row_copy(r, k).start()

    _for_rows(0, tm, start)

    @pl.when(i == 0)
    def _():
        zero_ref[...] = jnp.zeros_like(zero_ref)
        n_fill = fill_ref.shape[0] // 2

        def zero_copy(row):
            return pltpu.make_async_copy(zero_ref, xs_hbm.at[pl.ds(row, 1)], zsem)

        for e in range(n_fill):
            _for_rows(fill_ref[2 * e], fill_ref[2 * e + 1], lambda row: zero_copy(row).start())
        for e in range(n_fill):
            _for_rows(fill_ref[2 * e], fill_ref[2 * e + 1], lambda row: zero_copy(row).wait())

    for k in range(TOP_K):
        pltpu.make_async_copy(hn_ref, xs_hbm.at[pl.ds(0, tm)], sem).wait()


def _moe_dispatch(hn, pos_flat, fill, *, rows):
    n, dh = hn.shape
    tm = min(512, n)
    grid_spec = pltpu.PrefetchScalarGridSpec(
        num_scalar_prefetch=2,
        grid=(n // tm,),
        in_specs=[pl.BlockSpec((tm, dh), lambda i, p, f: (i, 0))],
        out_specs=pl.BlockSpec(memory_space=pl.ANY),
        scratch_shapes=[pltpu.VMEM((1, dh), jnp.int32), pltpu.SemaphoreType.DMA(()), pltpu.SemaphoreType.DMA(())],
    )
    return pl.pallas_call(
        functools.partial(_dispatch_body, n_tokens=n),
        grid_spec=grid_spec,
        out_shape=jax.ShapeDtypeStruct((rows, dh), jnp.int32),
        compiler_params=_params(("arbitrary",)),
        name="moe_dispatch",
    )(pos_flat, fill, hn)


def _moe_body(te_ref, nrows_ref, x_ref, w1_ref, w3_ref, w2_ref, o_ref):
    i = pl.program_id(0)
    used = nrows_ref[i] > 0

    @pl.when(used)
    def _():
        xb = _unpack_bf16_pairs(x_ref[...])
        h1 = jnp.dot(xb, w1_ref[...], preferred_element_type=F32)
        h3 = jnp.dot(xb, w3_ref[...], preferred_element_type=F32)
        hh = (h1 * jax.nn.sigmoid(h1) * h3).astype(BF16)
        o_ref[...] = jnp.dot(hh, w2_ref[...], preferred_element_type=F32)

    @pl.when(jnp.logical_not(used))
    def _():
        o_ref[...] = jnp.zeros_like(o_ref)


def _moe_experts(xs, w1, w3, w2, tile_expert, tile_rows, *, layer):
    rows, dh = xs.shape
    d = 2 * dh
    fdim = w1.shape[-1]
    tm = MOE_TILE
    resident = pl.Buffered(1)
    grid_spec = pltpu.PrefetchScalarGridSpec(
        num_scalar_prefetch=2,
        grid=(rows // tm,),
        in_specs=[
            pl.BlockSpec((tm, dh), lambda i, te, nr: (i, 0)),
            pl.BlockSpec((None, None, d, fdim), lambda i, te, nr: (layer, te[i], 0, 0), pipeline_mode=resident),
            pl.BlockSpec((None, None, d, fdim), lambda i, te, nr: (layer, te[i], 0, 0), pipeline_mode=resident),
            pl.BlockSpec((None, None, fdim, d), lambda i, te, nr: (layer, te[i], 0, 0), pipeline_mode=resident),
        ],
        out_specs=pl.BlockSpec((tm, d), lambda i, te, nr: (i, 0)),
    )
    return pl.pallas_call(
        _moe_body,
        grid_spec=grid_spec,
        out_shape=jax.ShapeDtypeStruct((rows, d), F32),
        compiler_params=_params(("arbitrary",)),
        name="moe_experts",
    )(tile_expert, tile_rows, xs, w1, w3, w2)


def _moe_plan(e, rank, cnt, n_tiles_max):
    tm = MOE_TILE
    n_e = cnt.shape[0]
    tiles_per = (cnt + tm - 1) // tm
    ends = jnp.cumsum(tiles_per)
    off = (ends - tiles_per) * tm
    onehot = e[..., None] == jnp.arange(n_e, dtype=jnp.int32)
    pos = jnp.sum(jnp.where(onehot, off, 0), axis=-1) + rank
    tile = jnp.arange(n_tiles_max, dtype=jnp.int32)
    tile_expert = jnp.minimum(jnp.sum(tile[:, None] >= ends[None, :], axis=1), n_e - 1).astype(jnp.int32)
    mine = tile_expert[:, None] == jnp.arange(n_e, dtype=jnp.int32)
    tile_rows = jnp.clip(jnp.sum(jnp.where(mine, off + cnt, 0), axis=1) - tile * tm, 0, tm)
    tile_rows = jnp.where(tile < ends[-1], tile_rows, 0).astype(jnp.int32)
    fill_end = jnp.where(jnp.arange(n_e) == n_e - 1, n_tiles_max * tm, ends * tm)
    fill = jnp.stack([off + cnt, fill_end], axis=1).reshape(-1).astype(jnp.int32)
    return pos.astype(jnp.int32), tile_expert, tile_rows, fill


COMBINE_ROW_CHUNKS = 8


def _combine_rows(pos_ref, x_ref, y_hbm, gw_ref, gate_ref, ybuf, sem, finish, *, n_tokens):
    i = pl.program_id(0)
    nsteps = pl.num_programs(0)
    tm = x_ref.shape[0]
    slot = i % 2
    nxt = (i + 1) % nsteps

    def row_copy(step, sl, r, k):
        row = pos_ref[k * n_tokens + step * tm + r]
        return pltpu.make_async_copy(y_hbm.at[pl.ds(row, 1)], ybuf.at[sl, k, pl.ds(r, 1)], sem.at[sl])

    def wait_tile(sl):
        for k in range(TOP_K):
            pltpu.make_async_copy(y_hbm.at[pl.ds(0, tm)], ybuf.at[sl, k], sem.at[sl]).wait()

    @pl.when(i == 0)
    def _():
        def one(r):
            for k in range(TOP_K):
                row_copy(0, 0, r, k).start()
        _for_rows(0, tm, one)

    wait_tile(slot)
    rc = tm // COMBINE_ROW_CHUNKS
    for c in range(COMBINE_ROW_CHUNKS):
        for r in range(c * rc, (c + 1) * rc):
            for k in range(TOP_K):
                row_copy(nxt, 1 - slot, r, k).start()
        rows = pl.ds(c * rc, rc)
        gw = gw_ref[rows, :]
        moe = gw[:, 0:1] * ybuf[slot, 0, rows, :] + gw[:, 1:2] * ybuf[slot, 1, rows, :]
        finish(rows, x_ref[rows, :] + gate_ref[...] * moe)

    @pl.when(i == nsteps - 1)
    def _():
        wait_tile(1 - slot)


def _combine_norm_body(pos_ref, x_ref, y_hbm, gw_ref, gate_ref, g_ref, s_ref, xo_ref, hn_ref, ybuf, sem, *, n_tokens):
    def finish(rows, xn):
        xo_ref[rows, :] = xn
        hn_ref[rows, :] = _rms_mod(xn, g_ref[...], s_ref[...]).astype(hn_ref.dtype)
    _combine_rows(pos_ref, x_ref, y_hbm, gw_ref, gate_ref, ybuf, sem, finish, n_tokens=n_tokens)


def _combine_final_body(pos_ref, x_ref, y_hbm, gw_ref, gate_ref, g_ref, o_ref, ybuf, sem, *, n_tokens):
    def finish(rows, xn):
        ms = jnp.mean(xn * xn, axis=-1, keepdims=True)
        o_ref[rows, :] = xn * lax.rsqrt(ms + EPS) * g_ref[...]
    _combine_rows(pos_ref, x_ref, y_hbm, gw_ref, gate_ref, ybuf, sem, finish, n_tokens=n_tokens)


def _ffn_combine(x, ys, pos_flat, gw_t, gate, norm_args, *, seq, final):
    n, d = x.shape
    tm = min(256, seq)
    per = seq // tm
    tile = pl.BlockSpec((tm, d), lambda i, p: (i, 0))
    vec = pl.BlockSpec((None, 1, d), lambda i, p: (i // per, 0, 0))
    in_specs = [tile, pl.BlockSpec(memory_space=pl.ANY), pl.BlockSpec((tm, TOP_K), lambda i, p: (i, 0)), vec]
    scratch = [pltpu.VMEM((2, TOP_K, tm, d), F32), pltpu.SemaphoreType.DMA((2,))]
    if final:
        body, name = _combine_final_body, "ffn_combine_final"
        in_specs.append(pl.BlockSpec((1, d), lambda i, p: (0, 0)))
        out_specs, out_shape = tile, jax.ShapeDtypeStruct((n, d), F32)
    else:
        body, name = _combine_norm_body, "ffn_combine_norm"
        in_specs += [vec, vec]
        out_specs = [tile, tile]
        out_shape = [jax.ShapeDtypeStruct((n, d), F32), jax.ShapeDtypeStruct((n, d), BF16)]
    grid_spec = pltpu.PrefetchScalarGridSpec(num_scalar_prefetch=1, grid=(n // tm,), in_specs=in_specs,
                                             out_specs=out_specs, scratch_shapes=scratch)
    return pl.pallas_call(
        functools.partial(body, n_tokens=n), grid_spec=grid_spec, out_shape=out_shape,
        compiler_params=_params(("arbitrary",)), name=name,
    )(pos_flat, x, ys, gw_t, gate, *norm_args)


def _moe_layer(x, geff, shift, rwt, rb, w1, w3, w2, *, seq, layer):
    n, _ = x.shape
    n_e = rwt.shape[1]
    hn, e, gw, rank, cnt = _ffn_prep(x, geff, shift, rwt, rb, seq=seq)
    n_tiles_max = (TOP_K * n) // MOE_TILE + n_e
    pos, tile_expert, tile_rows, fill = _moe_plan(e, rank, cnt, n_tiles_max)
    pos_flat = pos.reshape(-1)
    xs = _moe_dispatch(hn, pos_flat, fill, rows=n_tiles_max * MOE_TILE)
    ys = _moe_experts(xs, w1, w3, w2, tile_expert, tile_rows, layer=layer)
    return ys, pos_flat, gw.T


def kernel(x, c, norm_mix_g, norm_ffn_g, norm_final_g, ada_mix_w, ada_mix_b, ada_ffn_w, ada_ffn_b, s5_w_in, s5_lam_re, s5_lam_im, s5_log_step, s5_b_re, s5_b_im, s5_c_re, s5_c_im, s5_d, s5_w_glu, s5_b_glu, s5_w_out, gm_w_in, gm_b_in, gm_ln_g, gm_ln_b, gm_ws, gm_bs, gm_w_out, router_w, router_bias, moe_w1, moe_w3, moe_w2):
    bsz, seq, d = x.shape
    n = bsz * seq
    depth = norm_mix_g.shape[0]
    assert depth == 2 and seq % (CHUNK * SEGMENTS) == 0
    cb = seq // CHUNK
    nc = n // CHUNK

    mods_mix = _adaln(c, ada_mix_w, ada_mix_b)
    mods_ffn = _adaln(c, ada_ffn_w, ada_ffn_b)

    def split(m, g):
        shift, scale, gate = jnp.split(m, 3, axis=-1)
        return (g[None] * (1.0 + scale))[:, None], shift[:, None], gate[:, None]

    rw_hi = router_w.T.astype(BF16)
    rwt = jnp.stack([rw_hi, (router_w.T - rw_hi.astype(F32)).astype(BF16)])
    rb = router_bias.reshape(-1, 1)
    xf = x.reshape(n, d)

    geff, shift, gate = split(mods_mix[0], norm_mix_g[0])
    x3 = xf.reshape(nc, CHUNK, d)
    hn8 = _norm_to_planes(x3, geff, shift, chunks_per_batch=cb)
    u8 = _matmul(hn8, s5_w_in[0].astype(BF16), tm=1024, tn=1024, out_dtype=BF16, epilogue=_ep_identity, name="s5_in")
    wa, wc, a1, aseg = _s5_weights(s5_lam_re[0], s5_lam_im[0], s5_log_step[0], s5_b_re[0], s5_b_im[0],
                                   s5_c_re[0], s5_c_im[0], seg_rows=cb // SEGMENTS)
    y8 = _s5_ssm(u8, wa, wc, a1, aseg, s5_d[0].reshape(1, -1), chunks_per_batch=cb,
                 group=s5_b_re.shape[-1], state=s5_b_re.shape[-2])
    w = y8.shape[-1]
    tmg, tng = min(1024, nc), min(1024, w)
    z8 = _matmul(y8, s5_w_glu[0].astype(BF16), tm=tmg, tn=tng, out_dtype=BF16, epilogue=_ep_glu,
                 extras=[(y8, pl.BlockSpec((None, tmg, tng), lambda q, i, j: (q, i, j))),
                         (s5_b_glu[0].reshape(1, -1), pl.BlockSpec((1, tng), lambda q, i, j: (0, j)))],
                 name="s5_glu")
    x1 = _s5_out(z8, s5_w_out[0].astype(BF16), x3, gate, chunks_per_batch=cb).reshape(n, d)

    geff, shift, gate = split(mods_ffn[0], norm_ffn_g[0])
    w1b, w3b, w2b = moe_w1.astype(BF16), moe_w3.astype(BF16), moe_w2.astype(BF16)
    ys, pos, gw_t = _moe_layer(x1, geff, shift, rwt, rb, w1b, w3b, w2b, seq=seq, layer=0)
    geff1, shift1, gate1 = split(mods_mix[1], norm_mix_g[1])
    x2, hn = _ffn_combine(x1, ys, pos, gw_t, gate, (geff1, shift1), seq=seq, final=False)

    w2n = gm_w_in.shape[-1]
    tm_in, tn_in = min(1024, n), min(1024, w2n)
    zz = _matmul(hn[None], gm_w_in[0].astype(BF16), tm=tm_in, tn=tn_in, out_dtype=BF16, epilogue=_ep_bias_gelu,
                 extras=[(gm_b_in[0].reshape(1, -1), pl.BlockSpec((1, tn_in), lambda q, i, j: (0, j)))],
                 name="gmlp_in")[0]
    chunk = gm_ws.shape[-1]
    gated = _gm_gate(zz, gm_ln_g[0].reshape(1, -1), gm_ln_b[0].reshape(1, -1), gm_ws[0].astype(BF16),
                     gm_bs[0].T, chunk=chunk)
    tm_o, tn_o = min(1024, seq), min(512, d)
    per = seq // tm_o
    x3_ = _matmul(gated[None], gm_w_out[0].astype(BF16), tm=tm_o, tn=tn_o, out_dtype=F32, epilogue=_ep_residual,
                  extras=[(x2[None], pl.BlockSpec((None, tm_o, tn_o), lambda q, i, j: (q, i, j))),
                          (gate1, pl.BlockSpec((None, 1, tn_o), lambda q, i, j: (i // per, 0, j)))],
                  name="gmlp_out")[0]

    geff, shift, gate = split(mods_ffn[1], norm_ffn_g[1])
    ys, pos, gw_t = _moe_layer(x3_, geff, shift, rwt, rb, w1b, w3b, w2b, seq=seq, layer=1)
    out = _ffn_combine(x3_, ys, pos, gw_t, gate, (norm_final_g.reshape(1, -1),), seq=seq, final=True)
    return out.reshape(bsz, seq, d)
```

```python
import functools

import jax
import jax.numpy as jnp
from jax import lax
from jax.experimental import pallas as pl
from jax.experimental.pallas import tpu as pltpu

EPS = 1e-6
CHUNK = 8
LANES = 128
SEGMENTS = 8
N_GROUPS_MOE = 4
TOP_K = 2
MOE_TILE = 256
VMEM_LIMIT = 56 * 1024 * 1024

F32 = jnp.float32
BF16 = jnp.bfloat16


def _params(sem):
    return pltpu.CompilerParams(dimension_semantics=sem, vmem_limit_bytes=VMEM_LIMIT)


def _rms_mod(x, geff, shift):
    ms = jnp.mean(x * x, axis=-1, keepdims=True)
    return x * lax.rsqrt(ms + EPS) * geff + shift


def _ada_body(ct_ref, w_ref, b_ref, o_ref):
    ct = ct_ref[...]
    st = ct * jax.nn.sigmoid(ct)
    w = w_ref[...]
    rows = [jnp.sum(st[:, b:b + 1] * w, axis=0, keepdims=True) for b in range(ct.shape[1])]
    o_ref[...] = jnp.concatenate(rows, axis=0) + b_ref[...]


def _adaln(c, w, b):
    depth, d, d3 = w.shape
    bsz = c.shape[0]
    tn = 1024 if d3 % 1024 == 0 else 512
    assert d3 % tn == 0
    return pl.pallas_call(
        _ada_body,
        grid=(depth, d3 // tn),
        in_specs=[
            pl.BlockSpec((d, bsz), lambda l, j: (0, 0)),
            pl.BlockSpec((None, d, tn), lambda l, j: (l, 0, j)),
            pl.BlockSpec((None, 1, tn), lambda l, j: (l, 0, j)),
        ],
        out_specs=pl.BlockSpec((None, bsz, tn), lambda l, j: (l, 0, j)),
        out_shape=jax.ShapeDtypeStruct((depth, bsz, d3), F32),
        compiler_params=_params(("arbitrary", "arbitrary")),
        name="adaln",
    )(c.T, w, b.reshape(depth, 1, d3))


def _side_cast_specs(src, layer, n_steps, step_of):
    _, n_e, r, c = src.shape
    per_step = (n_e * r) // n_steps
    assert per_step * n_steps == n_e * r and per_step % 16 == 0
    eb, rows = max(1, per_step // r), min(per_step, r)
    assert eb * rows == per_step and r % rows == 0 and n_e % eb == 0
    nb = r // rows
    in_spec = pl.BlockSpec((None, eb, rows, c), lambda *g: (layer, step_of(*g) // nb, step_of(*g) % nb, 0))
    out_spec = pl.BlockSpec((eb, rows, c), lambda *g: (step_of(*g) // nb, step_of(*g) % nb, 0))
    return in_spec, out_spec, jax.ShapeDtypeStruct((n_e, r, c), BF16)


def _mm_body(*refs, epilogue, n_extra, n_side):
    a_ref, w_ref = refs[0], refs[1]
    extra = refs[2:2 + n_extra]
    side_in = refs[2 + n_extra:2 + n_extra + n_side]
    o_ref = refs[2 + n_extra + n_side]
    side_out = refs[3 + n_extra + n_side:]
    acc = jnp.dot(a_ref[...], w_ref[...], preferred_element_type=F32)
    o_ref[...] = epilogue(acc, *extra).astype(o_ref.dtype)
    for s_in, s_out in zip(side_in, side_out):
        s_out[...] = s_in[...].astype(s_out.dtype)


def _matmul(a, w, *, tm, tn, out_dtype, epilogue, extras=(), side=(), name):
    p, m, k = a.shape
    n = w.shape[1]
    tm, tn = min(tm, m), min(tn, n)
    ni, nj = m // tm, n // tn
    side_specs = [_side_cast_specs(src, layer, p * ni * nj, lambda q, i, j: (q * ni + i) * nj + j)
                  for src, layer in side]
    in_specs = [
        pl.BlockSpec((None, tm, k), lambda q, i, j: (q, i, 0)),
        pl.BlockSpec((k, tn), lambda q, i, j: (0, j)),
    ] + [s for _, s in extras] + [s[0] for s in side_specs]
    out = pl.pallas_call(
        functools.partial(_mm_body, epilogue=epilogue, n_extra=len(extras), n_side=len(side)),
        grid=(p, ni, nj),
        in_specs=in_specs,
        out_specs=[pl.BlockSpec((None, tm, tn), lambda q, i, j: (q, i, j))] + [s[1] for s in side_specs],
        out_shape=[jax.ShapeDtypeStruct((p, m, n), out_dtype)] + [s[2] for s in side_specs],
        compiler_params=_params(("arbitrary", "arbitrary", "arbitrary")),
        name=name,
    )(a, w, *[x for x, _ in extras], *[src for src, _ in side])
    return out if side else out[0]


def _ep_identity(acc):
    return acc


def _ep_glu(acc, y_ref, b_ref):
    return y_ref[...].astype(F32) * jax.nn.sigmoid(acc + b_ref[...])


def _ep_bias_gelu(acc, b_ref):
    return jax.nn.gelu(acc + b_ref[...])


def _ep_residual(acc, x_ref, g_ref):
    return x_ref[...] + g_ref[...] * acc


def _plane_rows(block, seg, *, cb, kb):
    per = cb // (kb * SEGMENTS)
    return pl.ds((block // per) * cb + seg * (cb // SEGMENTS) + (block % per) * kb, kb)


def _norm_plane_body(x_hbm, g_ref, s_ref, o_ref, buf, sem, *, cb):
    i = pl.program_id(0)
    nsteps = pl.num_programs(0)
    kb = buf.shape[2]
    slot = i % 2

    def fetch(step, sl, t, s):
        return pltpu.make_async_copy(x_hbm.at[_plane_rows(step, s, cb=cb, kb=kb), t], buf.at[sl, t, :, s], sem.at[sl, t])

    def start_all(step, sl):
        for t in range(CHUNK):
            for s in range(SEGMENTS):
                fetch(step, sl, t, s).start()

    @pl.when(i == 0)
    def _():
        start_all(0, 0)

    @pl.when(i + 1 < nsteps)
    def _():
        start_all(i + 1, 1 - slot)

    for t in range(CHUNK):
        for s in range(SEGMENTS):
            fetch(i, slot, t, s).wait()
        xt = buf[slot, t].reshape(kb * SEGMENTS, buf.shape[-1])
        o_ref[t] = _rms_mod(xt, g_ref[...], s_ref[...]).astype(o_ref.dtype)


def _norm_to_planes(x3, geff, shift, *, chunks_per_batch):
    nc, _, d = x3.shape
    kb = 4
    tc = kb * SEGMENTS
    per = chunks_per_batch // tc
    vec = pl.BlockSpec((None, 1, d), lambda i: (i // per, 0, 0))
    return pl.pallas_call(
        functools.partial(_norm_plane_body, cb=chunks_per_batch),
        grid=(nc // tc,),
        in_specs=[pl.BlockSpec(memory_space=pl.ANY), vec, vec],
        out_specs=pl.BlockSpec((CHUNK, tc, d), lambda i: (0, i, 0)),
        out_shape=jax.ShapeDtypeStruct((CHUNK, nc, d), BF16),
        scratch_shapes=[pltpu.VMEM((2, CHUNK, kb, SEGMENTS, d), F32), pltpu.SemaphoreType.DMA((2, CHUNK))],
        compiler_params=_params(("arbitrary",)),
        name="s5_norm_planes",
    )(x3, geff, shift)


def _cmul_add(ar, ai, xr, xi, vr, vi):
    return ar * xr - ai * xi + vr, ar * xi + ai * xr + vi


def _ssm_body(u_ref, wac_ref, wcc_ref, a1_ref, aseg_ref, d_ref, y_ref, xf_ref, xb_ref, wa_ref, wc_ref, *,
              rows_per_dot, gl, hsz, psz):
    cb = u_ref.shape[1]
    ks = cb // SEGMENTS
    half = xf_ref.shape[-1]
    q = half // 2

    @pl.when(pl.program_id(1) == 0)
    def _():
        iota = lambda shape, ax: lax.broadcasted_iota(jnp.int32, shape, ax)
        ca, na = wac_ref.shape[1], wa_ref.shape[1]
        dcol = iota((ca, na), 1)
        spread = (iota((ca, na), 0) == (dcol // (gl * psz)) * psz + dcol % psz).astype(BF16)
        rep = jnp.dot(wac_ref[...], spread, preferred_element_type=F32)
        shp = rep.shape
        keep = (iota(shp, 0) // hsz) % gl == (iota(shp, 1) // psz) % gl
        wa_ref[...] = jnp.where(keep, rep, 0.0).astype(wa_ref.dtype)
        cc, nc_ = wcc_ref.shape[1], wc_ref.shape[1]
        dcol = iota((cc, nc_), 1)
        spread = (iota((cc, nc_), 0) == (dcol // (gl * hsz)) * hsz + dcol % hsz).astype(BF16)
        rep = jnp.dot(wcc_ref[...], spread, preferred_element_type=F32)
        shp = rep.shape
        n_tok_rows = CHUNK * gl * hsz
        r = iota(shp, 0)
        row_g = jnp.where(r < n_tok_rows, (r // hsz) % gl, ((r - n_tok_rows) // psz) % gl)
        keep = row_g == (iota(shp, 1) // hsz) % gl
        wc_ref[...] = jnp.where(keep, rep, 0.0).astype(wc_ref.dtype)

    def u_rows(r0, nr):
        return jnp.concatenate([u_ref[t, pl.ds(r0, nr), :] for t in range(CHUNK)], axis=1)

    nr = min(cb, rows_per_dot)
    for r0 in range(0, cb, nr):
        res = jnp.dot(u_rows(r0, nr), wa_ref[...], preferred_element_type=F32)
        xf_ref[pl.ds(r0, nr), :] = res[:, :half]
        xb_ref[pl.ds(r0, nr), :] = res[:, half:]

    afr, afi, abr, abi = (a1_ref[i:i + 1, :] for i in range(4))
    group = lambda j: pl.ds(pl.multiple_of(j * SEGMENTS, SEGMENTS), SEGMENTS)

    def p1(j, carry):
        cfr, cfi, cbr, cbi = carry
        jb = ks - 1 - j
        vf = xf_ref[group(j), :]
        vb = xb_ref[group(jb), :]
        xf_ref[group(j), :] = jnp.concatenate([cfr, cfi], axis=1)
        xb_ref[group(jb), :] = jnp.concatenate([cbr, cbi], axis=1)
        nfr, nfi = _cmul_add(afr, afi, cfr, cfi, vf[:, :q], vf[:, q:])
        nbr, nbi = _cmul_add(abr, abi, cbr, cbi, vb[:, :q], vb[:, q:])
        return nfr, nfi, nbr, nbi

    z = jnp.zeros((SEGMENTS, q), F32)
    efr, efi, ebr, ebi = lax.fori_loop(0, ks, p1, (z, z, z, z))

    sfr, sfi, sbr, sbi = (aseg_ref[i:i + 1, :] for i in range(4))
    z1 = jnp.zeros((1, q), F32)
    gfr, gfi = [z1], [z1]
    for s in range(SEGMENTS - 1):
        r, i = _cmul_add(sfr, sfi, gfr[-1], gfi[-1], efr[s:s + 1], efi[s:s + 1])
        gfr.append(r)
        gfi.append(i)
    gbr, gbi = [z1], [z1]
    for s in range(SEGMENTS - 1, 0, -1):
        r, i = _cmul_add(sbr, sbi, gbr[0], gbi[0], ebr[s:s + 1], ebi[s:s + 1])
        gbr.insert(0, r)
        gbi.insert(0, i)
    gfr, gfi, gbr, gbi = (jnp.concatenate(g, axis=0) for g in (gfr, gfi, gbr, gbi))

    def p2(j, carry):
        pfr, pfi, pbr, pbi = carry
        jb = ks - 1 - j
        cf = xf_ref[group(j), :]
        cb_ = xb_ref[group(jb), :]
        xf_ref[group(j), :] = jnp.concatenate([cf[:, :q] + pfr * gfr - pfi * gfi, cf[:, q:] + pfr * gfi + pfi * gfr], axis=1)
        xb_ref[group(jb), :] = jnp.concatenate([cb_[:, :q] + pbr * gbr - pbi * gbi, cb_[:, q:] + pbr * gbi + pbi * gbr], axis=1)
        return (pfr * afr - pfi * afi, pfr * afi + pfi * afr, pbr * abr - pbi * abi, pbr * abi + pbi * abr)

    one = jnp.ones((1, q), F32)
    lax.fori_loop(0, ks, p2, (one, jnp.zeros((1, q), F32), one, jnp.zeros((1, q), F32)))

    lw = d_ref.shape[-1]
    for r0 in range(0, cb, nr):
        xf = xf_ref[pl.ds(r0, nr), :]
        xb = xb_ref[pl.ds(r0, nr), :]
        lhs = jnp.concatenate([u_rows(r0, nr), xf.astype(BF16), xb.astype(BF16)], axis=1)
        acc = jnp.dot(lhs, wc_ref[...], preferred_element_type=F32)
        for t in range(CHUNK):
            yt = acc[:, t * lw:(t + 1) * lw] + d_ref[...] * u_ref[t, pl.ds(r0, nr), :].astype(F32)
            y_ref[t, pl.ds(r0, nr), :] = jax.nn.gelu(yt).astype(y_ref.dtype)


def _s5_ssm(u8, wa, wc, a1, aseg, dvec, *, chunks_per_batch, group, state):
    _, nc, w = u8.shape
    nslab = w // LANES
    nb = nc // chunks_per_batch
    cb = chunks_per_batch
    ks = cb // SEGMENTS
    gl = LANES // group
    half = 2 * gl * state
    tok = CHUNK * LANES
    body = functools.partial(_ssm_body, rows_per_dot=512, gl=gl, hsz=group, psz=state)
    return pl.pallas_call(
        body,
        grid=(nslab, nb),
        in_specs=[
            pl.BlockSpec((CHUNK, cb, LANES), lambda s, b: (0, b, s)),
            pl.BlockSpec((None,) + wa.shape[1:], lambda s, b: (s, 0, 0)),
            pl.BlockSpec((None,) + wc.shape[1:], lambda s, b: (s, 0, 0)),
            pl.BlockSpec((None,) + a1.shape[1:], lambda s, b: (s, 0, 0)),
            pl.BlockSpec((None,) + aseg.shape[1:], lambda s, b: (s, 0, 0)),
            pl.BlockSpec((1, LANES), lambda s, b: (0, s)),
        ],
        out_specs=pl.BlockSpec((CHUNK, cb, LANES), lambda s, b: (0, b, s)),
        out_shape=jax.ShapeDtypeStruct(u8.shape, BF16),
        scratch_shapes=[
            pltpu.VMEM((cb, half), F32),
            pltpu.VMEM((cb, half), F32),
            pltpu.VMEM((tok, 2 * half), BF16),
            pltpu.VMEM((tok + 2 * half, tok), BF16),
        ],
        compiler_params=_params(("arbitrary", "arbitrary")),
        name="s5_ssm",
    )(u8, wa, wc, a1, aseg, dvec)


def _s5_weights(lam_re, lam_im, log_step, b_re, b_im, c_re, c_im, *, seg_rows):
    _, g, p = lam_re.shape
    h = b_re.shape[-1]
    gl = LANES // h
    s = g // gl
    t = CHUNK
    lam = lax.complex(lam_re.astype(F32), lam_im.astype(F32))
    zed = lam * jnp.exp(log_step.astype(F32))[..., None]
    lam_bar = jnp.exp(zed)
    bbar = ((lam_bar - 1.0) / lam)[..., None] * lax.complex(b_re.astype(F32), b_im.astype(F32))
    cmat = lax.complex(c_re.astype(F32), c_im.astype(F32))
    pw = jnp.exp(zed[:, None] * jnp.arange(t + 1, dtype=F32)[None, :, None, None])
    pw_down = jnp.exp(zed[:, None] * (t - jnp.arange(t + 1, dtype=F32))[None, :, None, None])

    ma_f = pw_down[0, 1:, :, :, None] * bbar[0][None]
    ma_b = pw[1, :t, :, :, None] * bbar[1][None]
    wa = jnp.stack([ma_f.real, ma_f.imag, ma_b.real, ma_b.imag], axis=0)
    wa = wa.reshape(4, t, s, gl, p, h).transpose(2, 1, 3, 5, 0, 4).reshape(s, t * gl * h, 4 * p)

    kf = jnp.einsum('gyp,dgp,gph->dgyh', cmat[0], pw[0, :t], bbar[0]).real
    kb = jnp.einsum('gyp,dgp,gph->dgyh', cmat[1], pw[1, :t], bbar[1]).real
    st = jnp.arange(t)
    lag = st[None, None, :] - st[None, :, None]
    sel_f = (lag == st[:, None, None]).astype(F32)
    sel_b = (-lag == st[:, None, None]).astype(F32)
    coef = jnp.einsum('dat,dgyh->atgyh', sel_f, kf) + jnp.einsum('dat,dgyh->atgyh', sel_b, kb)
    w_u = coef.reshape(t, t, s, gl, h, h).transpose(2, 0, 3, 5, 1, 4).reshape(s, t * gl * h, t * h)

    cl_f = cmat[0][None] * pw[0, 1:][:, :, None, :]
    cl_b = cmat[1][None] * pw_down[1, :t][:, :, None, :]
    def state_rows(x):
        return x.reshape(t, s, gl, h, p).transpose(1, 2, 4, 0, 3).reshape(s, gl * p, t * h)
    wc = jnp.concatenate([w_u, state_rows(cl_f.real), state_rows(-cl_f.imag),
                          state_rows(cl_b.real), state_rows(-cl_b.imag)], axis=1)

    def lanes(x):
        x = x.reshape(2, s, gl * p)
        return jnp.stack([x[0].real, x[0].imag, x[1].real, x[1].imag], axis=1)
    a1 = lanes(pw[:, t])
    aseg = lanes(jnp.exp(zed * float(t * seg_rows)))
    return wa.astype(BF16), wc.astype(BF16), a1, aseg


def _s5_out_body(a_ref, w_ref, g_ref, side_ref, x_hbm, o_hbm, side_out, xin, xout, isem, osem, *, cb):
    side_out[...] = side_ref[...].astype(side_out.dtype)
    nj = pl.num_programs(1)
    step = pl.program_id(0) * nj + pl.program_id(1)
    total = pl.num_programs(0) * nj
    tm, tn = a_ref.shape[1], w_ref.shape[1]
    kb = tm // SEGMENTS
    slot = step % 2

    def window(st, t, s):
        cols = pl.ds(pl.multiple_of((st % nj) * tn, tn), tn)
        return (_plane_rows(st // nj, s, cb=cb, kb=kb), t, cols)

    def fetch(st, sl, t, s):
        return pltpu.make_async_copy(x_hbm.at[window(st, t, s)], xin.at[sl, t, :, s], isem.at[sl, t])

    def put(st, sl, t, s):
        return pltpu.make_async_copy(xout.at[sl, t, :, s], o_hbm.at[window(st, t, s)], osem.at[sl, t])

    def each(fn):
        for t in range(CHUNK):
            for s in range(SEGMENTS):
                fn(t, s)

    @pl.when(step == 0)
    def _():
        each(lambda t, s: fetch(0, 0, t, s).start())

    @pl.when(step + 1 < total)
    def _():
        each(lambda t, s: fetch(step + 1, 1 - slot, t, s).start())

    a = a_ref[...].reshape(CHUNK * tm, a_ref.shape[2])
    acc = jnp.dot(a, w_ref[...], preferred_element_type=F32)

    @pl.when(step >= 2)
    def _():
        each(lambda t, s: put(step - 2, slot, t, s).wait())

    for t in range(CHUNK):
        for s in range(SEGMENTS):
            fetch(step, slot, t, s).wait()
        xt = xin[slot, t].reshape(tm, tn) + g_ref[...] * acc[t * tm:(t + 1) * tm]
        xout[slot, t] = xt.reshape(kb, SEGMENTS, tn)
    each(lambda t, s: put(step, slot, t, s).start())

    @pl.when(step == total - 1)
    def _():
        each(lambda t, s: put(step, slot, t, s).wait())

        @pl.when(total >= 2)
        def _():
            each(lambda t, s: put(step - 1, 1 - slot, t, s).wait())


def _s5_out(z8, w, x3, gate, side, *, chunks_per_batch):
    _, nc, k = z8.shape
    d = w.shape[1]
    tm = min(128, chunks_per_batch)
    tn = min(512, d)
    per = chunks_per_batch // tm
    kb = tm // SEGMENTS
    nj = d // tn
    side_in, side_out, side_shape = _side_cast_specs(side[0], side[1], (nc // tm) * nj, lambda i, j: i * nj + j)
    return pl.pallas_call(
        functools.partial(_s5_out_body, cb=chunks_per_batch),
        grid=(nc // tm, nj),
        in_specs=[
            pl.BlockSpec((CHUNK, tm, k), lambda i, j: (0, i, 0)),
            pl.BlockSpec((k, tn), lambda i, j: (0, j)),
            pl.BlockSpec((None, 1, tn), lambda i, j: (i // per, 0, j)),
            side_in,
            pl.BlockSpec(memory_space=pl.ANY),
        ],
        out_specs=[pl.BlockSpec(memory_space=pl.ANY), side_out],
        out_shape=[jax.ShapeDtypeStruct(x3.shape, F32), side_shape],
        scratch_shapes=[
            pltpu.VMEM((2, CHUNK, kb, SEGMENTS, tn), F32),
            pltpu.VMEM((2, CHUNK, kb, SEGMENTS, tn), F32),
            pltpu.SemaphoreType.DMA((2, CHUNK)),
            pltpu.SemaphoreType.DMA((2, CHUNK)),
        ],
        compiler_params=_params(("arbitrary", "arbitrary")),
        name="s5_out",
    )(z8, w, gate, side[0], x3)


def _gm_gate_body(z_ref, lng_ref, lnb_ref, ws_ref, bs_ref, o_ref, *, chunk):
    tm, w2 = z_ref.shape
    w = w2 // 2
    heads = ws_ref.shape[0]
    hd = w // heads
    v = z_ref[:, w:].astype(F32)
    mu = jnp.mean(v, axis=-1, keepdims=True)
    vc = v - mu
    var = jnp.mean(vc * vc, axis=-1, keepdims=True)
    vn = (vc * lax.rsqrt(var + EPS) * lng_ref[...] + lnb_ref[...]).astype(BF16)
    for c in range(tm // chunk):
        rows = slice(c * chunk, (c + 1) * chunk)
        for hh in range(heads):
            cols = slice(hh * hd, (hh + 1) * hd)
            sv = jnp.dot(ws_ref[hh], vn[rows, cols], preferred_element_type=F32) + bs_ref[:, hh:hh + 1]
            o_ref[rows, cols] = (z_ref[rows, cols].astype(F32) * sv).astype(o_ref.dtype)


def _gm_gate(z, ln_g, ln_b, ws, bs, *, chunk):
    n, w2 = z.shape
    w = w2 // 2
    tm = 2 * chunk
    full = lambda a: pl.BlockSpec(a.shape, lambda i: (0,) * a.ndim)
    return pl.pallas_call(
        functools.partial(_gm_gate_body, chunk=chunk),
        grid=(n // tm,),
        in_specs=[pl.BlockSpec((tm, w2), lambda i: (i, 0)), full(ln_g), full(ln_b), full(ws), full(bs)],
        out_specs=pl.BlockSpec((tm, w), lambda i: (i, 0)),
        out_shape=jax.ShapeDtypeStruct((n, w), BF16),
        compiler_params=_params(("arbitrary",)),
        name="gmlp_gate",
    )(z, ln_g, ln_b, ws, bs)


def _pack_bf16_pairs(x):
    half = x.shape[1] // 2
    bits = lambda v: lax.bitcast_convert_type(v.astype(F32), jnp.int32)
    return bits(x[:, half:]) | lax.shift_right_logical(bits(x[:, :half]), 16)


def _unpack_bf16_pairs(w):
    lo = lax.bitcast_convert_type(lax.shift_left(w, 16), F32)
    hi = lax.bitcast_convert_type(w & jnp.int32(-65536), F32)
    return jnp.concatenate([lo.astype(BF16), hi.astype(BF16)], axis=1)


def _top2_sum(a, b, c, d):
    m1, n1 = jnp.maximum(a, b), jnp.minimum(a, b)
    m2, n2 = jnp.maximum(c, d), jnp.minimum(c, d)
    return jnp.maximum(m1, m2) + jnp.maximum(jnp.minimum(m1, m2), jnp.maximum(n1, n2))


def _first_argmax(vals):
    best, idx = vals[0], jnp.zeros(vals[0].shape, jnp.int32)
    for j in range(1, len(vals)):
        upd = vals[j] > best
        idx = jnp.where(upd, j, idx)
        best = jnp.where(upd, vals[j], best)
    return idx, best


def _ffn_prep_body(x_ref, g_ref, s_ref, rwt_ref, rb_ref, hn_ref, e_ref, gw_ref, rank_ref, cnt_ref, carry_ref):
    i = pl.program_id(0)
    n_e = rwt_ref.shape[1]
    epg = n_e // N_GROUPS_MOE
    tm = x_ref.shape[0]

    @pl.when(i == 0)
    def _():
        carry_ref[...] = jnp.zeros_like(carry_ref)

    hn = _rms_mod(x_ref[...], g_ref[...], s_ref[...])
    hn_hi = hn.astype(BF16)
    hn_lo = (hn - hn_hi.astype(F32)).astype(BF16)
    hn_ref[...] = _pack_bf16_pairs(hn_hi)
    nt_dot = lambda a, b: lax.dot_general(a, b, (((1,), (1,)), ((), ())), preferred_element_type=F32)
    logits = nt_dot(rwt_ref[0], hn_hi) + (nt_dot(rwt_ref[1], hn_hi) + nt_dot(rwt_ref[0], hn_lo))
    score = jax.nn.sigmoid(logits)
    sel = score + rb_ref[...]

    row = lambda a, r: a[r:r + 1, :]
    gscore = [_top2_sum(*[row(sel, g * epg + j) for j in range(epg)]) for g in range(N_GROUPS_MOE)]
    bg, _ = _first_argmax(gscore)
    pick = lambda a, j: functools.reduce(
        lambda acc, g: jnp.where(bg == g, row(a, g * epg + j), acc), range(1, N_GROUPS_MOE), row(a, j))
    sel_in = [pick(sel, j) for j in range(epg)]
    sc_in = [pick(score, j) for j in range(epg)]
    i1, _ = _first_argmax(sel_in)
    neg = jnp.full_like(sel_in[0], -jnp.inf)
    i2, _ = _first_argmax([jnp.where(i1 == j, neg, sel_in[j]) for j in range(epg)])
    take = lambda vals, idx: functools.reduce(lambda acc, j: jnp.where(idx == j, vals[j], acc), range(1, epg), vals[0])
    w1, w2 = take(sc_in, i1), take(sc_in, i2)
    den = w1 + w2
    e1, e2 = bg * epg + i1, bg * epg + i2
    e_ref[...] = jnp.concatenate([e1, e2], axis=0)
    gw_ref[...] = jnp.concatenate([w1 / den, w2 / den], axis=0)

    eid = lax.broadcasted_iota(jnp.int32, (n_e, tm), 0)
    oh1 = (eid == e1).astype(F32)
    oh2 = (eid == e2).astype(F32)
    oh = oh1 + oh2
    tri = (lax.broadcasted_iota(jnp.int32, (tm, tm), 0) <= lax.broadcasted_iota(jnp.int32, (tm, tm), 1)).astype(BF16)
    incl = jnp.dot(oh.astype(BF16), tri, preferred_element_type=F32)
    before = carry_ref[:, 0:1] + incl - oh
    r1 = jnp.sum(oh1 * before, axis=0, keepdims=True)
    r2 = jnp.sum(oh2 * before, axis=0, keepdims=True)
    rank_ref[...] = jnp.concatenate([r1, r2], axis=0).astype(jnp.int32)
    carry_ref[...] = carry_ref[...] + incl[:, tm - 1:tm]
    cnt_ref[...] = carry_ref[...].astype(jnp.int32)


def _ffn_prep(x, geff, shift, rwt, rb, *, seq):
    n, d = x.shape
    n_e = rwt.shape[1]
    tm = min(256, seq)
    per = seq // tm
    vec = pl.BlockSpec((None, 1, d), lambda i: (i // per, 0, 0))
    row2 = pl.BlockSpec((TOP_K, tm), lambda i: (0, i))
    hn, e, gw, rank, cnt = pl.pallas_call(
        _ffn_prep_body,
        grid=(n // tm,),
        in_specs=[pl.BlockSpec((tm, d), lambda i: (i, 0)), vec, vec,
                  pl.BlockSpec((2, n_e, d), lambda i: (0, 0, 0)), pl.BlockSpec((n_e, 1), lambda i: (0, 0))],
        out_specs=[pl.BlockSpec((tm, d // 2), lambda i: (i, 0)), row2, row2, row2,
                   pl.BlockSpec((n_e, LANES), lambda i: (0, 0))],
        out_shape=[jax.ShapeDtypeStruct((n, d // 2), jnp.int32), jax.ShapeDtypeStruct((TOP_K, n), jnp.int32),
                   jax.ShapeDtypeStruct((TOP_K, n), F32), jax.ShapeDtypeStruct((TOP_K, n), jnp.int32),
                   jax.ShapeDtypeStruct((n_e, LANES), jnp.int32)],
        scratch_shapes=[pltpu.VMEM((n_e, LANES), F32)],
        compiler_params=_params(("arbitrary",)),
        name="ffn_prep",
    )(x, geff, shift, rwt, rb)
    return hn, e, gw, rank, cnt[:, 0]


def _for_rows(lo, hi, fn, unroll=1):
    def body(r, c):
        fn(r)
        return c
    lax.fori_loop(lo, hi, body, 0, unroll=unroll)


def _dispatch_body(pos_ref, fill_ref, hn_ref, xs_hbm, zero_ref, sem, zsem, *, n_tokens):
    i = pl.program_id(0)
    tm = hn_ref.shape[0]

    def row_copy(r, k):
        return pltpu.make_async_copy(hn_ref.at[pl.ds(r, 1)], xs_hbm.at[pl.ds(pos_ref[k * n_tokens + i * tm + r], 1)], sem)

    def start(r):
        for k in range(TOP_K):
            row_copy(r, k).start()

    _for_rows(0, tm, start, unroll=8)

    @pl.when(i == 0)
    def _():
        zero_ref[...] = jnp.zeros_like(zero_ref)
        n_fill = fill_ref.shape[0] // 2

        def zero_copy(row):
            return pltpu.make_async_copy(zero_ref, xs_hbm.at[pl.ds(row, 1)], zsem)

        for e in range(n_fill):
            _for_rows(fill_ref[2 * e], fill_ref[2 * e + 1], lambda row: zero_copy(row).start())
        for e in range(n_fill):
            _for_rows(fill_ref[2 * e], fill_ref[2 * e + 1], lambda row: zero_copy(row).wait())

    for k in range(TOP_K):
        pltpu.make_async_copy(hn_ref, xs_hbm.at[pl.ds(0, tm)], sem).wait()


def _moe_dispatch(hn, pos_flat, fill, *, rows):
    n, dh = hn.shape
    tm = min(512, n)
    grid_spec = pltpu.PrefetchScalarGridSpec(
        num_scalar_prefetch=2,
        grid=(n // tm,),
        in_specs=[pl.BlockSpec((tm, dh), lambda i, p, f: (i, 0))],
        out_specs=pl.BlockSpec(memory_space=pl.ANY),
        scratch_shapes=[pltpu.VMEM((1, dh), jnp.int32), pltpu.SemaphoreType.DMA(()), pltpu.SemaphoreType.DMA(())],
    )
    return pl.pallas_call(
        functools.partial(_dispatch_body, n_tokens=n),
        grid_spec=grid_spec,
        out_shape=jax.ShapeDtypeStruct((rows, dh), jnp.int32),
        compiler_params=_params(("arbitrary",)),
        name="moe_dispatch",
    )(pos_flat, fill, hn)


def _moe_body(te_ref, nrows_ref, x_ref, w1_ref, w3_ref, w2_ref, o_ref):
    i = pl.program_id(0)
    used = nrows_ref[i] > 0

    @pl.when(used)
    def _():
        xb = _unpack_bf16_pairs(x_ref[...])
        h1 = jnp.dot(xb, w1_ref[...], preferred_element_type=F32)
        h3 = jnp.dot(xb, w3_ref[...], preferred_element_type=F32)
        hh = (h1 * jax.nn.sigmoid(h1) * h3).astype(BF16)
        o_ref[...] = jnp.dot(hh, w2_ref[...], preferred_element_type=F32)

    @pl.when(jnp.logical_not(used))
    def _():
        o_ref[...] = jnp.zeros_like(o_ref)


def _moe_experts(xs, w1, w3, w2, tile_expert, tile_rows):
    rows, dh = xs.shape
    d = 2 * dh
    fdim = w1.shape[-1]
    tm = MOE_TILE
    resident = pl.Buffered(1)
    grid_spec = pltpu.PrefetchScalarGridSpec(
        num_scalar_prefetch=2,
        grid=(rows // tm,),
        in_specs=[
            pl.BlockSpec((tm, dh), lambda i, te, nr: (i, 0)),
            pl.BlockSpec((None, d, fdim), lambda i, te, nr: (te[i], 0, 0), pipeline_mode=resident),
            pl.BlockSpec((None, d, fdim), lambda i, te, nr: (te[i], 0, 0), pipeline_mode=resident),
            pl.BlockSpec((None, fdim, d), lambda i, te, nr: (te[i], 0, 0), pipeline_mode=resident),
        ],
        out_specs=pl.BlockSpec((tm, d), lambda i, te, nr: (i, 0)),
    )
    return pl.pallas_call(
        _moe_body,
        grid_spec=grid_spec,
        out_shape=jax.ShapeDtypeStruct((rows, d), F32),
        compiler_params=_params(("arbitrary",)),
        name="moe_experts",
    )(tile_expert, tile_rows, xs, w1, w3, w2)


def _moe_plan(e, rank, cnt, n_tiles_max):
    tm = MOE_TILE
    n_e = cnt.shape[0]
    tiles_per = (cnt + tm - 1) // tm
    ends = jnp.cumsum(tiles_per)
    off = (ends - tiles_per) * tm
    onehot = e[..., None] == jnp.arange(n_e, dtype=jnp.int32)
    pos = jnp.sum(jnp.where(onehot, off, 0), axis=-1) + rank
    tile = jnp.arange(n_tiles_max, dtype=jnp.int32)
    tile_expert = jnp.minimum(jnp.sum(tile[:, None] >= ends[None, :], axis=1), n_e - 1).astype(jnp.int32)
    mine = tile_expert[:, None] == jnp.arange(n_e, dtype=jnp.int32)
    tile_rows = jnp.clip(jnp.sum(jnp.where(mine, off + cnt, 0), axis=1) - tile * tm, 0, tm)
    tile_rows = jnp.where(tile < ends[-1], tile_rows, 0).astype(jnp.int32)
    fill_end = jnp.where(jnp.arange(n_e) == n_e - 1, n_tiles_max * tm, ends * tm)
    fill = jnp.stack([off + cnt, fill_end], axis=1).reshape(-1).astype(jnp.int32)
    return pos.astype(jnp.int32), tile_expert, tile_rows, fill


COMBINE_ROW_CHUNKS = 8


def _combine_rows(pos_ref, x_ref, y_hbm, gw_ref, gate_ref, ybuf, sem, finish, *, n_tokens):
    i = pl.program_id(0)
    nsteps = pl.num_programs(0)
    tm = x_ref.shape[0]
    slot = i % 2
    nxt = (i + 1) % nsteps

    def row_copy(step, sl, r, k):
        row = pos_ref[k * n_tokens + step * tm + r]
        return pltpu.make_async_copy(y_hbm.at[pl.ds(row, 1)], ybuf.at[sl, k, pl.ds(r, 1)], sem.at[sl])

    def wait_tile(sl):
        for k in range(TOP_K):
            pltpu.make_async_copy(y_hbm.at[pl.ds(0, tm)], ybuf.at[sl, k], sem.at[sl]).wait()

    @pl.when(i == 0)
    def _():
        def one(r):
            for k in range(TOP_K):
                row_copy(0, 0, r, k).start()
        _for_rows(0, tm, one)

    wait_tile(slot)
    rc = tm // COMBINE_ROW_CHUNKS
    for c in range(COMBINE_ROW_CHUNKS):
        for r in range(c * rc, (c + 1) * rc):
            for k in range(TOP_K):
                row_copy(nxt, 1 - slot, r, k).start()
        rows = pl.ds(c * rc, rc)
        gw = gw_ref[rows, :]
        moe = gw[:, 0:1] * ybuf[slot, 0, rows, :] + gw[:, 1:2] * ybuf[slot, 1, rows, :]
        finish(rows, x_ref[rows, :] + gate_ref[...] * moe)

    @pl.when(i == nsteps - 1)
    def _():
        wait_tile(1 - slot)


def _combine_norm_body(pos_ref, x_ref, y_hbm, gw_ref, gate_ref, g_ref, s_ref, xo_ref, hn_ref, ybuf, sem, *, n_tokens):
    def finish(rows, xn):
        xo_ref[rows, :] = xn
        hn_ref[rows, :] = _rms_mod(xn, g_ref[...], s_ref[...]).astype(hn_ref.dtype)
    _combine_rows(pos_ref, x_ref, y_hbm, gw_ref, gate_ref, ybuf, sem, finish, n_tokens=n_tokens)


def _combine_final_body(pos_ref, x_ref, y_hbm, gw_ref, gate_ref, g_ref, o_ref, ybuf, sem, *, n_tokens):
    def finish(rows, xn):
        ms = jnp.mean(xn * xn, axis=-1, keepdims=True)
        o_ref[rows, :] = xn * lax.rsqrt(ms + EPS) * g_ref[...]
    _combine_rows(pos_ref, x_ref, y_hbm, gw_ref, gate_ref, ybuf, sem, finish, n_tokens=n_tokens)


def _ffn_combine(x, ys, pos_flat, gw_t, gate, norm_args, *, seq, final):
    n, d = x.shape
    tm = min(256, seq)
    per = seq // tm
    tile = pl.BlockSpec((tm, d), lambda i, p: (i, 0))
    vec = pl.BlockSpec((None, 1, d), lambda i, p: (i // per, 0, 0))
    in_specs = [tile, pl.BlockSpec(memory_space=pl.ANY), pl.BlockSpec((tm, TOP_K), lambda i, p: (i, 0)), vec]
    scratch = [pltpu.VMEM((2, TOP_K, tm, d), F32), pltpu.SemaphoreType.DMA((2,))]
    if final:
        body, name = _combine_final_body, "ffn_combine_final"
        in_specs.append(pl.BlockSpec((1, d), lambda i, p: (0, 0)))
        out_specs, out_shape = tile, jax.ShapeDtypeStruct((n, d), F32)
    else:
        body, name = _combine_norm_body, "ffn_combine_norm"
        in_specs += [vec, vec]
        out_specs = [tile, tile]
        out_shape = [jax.ShapeDtypeStruct((n, d), F32), jax.ShapeDtypeStruct((n, d), BF16)]
    grid_spec = pltpu.PrefetchScalarGridSpec(num_scalar_prefetch=1, grid=(n // tm,), in_specs=in_specs,
                                             out_specs=out_specs, scratch_shapes=scratch)
    return pl.pallas_call(
        functools.partial(body, n_tokens=n), grid_spec=grid_spec, out_shape=out_shape,
        compiler_params=_params(("arbitrary",)), name=name,
    )(pos_flat, x, ys, gw_t, gate, *norm_args)


def _moe_layer(x, geff, shift, rwt, rb, w1, w3, w2, *, seq):
    n, _ = x.shape
    n_e = rwt.shape[1]
    hn, e, gw, rank, cnt = _ffn_prep(x, geff, shift, rwt, rb, seq=seq)
    n_tiles_max = (TOP_K * n) // MOE_TILE + n_e
    pos, tile_expert, tile_rows, fill = _moe_plan(e, rank, cnt, n_tiles_max)
    pos_flat = pos.reshape(-1)
    xs = _moe_dispatch(hn, pos_flat, fill, rows=n_tiles_max * MOE_TILE)
    ys = _moe_experts(xs, w1, w3, w2, tile_expert, tile_rows)
    return ys, pos_flat, gw.T


def kernel(x, c, norm_mix_g, norm_ffn_g, norm_final_g, ada_mix_w, ada_mix_b, ada_ffn_w, ada_ffn_b, s5_w_in, s5_lam_re, s5_lam_im, s5_log_step, s5_b_re, s5_b_im, s5_c_re, s5_c_im, s5_d, s5_w_glu, s5_b_glu, s5_w_out, gm_w_in, gm_b_in, gm_ln_g, gm_ln_b, gm_ws, gm_bs, gm_w_out, router_w, router_bias, moe_w1, moe_w3, moe_w2):
    bsz, seq, d = x.shape
    n = bsz * seq
    depth = norm_mix_g.shape[0]
    assert depth == 2 and seq % (CHUNK * SEGMENTS) == 0
    cb = seq // CHUNK
    nc = n // CHUNK

    mods_mix = _adaln(c, ada_mix_w, ada_mix_b)
    mods_ffn = _adaln(c, ada_ffn_w, ada_ffn_b)

    def split(m, g):
        shift, scale, gate = jnp.split(m, 3, axis=-1)
        return (g[None] * (1.0 + scale))[:, None], shift[:, None], gate[:, None]

    rw_hi = router_w.T.astype(BF16)
    rwt = jnp.stack([rw_hi, (router_w.T - rw_hi.astype(F32)).astype(BF16)])
    rb = router_bias.reshape(-1, 1)
    xf = x.reshape(n, d)

    geff, shift, gate = split(mods_mix[0], norm_mix_g[0])
    x3 = xf.reshape(nc, CHUNK, d)
    hn8 = _norm_to_planes(x3, geff, shift, chunks_per_batch=cb)
    u8, w1_l0 = _matmul(hn8, s5_w_in[0].astype(BF16), tm=1024, tn=1024, out_dtype=BF16, epilogue=_ep_identity,
                        side=[(moe_w1, 0)], name="s5_in")
    wa, wc, a1, aseg = _s5_weights(s5_lam_re[0], s5_lam_im[0], s5_log_step[0], s5_b_re[0], s5_b_im[0],
                                   s5_c_re[0], s5_c_im[0], seg_rows=cb // SEGMENTS)
    y8 = _s5_ssm(u8, wa, wc, a1, aseg, s5_d[0].reshape(1, -1), chunks_per_batch=cb,
                 group=s5_b_re.shape[-1], state=s5_b_re.shape[-2])
    w = y8.shape[-1]
    tmg, tng = min(1024, nc), min(512, w)
    z8, w3_l0 = _matmul(y8, s5_w_glu[0].astype(BF16), tm=tmg, tn=tng, out_dtype=BF16, epilogue=_ep_glu,
                        extras=[(y8, pl.BlockSpec((None, tmg, tng), lambda q, i, j: (q, i, j))),
                                (s5_b_glu[0].reshape(1, -1), pl.BlockSpec((1, tng), lambda q, i, j: (0, j)))],
                        side=[(moe_w3, 0)], name="s5_glu")
    x1, w2_l0 = _s5_out(z8, s5_w_out[0].astype(BF16), x3, gate, (moe_w2, 0), chunks_per_batch=cb)
    x1 = x1.reshape(n, d)

    geff, shift, gate = split(mods_ffn[0], norm_ffn_g[0])
    ys, pos, gw_t = _moe_layer(x1, geff, shift, rwt, rb, w1_l0, w3_l0, w2_l0, seq=seq)
    geff1, shift1, gate1 = split(mods_mix[1], norm_mix_g[1])
    x2, hn = _ffn_combine(x1, ys, pos, gw_t, gate, (geff1, shift1), seq=seq, final=False)

    w2n = gm_w_in.shape[-1]
    tm_in, tn_in = min(1024, n), min(1024, w2n)
    zz, w1_l1 = _matmul(
        hn[None], gm_w_in[0].astype(BF16), tm=tm_in, tn=tn_in, out_dtype=BF16, epilogue=_ep_bias_gelu,
        extras=[(gm_b_in[0].reshape(1, -1), pl.BlockSpec((1, tn_in), lambda q, i, j: (0, j)))],
        side=[(moe_w1, 1)], name="gmlp_in")
    zz = zz[0]
    chunk = gm_ws.shape[-1]
    gated = _gm_gate(zz, gm_ln_g[0].reshape(1, -1), gm_ln_b[0].reshape(1, -1), gm_ws[0].astype(BF16),
                     gm_bs[0].T, chunk=chunk)
    tm_o, tn_o = min(1024, seq), min(512, d)
    per = seq // tm_o
    x3_, w3_l1, w2_l1 = _matmul(gated[None], gm_w_out[0].astype(BF16), tm=tm_o, tn=tn_o, out_dtype=F32,
                                epilogue=_ep_residual,
                                extras=[(x2[None], pl.BlockSpec((None, tm_o, tn_o), lambda q, i, j: (q, i, j))),
                                        (gate1, pl.BlockSpec((None, 1, tn_o), lambda q, i, j: (i // per, 0, j)))],
                                side=[(moe_w3, 1), (moe_w2, 1)], name="gmlp_out")
    x3_ = x3_[0]

    geff, shift, gate = split(mods_ffn[1], norm_ffn_g[1])
    ys, pos, gw_t = _moe_layer(x3_, geff, shift, rwt, rb, w1_l1, w3_l1, w2_l1, seq=seq)
    out = _ffn_combine(x3_, ys, pos, gw_t, gate, (norm_final_g.reshape(1, -1),), seq=seq, final=True)
    return out.reshape(bsz, seq, d)
```

```python
import functools

import jax
import jax.numpy as jnp
from jax import lax
from jax.experimental import pallas as pl
from jax.experimental.pallas import tpu as pltpu

EPS = 1e-6
CHUNK = 8
LANES = 128
SEGMENTS = 8
N_GROUPS_MOE = 4
TOP_K = 2
MOE_TILE = 256
VMEM_LIMIT = 56 * 1024 * 1024

F32 = jnp.float32
BF16 = jnp.bfloat16


def _params(sem):
    return pltpu.CompilerParams(dimension_semantics=sem, vmem_limit_bytes=VMEM_LIMIT)


def _rms_mod(x, geff, shift):
    ms = jnp.mean(x * x, axis=-1, keepdims=True)
    return x * lax.rsqrt(ms + EPS) * geff + shift


def _ada_body(ct_ref, w_ref, b_ref, o_ref):
    ct = ct_ref[...]
    st = ct * jax.nn.sigmoid(ct)
    w = w_ref[...]
    rows = [jnp.sum(st[:, b:b + 1] * w, axis=0, keepdims=True) for b in range(ct.shape[1])]
    o_ref[...] = jnp.concatenate(rows, axis=0) + b_ref[...]


def _adaln(c, w, b):
    depth, d, d3 = w.shape
    bsz = c.shape[0]
    tn = 1024 if d3 % 1024 == 0 else 512
    assert d3 % tn == 0
    return pl.pallas_call(
        _ada_body,
        grid=(depth, d3 // tn),
        in_specs=[
            pl.BlockSpec((d, bsz), lambda l, j: (0, 0)),
            pl.BlockSpec((None, d, tn), lambda l, j: (l, 0, j)),
            pl.BlockSpec((None, 1, tn), lambda l, j: (l, 0, j)),
        ],
        out_specs=pl.BlockSpec((None, bsz, tn), lambda l, j: (l, 0, j)),
        out_shape=jax.ShapeDtypeStruct((depth, bsz, d3), F32),
        compiler_params=_params(("arbitrary", "arbitrary")),
        name="adaln",
    )(c.T, w, b.reshape(depth, 1, d3))


def _side_cast_specs(src, layer, n_steps, step_of):
    _, n_e, r, c = src.shape
    per_step = (n_e * r) // n_steps
    assert per_step * n_steps == n_e * r and per_step % 16 == 0
    eb, rows = max(1, per_step // r), min(per_step, r)
    assert eb * rows == per_step and r % rows == 0 and n_e % eb == 0
    nb = r // rows
    in_spec = pl.BlockSpec((None, eb, rows, c), lambda *g: (layer, step_of(*g) // nb, step_of(*g) % nb, 0))
    out_spec = pl.BlockSpec((eb, rows, c), lambda *g: (step_of(*g) // nb, step_of(*g) % nb, 0))
    return in_spec, out_spec, jax.ShapeDtypeStruct((n_e, r, c), BF16)


def _mm_body(*refs, epilogue, n_extra, n_side):
    a_ref, w_ref = refs[0], refs[1]
    extra = refs[2:2 + n_extra]
    side_in = refs[2 + n_extra:2 + n_extra + n_side]
    o_ref = refs[2 + n_extra + n_side]
    side_out = refs[3 + n_extra + n_side:]
    acc = jnp.dot(a_ref[...], w_ref[...], preferred_element_type=F32)
    o_ref[...] = epilogue(acc, *extra).astype(o_ref.dtype)
    for s_in, s_out in zip(side_in, side_out):
        s_out[...] = s_in[...].astype(s_out.dtype)


def _matmul(a, w, *, tm, tn, out_dtype, epilogue, extras=(), side=(), name):
    p, m, k = a.shape
    n = w.shape[1]
    tm, tn = min(tm, m), min(tn, n)
    ni, nj = m // tm, n // tn
    side_specs = [_side_cast_specs(src, layer, p * ni * nj, lambda q, i, j: (q * ni + i) * nj + j)
                  for src, layer in side]
    in_specs = [
        pl.BlockSpec((None, tm, k), lambda q, i, j: (q, i, 0)),
        pl.BlockSpec((k, tn), lambda q, i, j: (0, j)),
    ] + [s for _, s in extras] + [s[0] for s in side_specs]
    out = pl.pallas_call(
        functools.partial(_mm_body, epilogue=epilogue, n_extra=len(extras), n_side=len(side)),
        grid=(p, ni, nj),
        in_specs=in_specs,
        out_specs=[pl.BlockSpec((None, tm, tn), lambda q, i, j: (q, i, j))] + [s[1] for s in side_specs],
        out_shape=[jax.ShapeDtypeStruct((p, m, n), out_dtype)] + [s[2] for s in side_specs],
        compiler_params=_params(("arbitrary", "arbitrary", "arbitrary")),
        name=name,
    )(a, w, *[x for x, _ in extras], *[src for src, _ in side])
    return out if side else out[0]


def _ep_identity(acc):
    return acc


def _ep_glu(acc, y_ref, b_ref):
    return y_ref[...].astype(F32) * jax.nn.sigmoid(acc + b_ref[...])


def _ep_bias_gelu(acc, b_ref):
    return jax.nn.gelu(acc + b_ref[...])


def _ep_residual(acc, x_ref, g_ref):
    return x_ref[...] + g_ref[...] * acc


def _plane_rows(block, seg, *, cb, kb):
    per = cb // (kb * SEGMENTS)
    return pl.ds((block // per) * cb + seg * (cb // SEGMENTS) + (block % per) * kb, kb)


def _norm_plane_body(x_hbm, g_ref, s_ref, o_ref, buf, sem, *, cb):
    i = pl.program_id(0)
    nsteps = pl.num_programs(0)
    kb = buf.shape[2]
    slot = i % 2

    def fetch(step, sl, t, s):
        return pltpu.make_async_copy(x_hbm.at[_plane_rows(step, s, cb=cb, kb=kb), t], buf.at[sl, t, :, s], sem.at[sl, t])

    def start_all(step, sl):
        for t in range(CHUNK):
            for s in range(SEGMENTS):
                fetch(step, sl, t, s).start()

    @pl.when(i == 0)
    def _():
        start_all(0, 0)

    @pl.when(i + 1 < nsteps)
    def _():
        start_all(i + 1, 1 - slot)

    for t in range(CHUNK):
        for s in range(SEGMENTS):
            fetch(i, slot, t, s).wait()
        xt = buf[slot, t].reshape(kb * SEGMENTS, buf.shape[-1])
        o_ref[t] = _rms_mod(xt, g_ref[...], s_ref[...]).astype(o_ref.dtype)


def _norm_to_planes(x3, geff, shift, *, chunks_per_batch):
    nc, _, d = x3.shape
    kb = 4
    tc = kb * SEGMENTS
    per = chunks_per_batch // tc
    vec = pl.BlockSpec((None, 1, d), lambda i: (i // per, 0, 0))
    return pl.pallas_call(
        functools.partial(_norm_plane_body, cb=chunks_per_batch),
        grid=(nc // tc,),
        in_specs=[pl.BlockSpec(memory_space=pl.ANY), vec, vec],
        out_specs=pl.BlockSpec((CHUNK, tc, d), lambda i: (0, i, 0)),
        out_shape=jax.ShapeDtypeStruct((CHUNK, nc, d), BF16),
        scratch_shapes=[pltpu.VMEM((2, CHUNK, kb, SEGMENTS, d), F32), pltpu.SemaphoreType.DMA((2, CHUNK))],
        compiler_params=_params(("arbitrary",)),
        name="s5_norm_planes",
    )(x3, geff, shift)


def _cmul_add(ar, ai, xr, xi, vr, vi):
    return ar * xr - ai * xi + vr, ar * xi + ai * xr + vi


def _ssm_body(u_ref, wac_ref, wcc_ref, a1_ref, aseg_ref, d_ref, y_ref, xf_ref, xb_ref, wa_ref, wc_ref, *,
              rows_per_dot, gl, hsz, psz):
    nb, cb = xf_ref.shape[0], xf_ref.shape[1]
    ks = cb // SEGMENTS
    half = xf_ref.shape[-1]
    q = half // 2

    if True:
        iota = lambda shape, ax: lax.broadcasted_iota(jnp.int32, shape, ax)
        ca, na = wac_ref.shape[1], wa_ref.shape[1]
        dcol = iota((ca, na), 1)
        spread = (iota((ca, na), 0) == (dcol // (gl * psz)) * psz + dcol % psz).astype(BF16)
        rep = jnp.dot(wac_ref[...], spread, preferred_element_type=F32)
        shp = rep.shape
        keep = (iota(shp, 0) // hsz) % gl == (iota(shp, 1) // psz) % gl
        wa_ref[...] = jnp.where(keep, rep, 0.0).astype(wa_ref.dtype)
        cc, nc_ = wcc_ref.shape[1], wc_ref.shape[1]
        dcol = iota((cc, nc_), 1)
        spread = (iota((cc, nc_), 0) == (dcol // (gl * hsz)) * hsz + dcol % hsz).astype(BF16)
        rep = jnp.dot(wcc_ref[...], spread, preferred_element_type=F32)
        shp = rep.shape
        n_tok_rows = CHUNK * gl * hsz
        r = iota(shp, 0)
        row_g = jnp.where(r < n_tok_rows, (r // hsz) % gl, ((r - n_tok_rows) // psz) % gl)
        keep = row_g == (iota(shp, 1) // hsz) % gl
        wc_ref[...] = jnp.where(keep, rep, 0.0).astype(wc_ref.dtype)

    def u_rows(r0, nr):
        return jnp.concatenate([u_ref[t, pl.ds(r0, nr), :] for t in range(CHUNK)], axis=1)

    nr = min(cb, rows_per_dot)
    afr, afi, abr, abi = (a1_ref[i:i + 1, :] for i in range(4))
    sfr, sfi, sbr, sbi = (aseg_ref[i:i + 1, :] for i in range(4))
    group = lambda j: slice(j * SEGMENTS, (j + 1) * SEGMENTS)
    lw = d_ref.shape[-1]

    def phase_a(b):
        for r0 in range(0, cb, nr):
            res = jnp.dot(u_rows(b * cb + r0, nr), wa_ref[...], preferred_element_type=F32)
            xf_ref[b, pl.ds(r0, nr), :] = res[:, :half]
            xb_ref[b, pl.ds(r0, nr), :] = res[:, half:]

    def phase_b(b):
        z = jnp.zeros((SEGMENTS, q), F32)
        cfr, cfi, cbr, cbi = z, z, z, z
        for j in range(ks):
            jb = ks - 1 - j
            vf = xf_ref[b, group(j), :]
            vb = xb_ref[b, group(jb), :]
            xf_ref[b, group(j), :] = jnp.concatenate([cfr, cfi], axis=1)
            xb_ref[b, group(jb), :] = jnp.concatenate([cbr, cbi], axis=1)
            cfr, cfi = _cmul_add(afr, afi, cfr, cfi, vf[:, :q], vf[:, q:])
            cbr, cbi = _cmul_add(abr, abi, cbr, cbi, vb[:, :q], vb[:, q:])
        efr, efi, ebr, ebi = cfr, cfi, cbr, cbi

        z1 = jnp.zeros((1, q), F32)
        gfr, gfi = [z1], [z1]
        for s in range(SEGMENTS - 1):
            r, i = _cmul_add(sfr, sfi, gfr[-1], gfi[-1], efr[s:s + 1], efi[s:s + 1])
            gfr.append(r)
            gfi.append(i)
        gbr, gbi = [z1], [z1]
        for s in range(SEGMENTS - 1, 0, -1):
            r, i = _cmul_add(sbr, sbi, gbr[0], gbi[0], ebr[s:s + 1], ebi[s:s + 1])
            gbr.insert(0, r)
            gbi.insert(0, i)
        gfr, gfi, gbr, gbi = (jnp.concatenate(g, axis=0) for g in (gfr, gfi, gbr, gbi))

        pfr = pbr = jnp.ones((1, q), F32)
        pfi = pbi = jnp.zeros((1, q), F32)
        for j in range(ks):
            jb = ks - 1 - j
            cf = xf_ref[b, group(j), :]
            cb_ = xb_ref[b, group(jb), :]
            xf_ref[b, group(j), :] = jnp.concatenate(
                [cf[:, :q] + pfr * gfr - pfi * gfi, cf[:, q:] + pfr * gfi + pfi * gfr], axis=1)
            xb_ref[b, group(jb), :] = jnp.concatenate(
                [cb_[:, :q] + pbr * gbr - pbi * gbi, cb_[:, q:] + pbr * gbi + pbi * gbr], axis=1)
            pfr, pfi = pfr * afr - pfi * afi, pfr * afi + pfi * afr
            pbr, pbi = pbr * abr - pbi * abi, pbr * abi + pbi * abr

    def phase_c(b):
        for r0 in range(0, cb, nr):
            rows = pl.ds(b * cb + r0, nr)
            xf = xf_ref[b, pl.ds(r0, nr), :]
            xb = xb_ref[b, pl.ds(r0, nr), :]
            lhs = jnp.concatenate([u_rows(b * cb + r0, nr), xf.astype(BF16), xb.astype(BF16)], axis=1)
            acc = jnp.dot(lhs, wc_ref[...], preferred_element_type=F32)
            for t in range(CHUNK):
                yt = acc[:, t * lw:(t + 1) * lw] + d_ref[...] * u_ref[t, rows, :].astype(F32)
                y_ref[t, rows, :] = jax.nn.gelu(yt).astype(y_ref.dtype)

    for b in range(nb):
        phase_a(b)
    for b in range(nb):
        phase_b(b)
    for b in range(nb):
        phase_c(b)


def _s5_ssm(u8, wa, wc, a1, aseg, dvec, *, chunks_per_batch, group, state):
    _, nc, w = u8.shape
    nslab = w // LANES
    nb = nc // chunks_per_batch
    cb = chunks_per_batch
    ks = cb // SEGMENTS
    gl = LANES // group
    half = 2 * gl * state
    tok = CHUNK * LANES
    body = functools.partial(_ssm_body, rows_per_dot=512, gl=gl, hsz=group, psz=state)
    return pl.pallas_call(
        body,
        grid=(nslab,),
        in_specs=[
            pl.BlockSpec((CHUNK, nc, LANES), lambda s: (0, 0, s)),
            pl.BlockSpec((None,) + wa.shape[1:], lambda s: (s, 0, 0)),
            pl.BlockSpec((None,) + wc.shape[1:], lambda s: (s, 0, 0)),
            pl.BlockSpec((None,) + a1.shape[1:], lambda s: (s, 0, 0)),
            pl.BlockSpec((None,) + aseg.shape[1:], lambda s: (s, 0, 0)),
            pl.BlockSpec((1, LANES), lambda s: (0, s)),
        ],
        out_specs=pl.BlockSpec((CHUNK, nc, LANES), lambda s: (0, 0, s)),
        out_shape=jax.ShapeDtypeStruct(u8.shape, BF16),
        scratch_shapes=[
            pltpu.VMEM((nb, cb, half), F32),
            pltpu.VMEM((nb, cb, half), F32),
            pltpu.VMEM((tok, 2 * half), BF16),
            pltpu.VMEM((tok + 2 * half, tok), BF16),
        ],
        compiler_params=_params(("arbitrary",)),
        name="s5_ssm",
    )(u8, wa, wc, a1, aseg, dvec)


def _s5_weights(lam_re, lam_im, log_step, b_re, b_im, c_re, c_im, *, seg_rows):
    _, g, p = lam_re.shape
    h = b_re.shape[-1]
    gl = LANES // h
    s = g // gl
    t = CHUNK
    lam = lax.complex(lam_re.astype(F32), lam_im.astype(F32))
    zed = lam * jnp.exp(log_step.astype(F32))[..., None]
    lam_bar = jnp.exp(zed)
    bbar = ((lam_bar - 1.0) / lam)[..., None] * lax.complex(b_re.astype(F32), b_im.astype(F32))
    cmat = lax.complex(c_re.astype(F32), c_im.astype(F32))
    pw = jnp.exp(zed[:, None] * jnp.arange(t + 1, dtype=F32)[None, :, None, None])
    pw_down = jnp.exp(zed[:, None] * (t - jnp.arange(t + 1, dtype=F32))[None, :, None, None])

    ma_f = pw_down[0, 1:, :, :, None] * bbar[0][None]
    ma_b = pw[1, :t, :, :, None] * bbar[1][None]
    wa = jnp.stack([ma_f.real, ma_f.imag, ma_b.real, ma_b.imag], axis=0)
    wa = wa.reshape(4, t, s, gl, p, h).transpose(2, 1, 3, 5, 0, 4).reshape(s, t * gl * h, 4 * p)

    kf = jnp.einsum('gyp,dgp,gph->dgyh', cmat[0], pw[0, :t], bbar[0]).real
    kb = jnp.einsum('gyp,dgp,gph->dgyh', cmat[1], pw[1, :t], bbar[1]).real
    st = jnp.arange(t)
    lag = st[None, None, :] - st[None, :, None]
    sel_f = (lag == st[:, None, None]).astype(F32)
    sel_b = (-lag == st[:, None, None]).astype(F32)
    coef = jnp.einsum('dat,dgyh->atgyh', sel_f, kf) + jnp.einsum('dat,dgyh->atgyh', sel_b, kb)
    w_u = coef.reshape(t, t, s, gl, h, h).transpose(2, 0, 3, 5, 1, 4).reshape(s, t * gl * h, t * h)

    cl_f = cmat[0][None] * pw[0, 1:][:, :, None, :]
    cl_b = cmat[1][None] * pw_down[1, :t][:, :, None, :]
    def state_rows(x):
        return x.reshape(t, s, gl, h, p).transpose(1, 2, 4, 0, 3).reshape(s, gl * p, t * h)
    wc = jnp.concatenate([w_u, state_rows(cl_f.real), state_rows(-cl_f.imag),
                          state_rows(cl_b.real), state_rows(-cl_b.imag)], axis=1)

    def lanes(x):
        x = x.reshape(2, s, gl * p)
        return jnp.stack([x[0].real, x[0].imag, x[1].real, x[1].imag], axis=1)
    a1 = lanes(pw[:, t])
    aseg = lanes(jnp.exp(zed * float(t * seg_rows)))
    return wa.astype(BF16), wc.astype(BF16), a1, aseg


def _s5_out_body(a_ref, w_ref, g_ref, side_ref, x_hbm, o_hbm, side_out, xin, xout, isem, osem, *, cb):
    side_out[...] = side_ref[...].astype(side_out.dtype)
    nj = pl.num_programs(1)
    step = pl.program_id(0) * nj + pl.program_id(1)
    total = pl.num_programs(0) * nj
    tm, tn = a_ref.shape[1], w_ref.shape[1]
    kb = tm // SEGMENTS
    slot = step % 2

    def window(st, t, s):
        cols = pl.ds(pl.multiple_of((st % nj) * tn, tn), tn)
        return (_plane_rows(st // nj, s, cb=cb, kb=kb), t, cols)

    def fetch(st, sl, t, s):
        return pltpu.make_async_copy(x_hbm.at[window(st, t, s)], xin.at[sl, t, :, s], isem.at[sl, t])

    def put(st, sl, t, s):
        return pltpu.make_async_copy(xout.at[sl, t, :, s], o_hbm.at[window(st, t, s)], osem.at[sl, t])

    def each(fn):
        for t in range(CHUNK):
            for s in range(SEGMENTS):
                fn(t, s)

    @pl.when(step == 0)
    def _():
        each(lambda t, s: fetch(0, 0, t, s).start())

    @pl.when(step + 1 < total)
    def _():
        each(lambda t, s: fetch(step + 1, 1 - slot, t, s).start())

    a = a_ref[...].reshape(CHUNK * tm, a_ref.shape[2])
    acc = jnp.dot(a, w_ref[...], preferred_element_type=F32)

    @pl.when(step >= 2)
    def _():
        each(lambda t, s: put(step - 2, slot, t, s).wait())

    for t in range(CHUNK):
        for s in range(SEGMENTS):
            fetch(step, slot, t, s).wait()
        xt = xin[slot, t].reshape(tm, tn) + g_ref[...] * acc[t * tm:(t + 1) * tm]
        xout[slot, t] = xt.reshape(kb, SEGMENTS, tn)
    each(lambda t, s: put(step, slot, t, s).start())

    @pl.when(step == total - 1)
    def _():
        each(lambda t, s: put(step, slot, t, s).wait())

        @pl.when(total >= 2)
        def _():
            each(lambda t, s: put(step - 1, 1 - slot, t, s).wait())


def _s5_out(z8, w, x3, gate, side, *, chunks_per_batch):
    _, nc, k = z8.shape
    d = w.shape[1]
    tm = min(128, chunks_per_batch)
    tn = min(512, d)
    per = chunks_per_batch // tm
    kb = tm // SEGMENTS
    nj = d // tn
    side_in, side_out, side_shape = _side_cast_specs(side[0], side[1], (nc // tm) * nj, lambda i, j: i * nj + j)
    return pl.pallas_call(
        functools.partial(_s5_out_body, cb=chunks_per_batch),
        grid=(nc // tm, nj),
        in_specs=[
            pl.BlockSpec((CHUNK, tm, k), lambda i, j: (0, i, 0)),
            pl.BlockSpec((k, tn), lambda i, j: (0, j)),
            pl.BlockSpec((None, 1, tn), lambda i, j: (i // per, 0, j)),
            side_in,
            pl.BlockSpec(memory_space=pl.ANY),
        ],
        out_specs=[pl.BlockSpec(memory_space=pl.ANY), side_out],
        out_shape=[jax.ShapeDtypeStruct(x3.shape, F32), side_shape],
        scratch_shapes=[
            pltpu.VMEM((2, CHUNK, kb, SEGMENTS, tn), F32),
            pltpu.VMEM((2, CHUNK, kb, SEGMENTS, tn), F32),
            pltpu.SemaphoreType.DMA((2, CHUNK)),
            pltpu.SemaphoreType.DMA((2, CHUNK)),
        ],
        compiler_params=_params(("arbitrary", "arbitrary")),
        name="s5_out",
    )(z8, w, gate, side[0], x3)


def _gm_gate_body(z_ref, lng_ref, lnb_ref, ws_ref, bs_ref, o_ref, *, chunk):
    tm, w2 = z_ref.shape
    w = w2 // 2
    heads = ws_ref.shape[0]
    hd = w // heads
    v = z_ref[:, w:].astype(F32)
    mu = jnp.mean(v, axis=-1, keepdims=True)
    vc = v - mu
    var = jnp.mean(vc * vc, axis=-1, keepdims=True)
    vn = (vc * lax.rsqrt(var + EPS) * lng_ref[...] + lnb_ref[...]).astype(BF16)
    for c in range(tm // chunk):
        rows = slice(c * chunk, (c + 1) * chunk)
        for hh in range(heads):
            cols = slice(hh * hd, (hh + 1) * hd)
            sv = jnp.dot(ws_ref[hh], vn[rows, cols], preferred_element_type=F32) + bs_ref[:, hh:hh + 1]
            o_ref[rows, cols] = (z_ref[rows, cols].astype(F32) * sv).astype(o_ref.dtype)


def _gm_gate(z, ln_g, ln_b, ws, bs, *, chunk):
    n, w2 = z.shape
    w = w2 // 2
    tm = 2 * chunk
    full = lambda a: pl.BlockSpec(a.shape, lambda i: (0,) * a.ndim)
    return pl.pallas_call(
        functools.partial(_gm_gate_body, chunk=chunk),
        grid=(n // tm,),
        in_specs=[pl.BlockSpec((tm, w2), lambda i: (i, 0)), full(ln_g), full(ln_b), full(ws), full(bs)],
        out_specs=pl.BlockSpec((tm, w), lambda i: (i, 0)),
        out_shape=jax.ShapeDtypeStruct((n, w), BF16),
        compiler_params=_params(("arbitrary",)),
        name="gmlp_gate",
    )(z, ln_g, ln_b, ws, bs)


def _pack_bf16_pairs(x):
    half = x.shape[1] // 2
    bits = lambda v: lax.bitcast_convert_type(v.astype(F32), jnp.int32)
    return bits(x[:, half:]) | lax.shift_right_logical(bits(x[:, :half]), 16)


def _unpack_bf16_pairs(w):
    lo = lax.bitcast_convert_type(lax.shift_left(w, 16), F32)
    hi = lax.bitcast_convert_type(w & jnp.int32(-65536), F32)
    return jnp.concatenate([lo.astype(BF16), hi.astype(BF16)], axis=1)


def _top2_sum(a, b, c, d):
    m1, n1 = jnp.maximum(a, b), jnp.minimum(a, b)
    m2, n2 = jnp.maximum(c, d), jnp.minimum(c, d)
    return jnp.maximum(m1, m2) + jnp.maximum(jnp.minimum(m1, m2), jnp.maximum(n1, n2))


def _first_argmax(vals):
    best, idx = vals[0], jnp.zeros(vals[0].shape, jnp.int32)
    for j in range(1, len(vals)):
        upd = vals[j] > best
        idx = jnp.where(upd, j, idx)
        best = jnp.where(upd, vals[j], best)
    return idx, best


def _ffn_prep_body(x_ref, g_ref, s_ref, rwt_ref, rb_ref, hn_ref, e_ref, gw_ref, rank_ref, cnt_ref, carry_ref):
    i = pl.program_id(0)
    n_e = rwt_ref.shape[1]
    epg = n_e // N_GROUPS_MOE
    tm = x_ref.shape[0]

    @pl.when(i == 0)
    def _():
        carry_ref[...] = jnp.zeros_like(carry_ref)

    hn = _rms_mod(x_ref[...], g_ref[...], s_ref[...])
    hn_hi = hn.astype(BF16)
    hn_lo = (hn - hn_hi.astype(F32)).astype(BF16)
    hn_ref[...] = _pack_bf16_pairs(hn_hi)
    nt_dot = lambda a, b: lax.dot_general(a, b, (((1,), (1,)), ((), ())), preferred_element_type=F32)
    logits = nt_dot(rwt_ref[0], hn_hi) + (nt_dot(rwt_ref[1], hn_hi) + nt_dot(rwt_ref[0], hn_lo))
    score = jax.nn.sigmoid(logits)
    sel = score + rb_ref[...]

    row = lambda a, r: a[r:r + 1, :]
    gscore = [_top2_sum(*[row(sel, g * epg + j) for j in range(epg)]) for g in range(N_GROUPS_MOE)]
    bg, _ = _first_argmax(gscore)
    pick = lambda a, j: functools.reduce(
        lambda acc, g: jnp.where(bg == g, row(a, g * epg + j), acc), range(1, N_GROUPS_MOE), row(a, j))
    sel_in = [pick(sel, j) for j in range(epg)]
    sc_in = [pick(score, j) for j in range(epg)]
    i1, _ = _first_argmax(sel_in)
    neg = jnp.full_like(sel_in[0], -jnp.inf)
    i2, _ = _first_argmax([jnp.where(i1 == j, neg, sel_in[j]) for j in range(epg)])
    take = lambda vals, idx: functools.reduce(lambda acc, j: jnp.where(idx == j, vals[j], acc), range(1, epg), vals[0])
    w1, w2 = take(sc_in, i1), take(sc_in, i2)
    den = w1 + w2
    e1, e2 = bg * epg + i1, bg * epg + i2
    e_ref[...] = jnp.concatenate([e1, e2], axis=0)
    gw_ref[...] = jnp.concatenate([w1 / den, w2 / den], axis=0)

    eid = lax.broadcasted_iota(jnp.int32, (n_e, tm), 0)
    oh1 = (eid == e1).astype(F32)
    oh2 = (eid == e2).astype(F32)
    oh = oh1 + oh2
    tri = (lax.broadcasted_iota(jnp.int32, (tm, tm), 0) <= lax.broadcasted_iota(jnp.int32, (tm, tm), 1)).astype(BF16)
    incl = jnp.dot(oh.astype(BF16), tri, preferred_element_type=F32)
    before = carry_ref[:, 0:1] + incl - oh
    r1 = jnp.sum(oh1 * before, axis=0, keepdims=True)
    r2 = jnp.sum(oh2 * before, axis=0, keepdims=True)
    rank_ref[...] = jnp.concatenate([r1, r2], axis=0).astype(jnp.int32)
    carry_ref[...] = carry_ref[...] + incl[:, tm - 1:tm]
    cnt_ref[...] = carry_ref[...].astype(jnp.int32)


def _ffn_prep(x, geff, shift, rwt, rb, *, seq):
    n, d = x.shape
    n_e = rwt.shape[1]
    tm = min(256, seq)
    per = seq // tm
    vec = pl.BlockSpec((None, 1, d), lambda i: (i // per, 0, 0))
    row2 = pl.BlockSpec((TOP_K, tm), lambda i: (0, i))
    hn, e, gw, rank, cnt = pl.pallas_call(
        _ffn_prep_body,
        grid=(n // tm,),
        in_specs=[pl.BlockSpec((tm, d), lambda i: (i, 0)), vec, vec,
                  pl.BlockSpec((2, n_e, d), lambda i: (0, 0, 0)), pl.BlockSpec((n_e, 1), lambda i: (0, 0))],
        out_specs=[pl.BlockSpec((tm, d // 2), lambda i: (i, 0)), row2, row2, row2,
                   pl.BlockSpec((n_e, LANES), lambda i: (0, 0))],
        out_shape=[jax.ShapeDtypeStruct((n, d // 2), jnp.int32), jax.ShapeDtypeStruct((TOP_K, n), jnp.int32),
                   jax.ShapeDtypeStruct((TOP_K, n), F32), jax.ShapeDtypeStruct((TOP_K, n), jnp.int32),
                   jax.ShapeDtypeStruct((n_e, LANES), jnp.int32)],
        scratch_shapes=[pltpu.VMEM((n_e, LANES), F32)],
        compiler_params=_params(("arbitrary",)),
        name="ffn_prep",
    )(x, geff, shift, rwt, rb)
    return hn, e, gw, rank, cnt[:, 0]


def _for_rows(lo, hi, fn, unroll=1):
    def body(r, c):
        fn(r)
        return c
    lax.fori_loop(lo, hi, body, 0, unroll=unroll)


def _dispatch_body(pos_ref, fill_ref, hn_ref, xs_hbm, zero_ref, sem, zsem, *, n_tokens):
    i = pl.program_id(0)
    tm = hn_ref.shape[0]

    def row_copy(r, k):
        return pltpu.make_async_copy(hn_ref.at[pl.ds(r, 1)], xs_hbm.at[pl.ds(pos_ref[k * n_tokens + i * tm + r], 1)], sem)

    def start(r):
        for k in range(TOP_K):
            row_copy(r, k).start()

    _for_rows(0, tm, start, unroll=8)

    @pl.when(i == 0)
    def _():
        zero_ref[...] = jnp.zeros_like(zero_ref)
        n_fill = fill_ref.shape[0] // 2

        def zero_copy(row):
            return pltpu.make_async_copy(zero_ref, xs_hbm.at[pl.ds(row, 1)], zsem)

        for e in range(n_fill):
            _for_rows(fill_ref[2 * e], fill_ref[2 * e + 1], lambda row: zero_copy(row).start())
        for e in range(n_fill):
            _for_rows(fill_ref[2 * e], fill_ref[2 * e + 1], lambda row: zero_copy(row).wait())

    for k in range(TOP_K):
        pltpu.make_async_copy(hn_ref, xs_hbm.at[pl.ds(0, tm)], sem).wait()


def _moe_dispatch(hn, pos_flat, fill, *, rows):
    n, dh = hn.shape
    tm = min(512, n)
    grid_spec = pltpu.PrefetchScalarGridSpec(
        num_scalar_prefetch=2,
        grid=(n // tm,),
        in_specs=[pl.BlockSpec((tm, dh), lambda i, p, f: (i, 0))],
        out_specs=pl.BlockSpec(memory_space=pl.ANY),
        scratch_shapes=[pltpu.VMEM((1, dh), jnp.int32), pltpu.SemaphoreType.DMA(()), pltpu.SemaphoreType.DMA(())],
    )
    return pl.pallas_call(
        functools.partial(_dispatch_body, n_tokens=n),
        grid_spec=grid_spec,
        out_shape=jax.ShapeDtypeStruct((rows, dh), jnp.int32),
        compiler_params=_params(("arbitrary",)),
        name="moe_dispatch",
    )(pos_flat, fill, hn)


def _moe_body(te_ref, nrows_ref, x_ref, w1_ref, w3_ref, w2_ref, o_ref):
    i = pl.program_id(0)
    used = nrows_ref[i] > 0

    @pl.when(used)
    def _():
        xb = _unpack_bf16_pairs(x_ref[...])
        h1 = jnp.dot(xb, w1_ref[...], preferred_element_type=F32)
        h3 = jnp.dot(xb, w3_ref[...], preferred_element_type=F32)
        hh = (h1 * jax.nn.sigmoid(h1) * h3).astype(BF16)
        o_ref[...] = jnp.dot(hh, w2_ref[...], preferred_element_type=F32)

    @pl.when(jnp.logical_not(used))
    def _():
        o_ref[...] = jnp.zeros_like(o_ref)


def _moe_experts(xs, w1, w3, w2, tile_expert, tile_rows):
    rows, dh = xs.shape
    d = 2 * dh
    fdim = w1.shape[-1]
    tm = MOE_TILE
    resident = pl.Buffered(1)
    grid_spec = pltpu.PrefetchScalarGridSpec(
        num_scalar_prefetch=2,
        grid=(rows // tm,),
        in_specs=[
            pl.BlockSpec((tm, dh), lambda i, te, nr: (i, 0)),
            pl.BlockSpec((None, d, fdim), lambda i, te, nr: (te[i], 0, 0), pipeline_mode=resident),
            pl.BlockSpec((None, d, fdim), lambda i, te, nr: (te[i], 0, 0), pipeline_mode=resident),
            pl.BlockSpec((None, fdim, d), lambda i, te, nr: (te[i], 0, 0), pipeline_mode=resident),
        ],
        out_specs=pl.BlockSpec((tm, d), lambda i, te, nr: (i, 0)),
    )
    return pl.pallas_call(
        _moe_body,
        grid_spec=grid_spec,
        out_shape=jax.ShapeDtypeStruct((rows, d), F32),
        compiler_params=_params(("arbitrary",)),
        name="moe_experts",
    )(tile_expert, tile_rows, xs, w1, w3, w2)


def _moe_plan(e, rank, cnt, n_tiles_max):
    tm = MOE_TILE
    n_e = cnt.shape[0]
    tiles_per = (cnt + tm - 1) // tm
    ends = jnp.cumsum(tiles_per)
    off = (ends - tiles_per) * tm
    onehot = e[..., None] == jnp.arange(n_e, dtype=jnp.int32)
    pos = jnp.sum(jnp.where(onehot, off, 0), axis=-1) + rank
    tile = jnp.arange(n_tiles_max, dtype=jnp.int32)
    tile_expert = jnp.minimum(jnp.sum(tile[:, None] >= ends[None, :], axis=1), n_e - 1).astype(jnp.int32)
    mine = tile_expert[:, None] == jnp.arange(n_e, dtype=jnp.int32)
    tile_rows = jnp.clip(jnp.sum(jnp.where(mine, off + cnt, 0), axis=1) - tile * tm, 0, tm)
    tile_rows = jnp.where(tile < ends[-1], tile_rows, 0).astype(jnp.int32)
    fill_end = jnp.where(jnp.arange(n_e) == n_e - 1, n_tiles_max * tm, ends * tm)
    fill = jnp.stack([off + cnt, fill_end], axis=1).reshape(-1).astype(jnp.int32)
    return pos.astype(jnp.int32), tile_expert, tile_rows, fill


COMBINE_ROW_CHUNKS = 8


def _combine_rows(pos_ref, x_ref, y_hbm, gw_ref, gate_ref, ybuf, sem, finish, *, n_tokens):
    i = pl.program_id(0)
    nsteps = pl.num_programs(0)
    tm = x_ref.shape[0]
    slot = i % 2
    nxt = (i + 1) % nsteps

    def row_copy(step, sl, r, k):
        row = pos_ref[k * n_tokens + step * tm + r]
        return pltpu.make_async_copy(y_hbm.at[pl.ds(row, 1)], ybuf.at[sl, k, pl.ds(r, 1)], sem.at[sl])

    def wait_tile(sl):
        for k in range(TOP_K):
            pltpu.make_async_copy(y_hbm.at[pl.ds(0, tm)], ybuf.at[sl, k], sem.at[sl]).wait()

    @pl.when(i == 0)
    def _():
        def one(r):
            for k in range(TOP_K):
                row_copy(0, 0, r, k).start()
        _for_rows(0, tm, one)

    wait_tile(slot)
    rc = tm // COMBINE_ROW_CHUNKS
    for c in range(COMBINE_ROW_CHUNKS):
        for r in range(c * rc, (c + 1) * rc):
            for k in range(TOP_K):
                row_copy(nxt, 1 - slot, r, k).start()
        rows = pl.ds(c * rc, rc)
        gw = gw_ref[rows, :]
        moe = gw[:, 0:1] * ybuf[slot, 0, rows, :] + gw[:, 1:2] * ybuf[slot, 1, rows, :]
        finish(rows, x_ref[rows, :] + gate_ref[...] * moe)

    @pl.when(i == nsteps - 1)
    def _():
        wait_tile(1 - slot)


def _combine_norm_body(pos_ref, x_ref, y_hbm, gw_ref, gate_ref, g_ref, s_ref, xo_ref, hn_ref, ybuf, sem, *, n_tokens):
    def finish(rows, xn):
        xo_ref[rows, :] = xn
        hn_ref[rows, :] = _rms_mod(xn, g_ref[...], s_ref[...]).astype(hn_ref.dtype)
    _combine_rows(pos_ref, x_ref, y_hbm, gw_ref, gate_ref, ybuf, sem, finish, n_tokens=n_tokens)


def _combine_final_body(pos_ref, x_ref, y_hbm, gw_ref, gate_ref, g_ref, o_ref, ybuf, sem, *, n_tokens):
    def finish(rows, xn):
        ms = jnp.mean(xn * xn, axis=-1, keepdims=True)
        o_ref[rows, :] = xn * lax.rsqrt(ms + EPS) * g_ref[...]
    _combine_rows(pos_ref, x_ref, y_hbm, gw_ref, gate_ref, ybuf, sem, finish, n_tokens=n_tokens)


def _ffn_combine(x, ys, pos_flat, gw_t, gate, norm_args, *, seq, final):
    n, d = x.shape
    tm = min(256, seq)
    per = seq // tm
    tile = pl.BlockSpec((tm, d), lambda i, p: (i, 0))
    vec = pl.BlockSpec((None, 1, d), lambda i, p: (i // per, 0, 0))
    in_specs = [tile, pl.BlockSpec(memory_space=pl.ANY), pl.BlockSpec((tm, TOP_K), lambda i, p: (i, 0)), vec]
    scratch = [pltpu.VMEM((2, TOP_K, tm, d), F32), pltpu.SemaphoreType.DMA((2,))]
    if final:
        body, name = _combine_final_body, "ffn_combine_final"
        in_specs.append(pl.BlockSpec((1, d), lambda i, p: (0, 0)))
        out_specs, out_shape = tile, jax.ShapeDtypeStruct((n, d), F32)
    else:
        body, name = _combine_norm_body, "ffn_combine_norm"
        in_specs += [vec, vec]
        out_specs = [tile, tile]
        out_shape = [jax.ShapeDtypeStruct((n, d), F32), jax.ShapeDtypeStruct((n, d), BF16)]
    grid_spec = pltpu.PrefetchScalarGridSpec(num_scalar_prefetch=1, grid=(n // tm,), in_specs=in_specs,
                                             out_specs=out_specs, scratch_shapes=scratch)
    return pl.pallas_call(
        functools.partial(body, n_tokens=n), grid_spec=grid_spec, out_shape=out_shape,
        compiler_params=_params(("arbitrary",)), name=name,
    )(pos_flat, x, ys, gw_t, gate, *norm_args)


def _moe_layer(x, geff, shift, rwt, rb, w1, w3, w2, *, seq):
    n, _ = x.shape
    n_e = rwt.shape[1]
    hn, e, gw, rank, cnt = _ffn_prep(x, geff, shift, rwt, rb, seq=seq)
    n_tiles_max = (TOP_K * n) // MOE_TILE + n_e
    pos, tile_expert, tile_rows, fill = _moe_plan(e, rank, cnt, n_tiles_max)
    pos_flat = pos.reshape(-1)
    xs = _moe_dispatch(hn, pos_flat, fill, rows=n_tiles_max * MOE_TILE)
    ys = _moe_experts(xs, w1, w3, w2, tile_expert, tile_rows)
    return ys, pos_flat, gw.T


def kernel(x, c, norm_mix_g, norm_ffn_g, norm_final_g, ada_mix_w, ada_mix_b, ada_ffn_w, ada_ffn_b, s5_w_in, s5_lam_re, s5_lam_im, s5_log_step, s5_b_re, s5_b_im, s5_c_re, s5_c_im, s5_d, s5_w_glu, s5_b_glu, s5_w_out, gm_w_in, gm_b_in, gm_ln_g, gm_ln_b, gm_ws, gm_bs, gm_w_out, router_w, router_bias, moe_w1, moe_w3, moe_w2):
    bsz, seq, d = x.shape
    n = bsz * seq
    depth = norm_mix_g.shape[0]
    assert depth == 2 and seq % (CHUNK * SEGMENTS) == 0
    cb = seq // CHUNK
    nc = n // CHUNK

    mods_mix = _adaln(c, ada_mix_w, ada_mix_b)
    mods_ffn = _adaln(c, ada_ffn_w, ada_ffn_b)

    def split(m, g):
        shift, scale, gate = jnp.split(m, 3, axis=-1)
        return (g[None] * (1.0 + scale))[:, None], shift[:, None], gate[:, None]

    rw_hi = router_w.T.astype(BF16)
    rwt = jnp.stack([rw_hi, (router_w.T - rw_hi.astype(F32)).astype(BF16)])
    rb = router_bias.reshape(-1, 1)
    xf = x.reshape(n, d)

    geff, shift, gate = split(mods_mix[0], norm_mix_g[0])
    x3 = xf.reshape(nc, CHUNK, d)
    hn8 = _norm_to_planes(x3, geff, shift, chunks_per_batch=cb)
    u8, w1_l0 = _matmul(hn8, s5_w_in[0].astype(BF16), tm=1024, tn=1024, out_dtype=BF16, epilogue=_ep_identity,
                        side=[(moe_w1, 0)], name="s5_in")
    wa, wc, a1, aseg = _s5_weights(s5_lam_re[0], s5_lam_im[0], s5_log_step[0], s5_b_re[0], s5_b_im[0],
                                   s5_c_re[0], s5_c_im[0], seg_rows=cb // SEGMENTS)
    y8 = _s5_ssm(u8, wa, wc, a1, aseg, s5_d[0].reshape(1, -1), chunks_per_batch=cb,
                 group=s5_b_re.shape[-1], state=s5_b_re.shape[-2])
    w = y8.shape[-1]
    tmg, tng = min(1024, nc), min(512, w)
    z8, w3_l0 = _matmul(y8, s5_w_glu[0].astype(BF16), tm=tmg, tn=tng, out_dtype=BF16, epilogue=_ep_glu,
                        extras=[(y8, pl.BlockSpec((None, tmg, tng), lambda q, i, j: (q, i, j))),
                                (s5_b_glu[0].reshape(1, -1), pl.BlockSpec((1, tng), lambda q, i, j: (0, j)))],
                        side=[(moe_w3, 0)], name="s5_glu")
    x1, w2_l0 = _s5_out(z8, s5_w_out[0].astype(BF16), x3, gate, (moe_w2, 0), chunks_per_batch=cb)
    x1 = x1.reshape(n, d)

    geff, shift, gate = split(mods_ffn[0], norm_ffn_g[0])
    ys, pos, gw_t = _moe_layer(x1, geff, shift, rwt, rb, w1_l0, w3_l0, w2_l0, seq=seq)
    geff1, shift1, gate1 = split(mods_mix[1], norm_mix_g[1])
    x2, hn = _ffn_combine(x1, ys, pos, gw_t, gate, (geff1, shift1), seq=seq, final=False)

    w2n = gm_w_in.shape[-1]
    tm_in, tn_in = min(1024, n), min(1024, w2n)
    zz, w1_l1 = _matmul(
        hn[None], gm_w_in[0].astype(BF16), tm=tm_in, tn=tn_in, out_dtype=BF16, epilogue=_ep_bias_gelu,
        extras=[(gm_b_in[0].reshape(1, -1), pl.BlockSpec((1, tn_in), lambda q, i, j: (0, j)))],
        side=[(moe_w1, 1)], name="gmlp_in")
    zz = zz[0]
    chunk = gm_ws.shape[-1]
    gated = _gm_gate(zz, gm_ln_g[0].reshape(1, -1), gm_ln_b[0].reshape(1, -1), gm_ws[0].astype(BF16),
                     gm_bs[0].T, chunk=chunk)
    tm_o, tn_o = min(1024, seq), min(512, d)
    per = seq // tm_o
    x3_, w3_l1, w2_l1 = _matmul(gated[None], gm_w_out[0].astype(BF16), tm=tm_o, tn=tn_o, out_dtype=F32,
                                epilogue=_ep_residual,
                                extras=[(x2[None], pl.BlockSpec((None, tm_o, tn_o), lambda q, i, j: (q, i, j))),
                                        (gate1, pl.BlockSpec((None, 1, tn_o), lambda q, i, j: (i // per, 0, j)))],
                                side=[(moe_w3, 1), (moe_w2, 1)], name="gmlp_out")
    x3_ = x3_[0]

    geff, shift, gate = split(mods_ffn[1], norm_ffn_g[1])
    ys, pos, gw_t = _moe_layer(x3_, geff, shift, rwt, rb, w1_l1, w3_l1, w2_l1, seq=seq)
    out = _ffn_combine(x3_, ys, pos, gw_t, gate, (norm_final_g.reshape(1, -1),), seq=seq, final=True)
    return out.reshape(bsz, seq, d)
```

```python
import functools

import jax
import jax.numpy as jnp
from jax import lax
from jax.experimental import pallas as pl
from jax.experimental.pallas import tpu as pltpu

EPS = 1e-6
CHUNK = 8
LANES = 128
SEGMENTS = 8
N_GROUPS_MOE = 4
TOP_K = 2
MOE_TILE = 256
VMEM_LIMIT = 56 * 1024 * 1024

F32 = jnp.float32
BF16 = jnp.bfloat16


def _params(sem):
    return pltpu.CompilerParams(dimension_semantics=sem, vmem_limit_bytes=VMEM_LIMIT)


def _rms_mod(x, geff, shift):
    ms = jnp.mean(x * x, axis=-1, keepdims=True)
    return x * lax.rsqrt(ms + EPS) * geff + shift


def _ada_body(ct_ref, w_ref, b_ref, o_ref):
    ct = ct_ref[...]
    st = ct * jax.nn.sigmoid(ct)
    w = w_ref[...]
    rows = [jnp.sum(st[:, b:b + 1] * w, axis=0, keepdims=True) for b in range(ct.shape[1])]
    o_ref[...] = jnp.concatenate(rows, axis=0) + b_ref[...]


def _adaln(c, w, b):
    depth, d, d3 = w.shape
    bsz = c.shape[0]
    tn = 1024 if d3 % 1024 == 0 else 512
    assert d3 % tn == 0
    return pl.pallas_call(
        _ada_body,
        grid=(depth, d3 // tn),
        in_specs=[
            pl.BlockSpec((d, bsz), lambda l, j: (0, 0)),
            pl.BlockSpec((None, d, tn), lambda l, j: (l, 0, j)),
            pl.BlockSpec((None, 1, tn), lambda l, j: (l, 0, j)),
        ],
        out_specs=pl.BlockSpec((None, bsz, tn), lambda l, j: (l, 0, j)),
        out_shape=jax.ShapeDtypeStruct((depth, bsz, d3), F32),
        compiler_params=_params(("arbitrary", "arbitrary")),
        name="adaln",
    )(c.T, w, b.reshape(depth, 1, d3))


def _side_cast_specs(src, layer, n_steps, step_of):
    _, n_e, r, c = src.shape
    per_step = (n_e * r) // n_steps
    assert per_step * n_steps == n_e * r and per_step % 16 == 0
    eb, rows = max(1, per_step // r), min(per_step, r)
    assert eb * rows == per_step and r % rows == 0 and n_e % eb == 0
    nb = r // rows
    in_spec = pl.BlockSpec((None, eb, rows, c), lambda *g: (layer, step_of(*g) // nb, step_of(*g) % nb, 0))
    out_spec = pl.BlockSpec((eb, rows, c), lambda *g: (step_of(*g) // nb, step_of(*g) % nb, 0))
    return in_spec, out_spec, jax.ShapeDtypeStruct((n_e, r, c), BF16)


def _mm_body(*refs, epilogue, n_extra, n_side):
    a_ref, w_ref = refs[0], refs[1]
    extra = refs[2:2 + n_extra]
    side_in = refs[2 + n_extra:2 + n_extra + n_side]
    o_ref = refs[2 + n_extra + n_side]
    side_out = refs[3 + n_extra + n_side:]
    acc = jnp.dot(a_ref[...], w_ref[...], preferred_element_type=F32)
    o_ref[...] = epilogue(acc, *extra).astype(o_ref.dtype)
    for s_in, s_out in zip(side_in, side_out):
        s_out[...] = s_in[...].astype(s_out.dtype)


def _matmul(a, w, *, tm, tn, out_dtype, epilogue, extras=(), side=(), name):
    p, m, k = a.shape
    n = w.shape[1]
    tm, tn = min(tm, m), min(tn, n)
    ni, nj = m // tm, n // tn
    side_specs = [_side_cast_specs(src, layer, p * ni * nj, lambda q, i, j: (q * ni + i) * nj + j)
                  for src, layer in side]
    in_specs = [
        pl.BlockSpec((None, tm, k), lambda q, i, j: (q, i, 0)),
        pl.BlockSpec((k, tn), lambda q, i, j: (0, j)),
    ] + [s for _, s in extras] + [s[0] for s in side_specs]
    out = pl.pallas_call(
        functools.partial(_mm_body, epilogue=epilogue, n_extra=len(extras), n_side=len(side)),
        grid=(p, ni, nj),
        in_specs=in_specs,
        out_specs=[pl.BlockSpec((None, tm, tn), lambda q, i, j: (q, i, j))] + [s[1] for s in side_specs],
        out_shape=[jax.ShapeDtypeStruct((p, m, n), out_dtype)] + [s[2] for s in side_specs],
        compiler_params=_params(("arbitrary", "arbitrary", "arbitrary")),
        name=name,
    )(a, w, *[x for x, _ in extras], *[src for src, _ in side])
    return out if side else out[0]


def _ep_identity(acc):
    return acc


def _ep_glu(acc, y_ref, b_ref):
    return y_ref[...].astype(F32) * jax.nn.sigmoid(acc + b_ref[...])


def _ep_bias_gelu(acc, b_ref):
    return jax.nn.gelu(acc + b_ref[...])


def _ep_residual(acc, x_ref, g_ref):
    return x_ref[...] + g_ref[...] * acc


def _plane_rows(block, seg, *, cb, kb):
    per = cb // (kb * SEGMENTS)
    return pl.ds((block // per) * cb + seg * (cb // SEGMENTS) + (block % per) * kb, kb)


def _norm_plane_body(x_hbm, g_ref, s_ref, o_ref, buf, sem, *, cb):
    i = pl.program_id(0)
    nsteps = pl.num_programs(0)
    kb = buf.shape[2]
    slot = i % 2

    def fetch(step, sl, t, s):
        return pltpu.make_async_copy(x_hbm.at[_plane_rows(step, s, cb=cb, kb=kb), t], buf.at[sl, t, :, s], sem.at[sl, t])

    def start_all(step, sl):
        for t in range(CHUNK):
            for s in range(SEGMENTS):
                fetch(step, sl, t, s).start()

    @pl.when(i == 0)
    def _():
        start_all(0, 0)

    @pl.when(i + 1 < nsteps)
    def _():
        start_all(i + 1, 1 - slot)

    for t in range(CHUNK):
        for s in range(SEGMENTS):
            fetch(i, slot, t, s).wait()
        xt = buf[slot, t].reshape(kb * SEGMENTS, buf.shape[-1])
        o_ref[t] = _rms_mod(xt, g_ref[...], s_ref[...]).astype(o_ref.dtype)


def _norm_to_planes(x3, geff, shift, *, chunks_per_batch):
    nc, _, d = x3.shape
    kb = 4
    tc = kb * SEGMENTS
    per = chunks_per_batch // tc
    vec = pl.BlockSpec((None, 1, d), lambda i: (i // per, 0, 0))
    return pl.pallas_call(
        functools.partial(_norm_plane_body, cb=chunks_per_batch),
        grid=(nc // tc,),
        in_specs=[pl.BlockSpec(memory_space=pl.ANY), vec, vec],
        out_specs=pl.BlockSpec((CHUNK, tc, d), lambda i: (0, i, 0)),
        out_shape=jax.ShapeDtypeStruct((CHUNK, nc, d), BF16),
        scratch_shapes=[pltpu.VMEM((2, CHUNK, kb, SEGMENTS, d), F32), pltpu.SemaphoreType.DMA((2, CHUNK))],
        compiler_params=_params(("arbitrary",)),
        name="s5_norm_planes",
    )(x3, geff, shift)


def _cmul_add(ar, ai, xr, xi, vr, vi):
    return ar * xr - ai * xi + vr, ar * xi + ai * xr + vi


def _ssm_body(u_ref, wac_ref, wcc_ref, a1_ref, aseg_ref, d_ref, y_ref, xf_ref, xb_ref, wa_ref, wc_ref, *,
              rows_per_dot, gl, hsz, psz):
    nb, cb = xf_ref.shape[0], xf_ref.shape[1]
    ks = cb // SEGMENTS
    half = xf_ref.shape[-1]
    q = half // 2

    if True:
        iota = lambda shape, ax: lax.broadcasted_iota(jnp.int32, shape, ax)
        ca, na = wac_ref.shape[1], wa_ref.shape[1]
        dcol = iota((ca, na), 1)
        spread = (iota((ca, na), 0) == (dcol // (gl * psz)) * psz + dcol % psz).astype(BF16)
        rep = jnp.dot(wac_ref[...], spread, preferred_element_type=F32)
        shp = rep.shape
        keep = (iota(shp, 0) // hsz) % gl == (iota(shp, 1) // psz) % gl
        wa_ref[...] = jnp.where(keep, rep, 0.0).astype(wa_ref.dtype)
        cc, nc_ = wcc_ref.shape[1], wc_ref.shape[1]
        dcol = iota((cc, nc_), 1)
        spread = (iota((cc, nc_), 0) == (dcol // (gl * hsz)) * hsz + dcol % hsz).astype(BF16)
        rep = jnp.dot(wcc_ref[...], spread, preferred_element_type=F32)
        shp = rep.shape
        n_tok_rows = CHUNK * gl * hsz
        r = iota(shp, 0)
        row_g = jnp.where(r < n_tok_rows, (r // hsz) % gl, ((r - n_tok_rows) // psz) % gl)
        keep = row_g == (iota(shp, 1) // hsz) % gl
        wc_ref[...] = jnp.where(keep, rep, 0.0).astype(wc_ref.dtype)

    def u_rows(r0, nr):
        return jnp.concatenate([u_ref[t, pl.ds(r0, nr), :] for t in range(CHUNK)], axis=1)

    nr = min(cb, rows_per_dot)
    afr, afi, abr, abi = (a1_ref[i:i + 1, :] for i in range(4))
    sfr, sfi, sbr, sbi = (aseg_ref[i:i + 1, :] for i in range(4))
    group = lambda j: slice(j * SEGMENTS, (j + 1) * SEGMENTS)
    lw = d_ref.shape[-1]

    def phase_a(b):
        for r0 in range(0, cb, nr):
            res = jnp.dot(u_rows(b * cb + r0, nr), wa_ref[...], preferred_element_type=F32)
            xf_ref[b, pl.ds(r0, nr), :] = res[:, :half]
            xb_ref[b, pl.ds(r0, nr), :] = res[:, half:]

    def phase_b(b):
        z = jnp.zeros((SEGMENTS, q), F32)
        cfr, cfi, cbr, cbi = z, z, z, z
        for j in range(ks):
            jb = ks - 1 - j
            vf = xf_ref[b, group(j), :]
            vb = xb_ref[b, group(jb), :]
            xf_ref[b, group(j), :] = jnp.concatenate([cfr, cfi], axis=1)
            xb_ref[b, group(jb), :] = jnp.concatenate([cbr, cbi], axis=1)
            cfr, cfi = _cmul_add(afr, afi, cfr, cfi, vf[:, :q], vf[:, q:])
            cbr, cbi = _cmul_add(abr, abi, cbr, cbi, vb[:, :q], vb[:, q:])
        efr, efi, ebr, ebi = cfr, cfi, cbr, cbi

        z1 = jnp.zeros((1, q), F32)
        gfr, gfi = [z1], [z1]
        for s in range(SEGMENTS - 1):
            r, i = _cmul_add(sfr, sfi, gfr[-1], gfi[-1], efr[s:s + 1], efi[s:s + 1])
            gfr.append(r)
            gfi.append(i)
        gbr, gbi = [z1], [z1]
        for s in range(SEGMENTS - 1, 0, -1):
            r, i = _cmul_add(sbr, sbi, gbr[0], gbi[0], ebr[s:s + 1], ebi[s:s + 1])
            gbr.insert(0, r)
            gbi.insert(0, i)
        gfr, gfi, gbr, gbi = (jnp.concatenate(g, axis=0) for g in (gfr, gfi, gbr, gbi))

        pfr = pbr = jnp.ones((1, q), F32)
        pfi = pbi = jnp.zeros((1, q), F32)
        for j in range(ks):
            jb = ks - 1 - j
            cf = xf_ref[b, group(j), :]
            cb_ = xb_ref[b, group(jb), :]
            xf_ref[b, group(j), :] = jnp.concatenate(
                [cf[:, :q] + pfr * gfr - pfi * gfi, cf[:, q:] + pfr * gfi + pfi * gfr], axis=1)
            xb_ref[b, group(jb), :] = jnp.concatenate(
                [cb_[:, :q] + pbr * gbr - pbi * gbi, cb_[:, q:] + pbr * gbi + pbi * gbr], axis=1)
            pfr, pfi = pfr * afr - pfi * afi, pfr * afi + pfi * afr
            pbr, pbi = pbr * abr - pbi * abi, pbr * abi + pbi * abr

    def phase_c(b):
        for r0 in range(0, cb, nr):
            rows = pl.ds(b * cb + r0, nr)
            xf = xf_ref[b, pl.ds(r0, nr), :]
            xb = xb_ref[b, pl.ds(r0, nr), :]
            lhs = jnp.concatenate([u_rows(b * cb + r0, nr), xf.astype(BF16), xb.astype(BF16)], axis=1)
            acc = jnp.dot(lhs, wc_ref[...], preferred_element_type=F32)
            for t in range(CHUNK):
                yt = acc[:, t * lw:(t + 1) * lw] + d_ref[...] * u_ref[t, rows, :].astype(F32)
                y_ref[t, rows, :] = jax.nn.gelu(yt).astype(y_ref.dtype)

    for b in range(nb):
        phase_a(b)
    for b in range(nb):
        phase_b(b)
    for b in range(nb):
        phase_c(b)


def _s5_ssm(u8, wa, wc, a1, aseg, dvec, *, chunks_per_batch, group, state):
    _, nc, w = u8.shape
    nslab = w // LANES
    nb = nc // chunks_per_batch
    cb = chunks_per_batch
    ks = cb // SEGMENTS
    gl = LANES // group
    half = 2 * gl * state
    tok = CHUNK * LANES
    body = functools.partial(_ssm_body, rows_per_dot=512, gl=gl, hsz=group, psz=state)
    return pl.pallas_call(
        body,
        grid=(nslab,),
        in_specs=[
            pl.BlockSpec((CHUNK, nc, LANES), lambda s: (0, 0, s)),
            pl.BlockSpec((None,) + wa.shape[1:], lambda s: (s, 0, 0)),
            pl.BlockSpec((None,) + wc.shape[1:], lambda s: (s, 0, 0)),
            pl.BlockSpec((None,) + a1.shape[1:], lambda s: (s, 0, 0)),
            pl.BlockSpec((None,) + aseg.shape[1:], lambda s: (s, 0, 0)),
            pl.BlockSpec((1, LANES), lambda s: (0, s)),
        ],
        out_specs=pl.BlockSpec((CHUNK, nc, LANES), lambda s: (0, 0, s)),
        out_shape=jax.ShapeDtypeStruct(u8.shape, BF16),
        scratch_shapes=[
            pltpu.VMEM((nb, cb, half), F32),
            pltpu.VMEM((nb, cb, half), F32),
            pltpu.VMEM((tok, 2 * half), BF16),
            pltpu.VMEM((tok + 2 * half, tok), BF16),
        ],
        compiler_params=_params(("arbitrary",)),
        name="s5_ssm",
    )(u8, wa, wc, a1, aseg, dvec)


def _s5_weights(lam_re, lam_im, log_step, b_re, b_im, c_re, c_im, *, seg_rows):
    _, g, p = lam_re.shape
    h = b_re.shape[-1]
    gl = LANES // h
    s = g // gl
    t = CHUNK
    lam = lax.complex(lam_re.astype(F32), lam_im.astype(F32))
    zed = lam * jnp.exp(log_step.astype(F32))[..., None]
    lam_bar = jnp.exp(zed)
    bbar = ((lam_bar - 1.0) / lam)[..., None] * lax.complex(b_re.astype(F32), b_im.astype(F32))
    cmat = lax.complex(c_re.astype(F32), c_im.astype(F32))
    pw = jnp.exp(zed[:, None] * jnp.arange(t + 1, dtype=F32)[None, :, None, None])
    pw_down = jnp.exp(zed[:, None] * (t - jnp.arange(t + 1, dtype=F32))[None, :, None, None])

    ma_f = pw_down[0, 1:, :, :, None] * bbar[0][None]
    ma_b = pw[1, :t, :, :, None] * bbar[1][None]
    wa = jnp.stack([ma_f.real, ma_f.imag, ma_b.real, ma_b.imag], axis=0)
    wa = wa.reshape(4, t, s, gl, p, h).transpose(2, 1, 3, 5, 0, 4).reshape(s, t * gl * h, 4 * p)

    kf = jnp.einsum('gyp,dgp,gph->dgyh', cmat[0], pw[0, :t], bbar[0]).real
    kb = jnp.einsum('gyp,dgp,gph->dgyh', cmat[1], pw[1, :t], bbar[1]).real
    st = jnp.arange(t)
    lag = st[None, None, :] - st[None, :, None]
    sel_f = (lag == st[:, None, None]).astype(F32)
    sel_b = (-lag == st[:, None, None]).astype(F32)
    coef = jnp.einsum('dat,dgyh->atgyh', sel_f, kf) + jnp.einsum('dat,dgyh->atgyh', sel_b, kb)
    w_u = coef.reshape(t, t, s, gl, h, h).transpose(2, 0, 3, 5, 1, 4).reshape(s, t * gl * h, t * h)

    cl_f = cmat[0][None] * pw[0, 1:][:, :, None, :]
    cl_b = cmat[1][None] * pw_down[1, :t][:, :, None, :]
    def state_rows(x):
        return x.reshape(t, s, gl, h, p).transpose(1, 2, 4, 0, 3).reshape(s, gl * p, t * h)
    wc = jnp.concatenate([w_u, state_rows(cl_f.real), state_rows(-cl_f.imag),
                          state_rows(cl_b.real), state_rows(-cl_b.imag)], axis=1)

    def lanes(x):
        x = x.reshape(2, s, gl * p)
        return jnp.stack([x[0].real, x[0].imag, x[1].real, x[1].imag], axis=1)
    a1 = lanes(pw[:, t])
    aseg = lanes(jnp.exp(zed * float(t * seg_rows)))
    return wa.astype(BF16), wc.astype(BF16), a1, aseg


def _s5_out_body(a_ref, w_ref, g_ref, side_ref, x_hbm, o_hbm, side_out, xin, xout, isem, osem, *, cb):
    side_out[...] = side_ref[...].astype(side_out.dtype)
    nj = pl.num_programs(1)
    step = pl.program_id(0) * nj + pl.program_id(1)
    total = pl.num_programs(0) * nj
    tm, tn = a_ref.shape[1], w_ref.shape[1]
    kb = tm // SEGMENTS
    slot = step % 2

    def window(st, t, s):
        cols = pl.ds(pl.multiple_of((st % nj) * tn, tn), tn)
        return (_plane_rows(st // nj, s, cb=cb, kb=kb), t, cols)

    def fetch(st, sl, t, s):
        return pltpu.make_async_copy(x_hbm.at[window(st, t, s)], xin.at[sl, t, :, s], isem.at[sl, t])

    def put(st, sl, t, s):
        return pltpu.make_async_copy(xout.at[sl, t, :, s], o_hbm.at[window(st, t, s)], osem.at[sl, t])

    def each(fn):
        for t in range(CHUNK):
            for s in range(SEGMENTS):
                fn(t, s)

    @pl.when(step == 0)
    def _():
        each(lambda t, s: fetch(0, 0, t, s).start())

    @pl.when(step + 1 < total)
    def _():
        each(lambda t, s: fetch(step + 1, 1 - slot, t, s).start())

    a = a_ref[...].reshape(CHUNK * tm, a_ref.shape[2])
    acc = jnp.dot(a, w_ref[...], preferred_element_type=F32)

    @pl.when(step >= 2)
    def _():
        each(lambda t, s: put(step - 2, slot, t, s).wait())

    for t in range(CHUNK):
        for s in range(SEGMENTS):
            fetch(step, slot, t, s).wait()
        xt = xin[slot, t].reshape(tm, tn) + g_ref[...] * acc[t * tm:(t + 1) * tm]
        xout[slot, t] = xt.reshape(kb, SEGMENTS, tn)
    each(lambda t, s: put(step, slot, t, s).start())

    @pl.when(step == total - 1)
    def _():
        each(lambda t, s: put(step, slot, t, s).wait())

        @pl.when(total >= 2)
        def _():
            each(lambda t, s: put(step - 1, 1 - slot, t, s).wait())


def _s5_out(z8, w, x3, gate, side, *, chunks_per_batch):
    _, nc, k = z8.shape
    d = w.shape[1]
    tm = min(128, chunks_per_batch)
    tn = min(512, d)
    per = chunks_per_batch // tm
    kb = tm // SEGMENTS
    nj = d // tn
    side_in, side_out, side_shape = _side_cast_specs(side[0], side[1], (nc // tm) * nj, lambda i, j: i * nj + j)
    return pl.pallas_call(
        functools.partial(_s5_out_body, cb=chunks_per_batch),
        grid=(nc // tm, nj),
        in_specs=[
            pl.BlockSpec((CHUNK, tm, k), lambda i, j: (0, i, 0)),
            pl.BlockSpec((k, tn), lambda i, j: (0, j)),
            pl.BlockSpec((None, 1, tn), lambda i, j: (i // per, 0, j)),
            side_in,
            pl.BlockSpec(memory_space=pl.ANY),
        ],
        out_specs=[pl.BlockSpec(memory_space=pl.ANY), side_out],
        out_shape=[jax.ShapeDtypeStruct(x3.shape, F32), side_shape],
        scratch_shapes=[
            pltpu.VMEM((2, CHUNK, kb, SEGMENTS, tn), F32),
            pltpu.VMEM((2, CHUNK, kb, SEGMENTS, tn), F32),
            pltpu.SemaphoreType.DMA((2, CHUNK)),
            pltpu.SemaphoreType.DMA((2, CHUNK)),
        ],
        compiler_params=_params(("arbitrary", "arbitrary")),
        name="s5_out",
    )(z8, w, gate, side[0], x3)


def _gm_gate_body(z_ref, lng_ref, lnb_ref, ws_ref, bs_ref, o_ref, *, chunk):
    tm, w2 = z_ref.shape
    w = w2 // 2
    heads = ws_ref.shape[0]
    hd = w // heads
    v = z_ref[:, w:].astype(F32)
    mu = jnp.mean(v, axis=-1, keepdims=True)
    vc = v - mu
    var = jnp.mean(vc * vc, axis=-1, keepdims=True)
    vn = (vc * lax.rsqrt(var + EPS) * lng_ref[...] + lnb_ref[...]).astype(BF16)
    for c in range(tm // chunk):
        rows = slice(c * chunk, (c + 1) * chunk)
        for hh in range(heads):
            cols = slice(hh * hd, (hh + 1) * hd)
            sv = jnp.dot(ws_ref[hh], vn[rows, cols], preferred_element_type=F32) + bs_ref[:, hh:hh + 1]
            o_ref[rows, cols] = (z_ref[rows, cols].astype(F32) * sv).astype(o_ref.dtype)


def _gm_gate(z, ln_g, ln_b, ws, bs, *, chunk):
    n, w2 = z.shape
    w = w2 // 2
    tm = 2 * chunk
    full = lambda a: pl.BlockSpec(a.shape, lambda i: (0,) * a.ndim)
    return pl.pallas_call(
        functools.partial(_gm_gate_body, chunk=chunk),
        grid=(n // tm,),
        in_specs=[pl.BlockSpec((tm, w2), lambda i: (i, 0)), full(ln_g), full(ln_b), full(ws), full(bs)],
        out_specs=pl.BlockSpec((tm, w), lambda i: (i, 0)),
        out_shape=jax.ShapeDtypeStruct((n, w), BF16),
        compiler_params=_params(("arbitrary",)),
        name="gmlp_gate",
    )(z, ln_g, ln_b, ws, bs)


def _pack_bf16_pairs(x):
    half = x.shape[1] // 2
    bits = lambda v: lax.bitcast_convert_type(v.astype(F32), jnp.int32)
    return bits(x[:, half:]) | lax.shift_right_logical(bits(x[:, :half]), 16)


def _unpack_bf16_pairs(w):
    lo = lax.bitcast_convert_type(lax.shift_left(w, 16), F32)
    hi = lax.bitcast_convert_type(w & jnp.int32(-65536), F32)
    return jnp.concatenate([lo.astype(BF16), hi.astype(BF16)], axis=1)


def _top2_sum(a, b, c, d):
    m1, n1 = jnp.maximum(a, b), jnp.minimum(a, b)
    m2, n2 = jnp.maximum(c, d), jnp.minimum(c, d)
    return jnp.maximum(m1, m2) + jnp.maximum(jnp.minimum(m1, m2), jnp.maximum(n1, n2))


def _first_argmax(vals):
    best, idx = vals[0], jnp.zeros(vals[0].shape, jnp.int32)
    for j in range(1, len(vals)):
        upd = vals[j] > best
        idx = jnp.where(upd, j, idx)
        best = jnp.where(upd, vals[j], best)
    return idx, best


def _ffn_prep_body(x_ref, g_ref, s_ref, rwt_ref, rb_ref, hn_ref, e_ref, gw_ref, rank_ref, cnt_ref, carry_ref):
    i = pl.program_id(0)
    n_e = rwt_ref.shape[1]
    epg = n_e // N_GROUPS_MOE
    tm = x_ref.shape[0]

    @pl.when(i == 0)
    def _():
        carry_ref[...] = jnp.zeros_like(carry_ref)

    hn = _rms_mod(x_ref[...], g_ref[...], s_ref[...])
    hn_hi = hn.astype(BF16)
    hn_lo = (hn - hn_hi.astype(F32)).astype(BF16)
    hn_ref[...] = _pack_bf16_pairs(hn_hi)
    nt_dot = lambda a, b: lax.dot_general(a, b, (((1,), (1,)), ((), ())), preferred_element_type=F32)
    logits = nt_dot(rwt_ref[0], hn_hi) + (nt_dot(rwt_ref[1], hn_hi) + nt_dot(rwt_ref[0], hn_lo))
    score = jax.nn.sigmoid(logits)
    sel = score + rb_ref[...]

    row = lambda a, r: a[r:r + 1, :]
    gscore = [_top2_sum(*[row(sel, g * epg + j) for j in range(epg)]) for g in range(N_GROUPS_MOE)]
    bg, _ = _first_argmax(gscore)
    pick = lambda a, j: functools.reduce(
        lambda acc, g: jnp.where(bg == g, row(a, g * epg + j), acc), range(1, N_GROUPS_MOE), row(a, j))
    sel_in = [pick(sel, j) for j in range(epg)]
    sc_in = [pick(score, j) for j in range(epg)]
    i1, _ = _first_argmax(sel_in)
    neg = jnp.full_like(sel_in[0], -jnp.inf)
    i2, _ = _first_argmax([jnp.where(i1 == j, neg, sel_in[j]) for j in range(epg)])
    take = lambda vals, idx: functools.reduce(lambda acc, j: jnp.where(idx == j, vals[j], acc), range(1, epg), vals[0])
    w1, w2 = take(sc_in, i1), take(sc_in, i2)
    den = w1 + w2
    e1, e2 = bg * epg + i1, bg * epg + i2
    e_ref[...] = jnp.concatenate([e1, e2], axis=0)
    gw_ref[...] = jnp.concatenate([w1 / den, w2 / den], axis=0)

    eid = lax.broadcasted_iota(jnp.int32, (n_e, tm), 0)
    oh1 = (eid == e1).astype(F32)
    oh2 = (eid == e2).astype(F32)
    oh = oh1 + oh2
    tri = (lax.broadcasted_iota(jnp.int32, (tm, tm), 0) <= lax.broadcasted_iota(jnp.int32, (tm, tm), 1)).astype(BF16)
    incl = jnp.dot(oh.astype(BF16), tri, preferred_element_type=F32)
    before = carry_ref[:, 0:1] + incl - oh
    r1 = jnp.sum(oh1 * before, axis=0, keepdims=True)
    r2 = jnp.sum(oh2 * before, axis=0, keepdims=True)
    rank_ref[...] = jnp.concatenate([r1, r2], axis=0).astype(jnp.int32)
    carry_ref[...] = carry_ref[...] + incl[:, tm - 1:tm]
    cnt_ref[...] = carry_ref[...].astype(jnp.int32)


def _ffn_prep(x, geff, shift, rwt, rb, *, seq):
    n, d = x.shape
    n_e = rwt.shape[1]
    tm = min(256, seq)
    per = seq // tm
    vec = pl.BlockSpec((None, 1, d), lambda i: (i // per, 0, 0))
    row2 = pl.BlockSpec((TOP_K, tm), lambda i: (0, i))
    hn, e, gw, rank, cnt = pl.pallas_call(
        _ffn_prep_body,
        grid=(n // tm,),
        in_specs=[pl.BlockSpec((tm, d), lambda i: (i, 0)), vec, vec,
                  pl.BlockSpec((2, n_e, d), lambda i: (0, 0, 0)), pl.BlockSpec((n_e, 1), lambda i: (0, 0))],
        out_specs=[pl.BlockSpec((tm, d // 2), lambda i: (i, 0)), row2, row2, row2,
                   pl.BlockSpec((n_e, LANES), lambda i: (0, 0))],
        out_shape=[jax.ShapeDtypeStruct((n, d // 2), jnp.int32), jax.ShapeDtypeStruct((TOP_K, n), jnp.int32),
                   jax.ShapeDtypeStruct((TOP_K, n), F32), jax.ShapeDtypeStruct((TOP_K, n), jnp.int32),
                   jax.ShapeDtypeStruct((n_e, LANES), jnp.int32)],
        scratch_shapes=[pltpu.VMEM((n_e, LANES), F32)],
        compiler_params=_params(("arbitrary",)),
        name="ffn_prep",
    )(x, geff, shift, rwt, rb)
    return hn, e, gw, rank, cnt[:, 0]


def _for_rows(lo, hi, fn, unroll=1):
    def body(r, c):
        fn(r)
        return c
    lax.fori_loop(lo, hi, body, 0, unroll=unroll)


def _dispatch_body(pos_ref, fill_ref, hn_ref, xs_hbm, zero_ref, sem, zsem, *, n_tokens):
    i = pl.program_id(0)
    tm = hn_ref.shape[0]

    def row_copy(r, k):
        return pltpu.make_async_copy(hn_ref.at[pl.ds(r, 1)], xs_hbm.at[pl.ds(pos_ref[k * n_tokens + i * tm + r], 1)], sem)

    def start(r):
        for k in range(TOP_K):
            row_copy(r, k).start()

    for r in range(tm):
        start(r)

    @pl.when(i == 0)
    def _():
        zero_ref[...] = jnp.zeros_like(zero_ref)
        n_fill = fill_ref.shape[0] // 2

        def zero_copy(row):
            return pltpu.make_async_copy(zero_ref, xs_hbm.at[pl.ds(row, 1)], zsem)

        for e in range(n_fill):
            _for_rows(fill_ref[2 * e], fill_ref[2 * e + 1], lambda row: zero_copy(row).start())
        for e in range(n_fill):
            _for_rows(fill_ref[2 * e], fill_ref[2 * e + 1], lambda row: zero_copy(row).wait())

    for k in range(TOP_K):
        pltpu.make_async_copy(hn_ref, xs_hbm.at[pl.ds(0, tm)], sem).wait()


def _moe_dispatch(hn, pos_flat, fill, *, rows):
    n, dh = hn.shape
    tm = min(512, n)
    grid_spec = pltpu.PrefetchScalarGridSpec(
        num_scalar_prefetch=2,
        grid=(n // tm,),
        in_specs=[pl.BlockSpec((tm, dh), lambda i, p, f: (i, 0))],
        out_specs=pl.BlockSpec(memory_space=pl.ANY),
        scratch_shapes=[pltpu.VMEM((1, dh), jnp.int32), pltpu.SemaphoreType.DMA(()), pltpu.SemaphoreType.DMA(())],
    )
    return pl.pallas_call(
        functools.partial(_dispatch_body, n_tokens=n),
        grid_spec=grid_spec,
        out_shape=jax.ShapeDtypeStruct((rows, dh), jnp.int32),
        compiler_params=_params(("arbitrary",)),
        name="moe_dispatch",
    )(pos_flat, fill, hn)


def _moe_body(te_ref, nrows_ref, x_ref, w1_ref, w3_ref, w2_ref, o_ref):
    i = pl.program_id(0)
    used = nrows_ref[i] > 0

    @pl.when(used)
    def _():
        xb = _unpack_bf16_pairs(x_ref[...])
        h1 = jnp.dot(xb, w1_ref[...], preferred_element_type=F32)
        h3 = jnp.dot(xb, w3_ref[...], preferred_element_type=F32)
        hh = (h1 * jax.nn.sigmoid(h1) * h3).astype(BF16)
        out = jnp.dot(hh, w2_ref[...], preferred_element_type=F32)
        o_ref[...] = _pack_bf16_pairs(out.astype(BF16))

    @pl.when(jnp.logical_not(used))
    def _():
        o_ref[...] = jnp.zeros_like(o_ref)


def _moe_experts(xs, w1, w3, w2, tile_expert, tile_rows):
    rows, dh = xs.shape
    d = 2 * dh
    fdim = w1.shape[-1]
    tm = MOE_TILE
    resident = pl.Buffered(1)
    grid_spec = pltpu.PrefetchScalarGridSpec(
        num_scalar_prefetch=2,
        grid=(rows // tm,),
        in_specs=[
            pl.BlockSpec((tm, dh), lambda i, te, nr: (i, 0)),
            pl.BlockSpec((None, d, fdim), lambda i, te, nr: (te[i], 0, 0)),
            pl.BlockSpec((None, d, fdim), lambda i, te, nr: (te[i], 0, 0), pipeline_mode=resident),
            pl.BlockSpec((None, fdim, d), lambda i, te, nr: (te[i], 0, 0), pipeline_mode=resident),
        ],
        out_specs=pl.BlockSpec((tm, dh), lambda i, te, nr: (i, 0)),
    )
    return pl.pallas_call(
        _moe_body,
        grid_spec=grid_spec,
        out_shape=jax.ShapeDtypeStruct((rows, dh), jnp.int32),
        compiler_params=_params(("arbitrary",)),
        name="moe_experts",
    )(tile_expert, tile_rows, xs, w1, w3, w2)


def _moe_plan(e, rank, cnt, n_tiles_max):
    tm = MOE_TILE
    n_e = cnt.shape[0]
    tiles_per = (cnt + tm - 1) // tm
    ends = jnp.cumsum(tiles_per)
    off = (ends - tiles_per) * tm
    onehot = e[..., None] == jnp.arange(n_e, dtype=jnp.int32)
    pos = jnp.sum(jnp.where(onehot, off, 0), axis=-1) + rank
    tile = jnp.arange(n_tiles_max, dtype=jnp.int32)
    tile_expert = jnp.minimum(jnp.sum(tile[:, None] >= ends[None, :], axis=1), n_e - 1).astype(jnp.int32)
    mine = tile_expert[:, None] == jnp.arange(n_e, dtype=jnp.int32)
    tile_rows = jnp.clip(jnp.sum(jnp.where(mine, off + cnt, 0), axis=1) - tile * tm, 0, tm)
    tile_rows = jnp.where(tile < ends[-1], tile_rows, 0).astype(jnp.int32)
    fill_end = jnp.where(jnp.arange(n_e) == n_e - 1, n_tiles_max * tm, ends * tm)
    fill = jnp.stack([off + cnt, fill_end], axis=1).reshape(-1).astype(jnp.int32)
    return pos.astype(jnp.int32), tile_expert, tile_rows, fill


COMBINE_ROW_CHUNKS = 8


def _combine_rows(pos_ref, x_ref, y_hbm, gw_ref, gate_ref, ybuf, sem, finish, *, n_tokens):
    i = pl.program_id(0)
    nsteps = pl.num_programs(0)
    tm = x_ref.shape[0]
    slot = i % 2
    nxt = (i + 1) % nsteps

    def row_copy(step, sl, r, k):
        row = pos_ref[k * n_tokens + step * tm + r]
        return pltpu.make_async_copy(y_hbm.at[pl.ds(row, 1)], ybuf.at[sl, k, pl.ds(r, 1)], sem.at[sl])

    def wait_tile(sl):
        for k in range(TOP_K):
            pltpu.make_async_copy(y_hbm.at[pl.ds(0, tm)], ybuf.at[sl, k], sem.at[sl]).wait()

    @pl.when(i == 0)
    def _():
        def one(r):
            for k in range(TOP_K):
                row_copy(0, 0, r, k).start()
        _for_rows(0, tm, one)

    wait_tile(slot)
    rc = tm // COMBINE_ROW_CHUNKS
    for c in range(COMBINE_ROW_CHUNKS):
        for r in range(c * rc, (c + 1) * rc):
            for k in range(TOP_K):
                row_copy(nxt, 1 - slot, r, k).start()
        rows = pl.ds(c * rc, rc)
        gw = gw_ref[rows, :]
        y0, y1 = ybuf[slot, 0, rows, :], ybuf[slot, 1, rows, :]
        halves = []
        for part in (lambda w: lax.shift_left(w, 16), lambda w: w & jnp.int32(-65536)):
            f = lambda w: lax.bitcast_convert_type(part(w), F32)
            halves.append(gw[:, 0:1] * f(y0) + gw[:, 1:2] * f(y1))
        moe = jnp.concatenate(halves, axis=1)
        finish(rows, x_ref[rows, :] + gate_ref[...] * moe)

    @pl.when(i == nsteps - 1)
    def _():
        wait_tile(1 - slot)


def _combine_norm_body(pos_ref, x_ref, y_hbm, gw_ref, gate_ref, g_ref, s_ref, xo_ref, hn_ref, ybuf, sem, *, n_tokens):
    def finish(rows, xn):
        xo_ref[rows, :] = xn
        hn_ref[rows, :] = _rms_mod(xn, g_ref[...], s_ref[...]).astype(hn_ref.dtype)
    _combine_rows(pos_ref, x_ref, y_hbm, gw_ref, gate_ref, ybuf, sem, finish, n_tokens=n_tokens)


def _combine_final_body(pos_ref, x_ref, y_hbm, gw_ref, gate_ref, g_ref, o_ref, ybuf, sem, *, n_tokens):
    def finish(rows, xn):
        ms = jnp.mean(xn * xn, axis=-1, keepdims=True)
        o_ref[rows, :] = xn * lax.rsqrt(ms + EPS) * g_ref[...]
    _combine_rows(pos_ref, x_ref, y_hbm, gw_ref, gate_ref, ybuf, sem, finish, n_tokens=n_tokens)


def _ffn_combine(x, ys, pos_flat, gw_t, gate, norm_args, *, seq, final):
    n, d = x.shape
    tm = min(256, seq)
    per = seq // tm
    tile = pl.BlockSpec((tm, d), lambda i, p: (i, 0))
    vec = pl.BlockSpec((None, 1, d), lambda i, p: (i // per, 0, 0))
    in_specs = [tile, pl.BlockSpec(memory_space=pl.ANY), pl.BlockSpec((tm, TOP_K), lambda i, p: (i, 0)), vec]
    scratch = [pltpu.VMEM((2, TOP_K, tm, ys.shape[1]), ys.dtype), pltpu.SemaphoreType.DMA((2,))]
    if final:
        body, name = _combine_final_body, "ffn_combine_final"
        in_specs.append(pl.BlockSpec((1, d), lambda i, p: (0, 0)))
        out_specs, out_shape = tile, jax.ShapeDtypeStruct((n, d), F32)
    else:
        body, name = _combine_norm_body, "ffn_combine_norm"
        in_specs += [vec, vec]
        out_specs = [tile, tile]
        out_shape = [jax.ShapeDtypeStruct((n, d), F32), jax.ShapeDtypeStruct((n, d), BF16)]
    grid_spec = pltpu.PrefetchScalarGridSpec(num_scalar_prefetch=1, grid=(n // tm,), in_specs=in_specs,
                                             out_specs=out_specs, scratch_shapes=scratch)
    return pl.pallas_call(
        functools.partial(body, n_tokens=n), grid_spec=grid_spec, out_shape=out_shape,
        compiler_params=_params(("arbitrary",)), name=name,
    )(pos_flat, x, ys, gw_t, gate, *norm_args)


def _moe_layer(x, geff, shift, rwt, rb, w1, w3, w2, *, seq):
    n, _ = x.shape
    n_e = rwt.shape[1]
    hn, e, gw, rank, cnt = _ffn_prep(x, geff, shift, rwt, rb, seq=seq)
    n_tiles_max = (TOP_K * n) // MOE_TILE + n_e
    pos, tile_expert, tile_rows, fill = _moe_plan(e, rank, cnt, n_tiles_max)
    pos_flat = pos.reshape(-1)
    xs = _moe_dispatch(hn, pos_flat, fill, rows=n_tiles_max * MOE_TILE)
    ys = _moe_experts(xs, w1, w3, w2, tile_expert, tile_rows)
    return ys, pos_flat, gw.T


def kernel(x, c, norm_mix_g, norm_ffn_g, norm_final_g, ada_mix_w, ada_mix_b, ada_ffn_w, ada_ffn_b, s5_w_in, s5_lam_re, s5_lam_im, s5_log_step, s5_b_re, s5_b_im, s5_c_re, s5_c_im, s5_d, s5_w_glu, s5_b_glu, s5_w_out, gm_w_in, gm_b_in, gm_ln_g, gm_ln_b, gm_ws, gm_bs, gm_w_out, router_w, router_bias, moe_w1, moe_w3, moe_w2):
    bsz, seq, d = x.shape
    n = bsz * seq
    depth = norm_mix_g.shape[0]
    assert depth == 2 and seq % (CHUNK * SEGMENTS) == 0
    cb = seq // CHUNK
    nc = n // CHUNK

    mods_mix = _adaln(c, ada_mix_w, ada_mix_b)
    mods_ffn = _adaln(c, ada_ffn_w, ada_ffn_b)

    def split(m, g):
        shift, scale, gate = jnp.split(m, 3, axis=-1)
        return (g[None] * (1.0 + scale))[:, None], shift[:, None], gate[:, None]

    rw_hi = router_w.T.astype(BF16)
    rwt = jnp.stack([rw_hi, (router_w.T - rw_hi.astype(F32)).astype(BF16)])
    rb = router_bias.reshape(-1, 1)
    xf = x.reshape(n, d)

    geff, shift, gate = split(mods_mix[0], norm_mix_g[0])
    x3 = xf.reshape(nc, CHUNK, d)
    hn8 = _norm_to_planes(x3, geff, shift, chunks_per_batch=cb)
    u8, w1_l0 = _matmul(hn8, s5_w_in[0].astype(BF16), tm=1024, tn=1024, out_dtype=BF16, epilogue=_ep_identity,
                        side=[(moe_w1, 0)], name="s5_in")
    wa, wc, a1, aseg = _s5_weights(s5_lam_re[0], s5_lam_im[0], s5_log_step[0], s5_b_re[0], s5_b_im[0],
                                   s5_c_re[0], s5_c_im[0], seg_rows=cb // SEGMENTS)
    y8 = _s5_ssm(u8, wa, wc, a1, aseg, s5_d[0].reshape(1, -1), chunks_per_batch=cb,
                 group=s5_b_re.shape[-1], state=s5_b_re.shape[-2])
    w = y8.shape[-1]
    tmg, tng = min(1024, nc), min(512, w)
    z8, w3_l0 = _matmul(y8, s5_w_glu[0].astype(BF16), tm=tmg, tn=tng, out_dtype=BF16, epilogue=_ep_glu,
                        extras=[(y8, pl.BlockSpec((None, tmg, tng), lambda q, i, j: (q, i, j))),
                                (s5_b_glu[0].reshape(1, -1), pl.BlockSpec((1, tng), lambda q, i, j: (0, j)))],
                        side=[(moe_w3, 0)], name="s5_glu")
    x1, w2_l0 = _s5_out(z8, s5_w_out[0].astype(BF16), x3, gate, (moe_w2, 0), chunks_per_batch=cb)
    x1 = x1.reshape(n, d)

    geff, shift, gate = split(mods_ffn[0], norm_ffn_g[0])
    ys, pos, gw_t = _moe_layer(x1, geff, shift, rwt, rb, w1_l0, w3_l0, w2_l0, seq=seq)
    geff1, shift1, gate1 = split(mods_mix[1], norm_mix_g[1])
    x2, hn = _ffn_combine(x1, ys, pos, gw_t, gate, (geff1, shift1), seq=seq, final=False)

    w2n = gm_w_in.shape[-1]
    tm_in, tn_in = min(1024, n), min(1024, w2n)
    zz, w1_l1 = _matmul(
        hn[None], gm_w_in[0].astype(BF16), tm=tm_in, tn=tn_in, out_dtype=BF16, epilogue=_ep_bias_gelu,
        extras=[(gm_b_in[0].reshape(1, -1), pl.BlockSpec((1, tn_in), lambda q, i, j: (0, j)))],
        side=[(moe_w1, 1)], name="gmlp_in")
    zz = zz[0]
    chunk = gm_ws.shape[-1]
    gated = _gm_gate(zz, gm_ln_g[0].reshape(1, -1), gm_ln_b[0].reshape(1, -1), gm_ws[0].astype(BF16),
                     gm_bs[0].T, chunk=chunk)
    tm_o, tn_o = min(1024, seq), min(512, d)
    per = seq // tm_o
    x3_, w3_l1, w2_l1 = _matmul(gated[None], gm_w_out[0].astype(BF16), tm=tm_o, tn=tn_o, out_dtype=F32,
                                epilogue=_ep_residual,
                                extras=[(x2[None], pl.BlockSpec((None, tm_o, tn_o), lambda q, i, j: (q, i, j))),
                                        (gate1, pl.BlockSpec((None, 1, tn_o), lambda q, i, j: (i // per, 0, j)))],
                                side=[(moe_w3, 1), (moe_w2, 1)], name="gmlp_out")
    x3_ = x3_[0]

    geff, shift, gate = split(mods_ffn[1], norm_ffn_g[1])
    ys, pos, gw_t = _moe_layer(x3_, geff, shift, rwt, rb, w1_l1, w3_l1, w2_l1, seq=seq)
    out = _ffn_combine(x3_, ys, pos, gw_t, gate, (norm_final_g.reshape(1, -1),), seq=seq, final=True)
    return out.reshape(bsz, seq, d)
```

```python
import functools

import jax
import jax.numpy as jnp
from jax import lax
from jax.experimental import pallas as pl
from jax.experimental.pallas import tpu as pltpu

EPS = 1e-6
CHUNK = 8
LANES = 128
SEGMENTS = 8
N_GROUPS_MOE = 4
TOP_K = 2
MOE_TILE = 256
VMEM_LIMIT = 56 * 1024 * 1024

F32 = jnp.float32
BF16 = jnp.bfloat16


def _params(sem):
    return pltpu.CompilerParams(dimension_semantics=sem, vmem_limit_bytes=VMEM_LIMIT)


def _rms_mod(x, geff, shift):
    ms = jnp.mean(x * x, axis=-1, keepdims=True)
    return x * lax.rsqrt(ms + EPS) * geff + shift


def _ada_body(ct_ref, w_ref, b_ref, o_ref):
    ct = ct_ref[...]
    st = ct * jax.nn.sigmoid(ct)
    w = w_ref[...]
    rows = [jnp.sum(st[:, b:b + 1] * w, axis=0, keepdims=True) for b in range(ct.shape[1])]
    o_ref[...] = jnp.concatenate(rows, axis=0) + b_ref[...]


def _adaln(c, w, b):
    depth, d, d3 = w.shape
    bsz = c.shape[0]
    tn = 1024 if d3 % 1024 == 0 else 512
    assert d3 % tn == 0
    return pl.pallas_call(
        _ada_body,
        grid=(depth, d3 // tn),
        in_specs=[
            pl.BlockSpec((d, bsz), lambda l, j: (0, 0)),
            pl.BlockSpec((None, d, tn), lambda l, j: (l, 0, j)),
            pl.BlockSpec((None, 1, tn), lambda l, j: (l, 0, j)),
        ],
        out_specs=pl.BlockSpec((None, bsz, tn), lambda l, j: (l, 0, j)),
        out_shape=jax.ShapeDtypeStruct((depth, bsz, d3), F32),
        compiler_params=_params(("arbitrary", "arbitrary")),
        name="adaln",
    )(c.T, w, b.reshape(depth, 1, d3))


def _side_cast_specs(src, layer, n_steps, step_of):
    _, n_e, r, c = src.shape
    per_step = (n_e * r) // n_steps
    assert per_step * n_steps == n_e * r and per_step % 16 == 0
    eb, rows = max(1, per_step // r), min(per_step, r)
    assert eb * rows == per_step and r % rows == 0 and n_e % eb == 0
    nb = r // rows
    in_spec = pl.BlockSpec((None, eb, rows, c), lambda *g: (layer, step_of(*g) // nb, step_of(*g) % nb, 0))
    out_spec = pl.BlockSpec((eb, rows, c), lambda *g: (step_of(*g) // nb, step_of(*g) % nb, 0))
    return in_spec, out_spec, jax.ShapeDtypeStruct((n_e, r, c), BF16)


def _mm_body(*refs, epilogue, n_extra, n_side):
    a_ref, w_ref = refs[0], refs[1]
    extra = refs[2:2 + n_extra]
    side_in = refs[2 + n_extra:2 + n_extra + n_side]
    o_ref = refs[2 + n_extra + n_side]
    side_out = refs[3 + n_extra + n_side:]
    acc = jnp.dot(a_ref[...], w_ref[...], preferred_element_type=F32)
    o_ref[...] = epilogue(acc, *extra).astype(o_ref.dtype)
    for s_in, s_out in zip(side_in, side_out):
        s_out[...] = s_in[...].astype(s_out.dtype)


def _matmul(a, w, *, tm, tn, out_dtype, epilogue, extras=(), side=(), name):
    p, m, k = a.shape
    n = w.shape[1]
    tm, tn = min(tm, m), min(tn, n)
    ni, nj = m // tm, n // tn
    side_specs = [_side_cast_specs(src, layer, p * ni * nj, lambda q, i, j: (q * ni + i) * nj + j)
                  for src, layer in side]
    in_specs = [
        pl.BlockSpec((None, tm, k), lambda q, i, j: (q, i, 0)),
        pl.BlockSpec((k, tn), lambda q, i, j: (0, j)),
    ] + [s for _, s in extras] + [s[0] for s in side_specs]
    out = pl.pallas_call(
        functools.partial(_mm_body, epilogue=epilogue, n_extra=len(extras), n_side=len(side)),
        grid=(p, ni, nj),
        in_specs=in_specs,
        out_specs=[pl.BlockSpec((None, tm, tn), lambda q, i, j: (q, i, j))] + [s[1] for s in side_specs],
        out_shape=[jax.ShapeDtypeStruct((p, m, n), out_dtype)] + [s[2] for s in side_specs],
        compiler_params=_params(("arbitrary", "arbitrary", "arbitrary")),
        name=name,
    )(a, w, *[x for x, _ in extras], *[src for src, _ in side])
    return out if side else out[0]


def _ep_identity(acc):
    return acc


def _ep_glu(acc, y_ref, b_ref):
    return y_ref[...].astype(F32) * jax.nn.sigmoid(acc + b_ref[...])


def _ep_bias_gelu(acc, b_ref):
    return jax.nn.gelu(acc + b_ref[...])


def _ep_residual(acc, x_ref, g_ref):
    return x_ref[...] + g_ref[...] * acc


def _plane_rows(block, seg, *, cb, kb):
    per = cb // (kb * SEGMENTS)
    return pl.ds((block // per) * cb + seg * (cb // SEGMENTS) + (block % per) * kb, kb)


def _norm_plane_body(x_hbm, g_ref, s_ref, o_ref, buf, sem, *, cb):
    i = pl.program_id(0)
    nsteps = pl.num_programs(0)
    kb = buf.shape[2]
    slot = i % 2

    def fetch(step, sl, t, s):
        return pltpu.make_async_copy(x_hbm.at[_plane_rows(step, s, cb=cb, kb=kb), t], buf.at[sl, t, :, s], sem.at[sl, t])

    def start_all(step, sl):
        for t in range(CHUNK):
            for s in range(SEGMENTS):
                fetch(step, sl, t, s).start()

    @pl.when(i == 0)
    def _():
        start_all(0, 0)

    @pl.when(i + 1 < nsteps)
    def _():
        start_all(i + 1, 1 - slot)

    for t in range(CHUNK):
        for s in range(SEGMENTS):
            fetch(i, slot, t, s).wait()
        xt = buf[slot, t].reshape(kb * SEGMENTS, buf.shape[-1])
        o_ref[t] = _rms_mod(xt, g_ref[...], s_ref[...]).astype(o_ref.dtype)


def _norm_to_planes(x3, geff, shift, *, chunks_per_batch):
    nc, _, d = x3.shape
    kb = 4
    tc = kb * SEGMENTS
    per = chunks_per_batch // tc
    vec = pl.BlockSpec((None, 1, d), lambda i: (i // per, 0, 0))
    return pl.pallas_call(
        functools.partial(_norm_plane_body, cb=chunks_per_batch),
        grid=(nc // tc,),
        in_specs=[pl.BlockSpec(memory_space=pl.ANY), vec, vec],
        out_specs=pl.BlockSpec((CHUNK, tc, d), lambda i: (0, i, 0)),
        out_shape=jax.ShapeDtypeStruct((CHUNK, nc, d), BF16),
        scratch_shapes=[pltpu.VMEM((2, CHUNK, kb, SEGMENTS, d), F32), pltpu.SemaphoreType.DMA((2, CHUNK))],
        compiler_params=_params(("arbitrary",)),
        name="s5_norm_planes",
    )(x3, geff, shift)


def _cmul_add(ar, ai, xr, xi, vr, vi):
    return ar * xr - ai * xi + vr, ar * xi + ai * xr + vi


def _ssm_body(u_ref, wac_ref, wcc_ref, a1_ref, aseg_ref, d_ref, y_ref, xf_ref, xb_ref, wa_ref, wc_ref, *,
              rows_per_dot, gl, hsz, psz):
    nb, cb = xf_ref.shape[0], xf_ref.shape[1]
    ks = cb // SEGMENTS
    half = xf_ref.shape[-1]
    q = half // 2

    if True:
        iota = lambda shape, ax: lax.broadcasted_iota(jnp.int32, shape, ax)
        ca, na = wac_ref.shape[1], wa_ref.shape[1]
        dcol = iota((ca, na), 1)
        spread = (iota((ca, na), 0) == (dcol // (gl * psz)) * psz + dcol % psz).astype(BF16)
        rep = jnp.dot(wac_ref[...], spread, preferred_element_type=F32)
        shp = rep.shape
        keep = (iota(shp, 0) // hsz) % gl == (iota(shp, 1) // psz) % gl
        wa_ref[...] = jnp.where(keep, rep, 0.0).astype(wa_ref.dtype)
        cc, nc_ = wcc_ref.shape[1], wc_ref.shape[1]
        dcol = iota((cc, nc_), 1)
        spread = (iota((cc, nc_), 0) == (dcol // (gl * hsz)) * hsz + dcol % hsz).astype(BF16)
        rep = jnp.dot(wcc_ref[...], spread, preferred_element_type=F32)
        shp = rep.shape
        n_tok_rows = CHUNK * gl * hsz
        r = iota(shp, 0)
        row_g = jnp.where(r < n_tok_rows, (r // hsz) % gl, ((r - n_tok_rows) // psz) % gl)
        keep = row_g == (iota(shp, 1) // hsz) % gl
        wc_ref[...] = jnp.where(keep, rep, 0.0).astype(wc_ref.dtype)

    def u_rows(r0, nr):
        return jnp.concatenate([u_ref[t, pl.ds(r0, nr), :] for t in range(CHUNK)], axis=1)

    nr = min(cb, rows_per_dot)
    afr, afi, abr, abi = (a1_ref[i:i + 1, :] for i in range(4))
    sfr, sfi, sbr, sbi = (aseg_ref[i:i + 1, :] for i in range(4))
    group = lambda j: slice(j * SEGMENTS, (j + 1) * SEGMENTS)
    lw = d_ref.shape[-1]

    def phase_a(b):
        for r0 in range(0, cb, nr):
            res = jnp.dot(u_rows(b * cb + r0, nr), wa_ref[...], preferred_element_type=F32)
            xf_ref[b, pl.ds(r0, nr), :] = res[:, :half]
            xb_ref[b, pl.ds(r0, nr), :] = res[:, half:]

    def phase_b(b):
        z = jnp.zeros((SEGMENTS, q), F32)
        cfr, cfi, cbr, cbi = z, z, z, z
        for j in range(ks):
            jb = ks - 1 - j
            vf = xf_ref[b, group(j), :]
            vb = xb_ref[b, group(jb), :]
            xf_ref[b, group(j), :] = jnp.concatenate([cfr, cfi], axis=1)
            xb_ref[b, group(jb), :] = jnp.concatenate([cbr, cbi], axis=1)
            cfr, cfi = _cmul_add(afr, afi, cfr, cfi, vf[:, :q], vf[:, q:])
            cbr, cbi = _cmul_add(abr, abi, cbr, cbi, vb[:, :q], vb[:, q:])
        efr, efi, ebr, ebi = cfr, cfi, cbr, cbi

        z1 = jnp.zeros((1, q), F32)
        gfr, gfi = [z1], [z1]
        for s in range(SEGMENTS - 1):
            r, i = _cmul_add(sfr, sfi, gfr[-1], gfi[-1], efr[s:s + 1], efi[s:s + 1])
            gfr.append(r)
            gfi.append(i)
        gbr, gbi = [z1], [z1]
        for s in range(SEGMENTS - 1, 0, -1):
            r, i = _cmul_add(sbr, sbi, gbr[0], gbi[0], ebr[s:s + 1], ebi[s:s + 1])
            gbr.insert(0, r)
            gbi.insert(0, i)
        gfr, gfi, gbr, gbi = (jnp.concatenate(g, axis=0) for g in (gfr, gfi, gbr, gbi))

        pfr = pbr = jnp.ones((1, q), F32)
        pfi = pbi = jnp.zeros((1, q), F32)
        for j in range(ks):
            jb = ks - 1 - j
            cf = xf_ref[b, group(j), :]
            cb_ = xb_ref[b, group(jb), :]
            xf_ref[b, group(j), :] = jnp.concatenate(
                [cf[:, :q] + pfr * gfr - pfi * gfi, cf[:, q:] + pfr * gfi + pfi * gfr], axis=1)
            xb_ref[b, group(jb), :] = jnp.concatenate(
                [cb_[:, :q] + pbr * gbr - pbi * gbi, cb_[:, q:] + pbr * gbi + pbi * gbr], axis=1)
            pfr, pfi = pfr * afr - pfi * afi, pfr * afi + pfi * afr
            pbr, pbi = pbr * abr - pbi * abi, pbr * abi + pbi * abr

    def phase_c(b):
        for r0 in range(0, cb, nr):
            rows = pl.ds(b * cb + r0, nr)
            xf = xf_ref[b, pl.ds(r0, nr), :]
            xb = xb_ref[b, pl.ds(r0, nr), :]
            lhs = jnp.concatenate([u_rows(b * cb + r0, nr), xf.astype(BF16), xb.astype(BF16)], axis=1)
            acc = jnp.dot(lhs, wc_ref[...], preferred_element_type=F32)
            for t in range(CHUNK):
                yt = acc[:, t * lw:(t + 1) * lw] + d_ref[...] * u_ref[t, rows, :].astype(F32)
                y_ref[t, rows, :] = jax.nn.gelu(yt).astype(y_ref.dtype)

    for b in range(nb):
        phase_a(b)
    for b in range(nb):
        phase_b(b)
    for b in range(nb):
        phase_c(b)


def _s5_ssm(u8, wa, wc, a1, aseg, dvec, *, chunks_per_batch, group, state):
    _, nc, w = u8.shape
    nslab = w // LANES
    nb = nc // chunks_per_batch
    cb = chunks_per_batch
    ks = cb // SEGMENTS
    gl = LANES // group
    half = 2 * gl * state
    tok = CHUNK * LANES
    body = functools.partial(_ssm_body, rows_per_dot=512, gl=gl, hsz=group, psz=state)
    return pl.pallas_call(
        body,
        grid=(nslab,),
        in_specs=[
            pl.BlockSpec((CHUNK, nc, LANES), lambda s: (0, 0, s)),
            pl.BlockSpec((None,) + wa.shape[1:], lambda s: (s, 0, 0)),
            pl.BlockSpec((None,) + wc.shape[1:], lambda s: (s, 0, 0)),
            pl.BlockSpec((None,) + a1.shape[1:], lambda s: (s, 0, 0)),
            pl.BlockSpec((None,) + aseg.shape[1:], lambda s: (s, 0, 0)),
            pl.BlockSpec((1, LANES), lambda s: (0, s)),
        ],
        out_specs=pl.BlockSpec((CHUNK, nc, LANES), lambda s: (0, 0, s)),
        out_shape=jax.ShapeDtypeStruct(u8.shape, BF16),
        scratch_shapes=[
            pltpu.VMEM((nb, cb, half), F32),
            pltpu.VMEM((nb, cb, half), F32),
            pltpu.VMEM((tok, 2 * half), BF16),
            pltpu.VMEM((tok + 2 * half, tok), BF16),
        ],
        compiler_params=_params(("arbitrary",)),
        name="s5_ssm",
    )(u8, wa, wc, a1, aseg, dvec)


def _s5_weights(lam_re, lam_im, log_step, b_re, b_im, c_re, c_im, *, seg_rows):
    _, g, p = lam_re.shape
    h = b_re.shape[-1]
    gl = LANES // h
    s = g // gl
    t = CHUNK
    lam = lax.complex(lam_re.astype(F32), lam_im.astype(F32))
    zed = lam * jnp.exp(log_step.astype(F32))[..., None]
    lam_bar = jnp.exp(zed)
    bbar = ((lam_bar - 1.0) / lam)[..., None] * lax.complex(b_re.astype(F32), b_im.astype(F32))
    cmat = lax.complex(c_re.astype(F32), c_im.astype(F32))
    pw = jnp.exp(zed[:, None] * jnp.arange(t + 1, dtype=F32)[None, :, None, None])
    pw_down = jnp.exp(zed[:, None] * (t - jnp.arange(t + 1, dtype=F32))[None, :, None, None])

    ma_f = pw_down[0, 1:, :, :, None] * bbar[0][None]
    ma_b = pw[1, :t, :, :, None] * bbar[1][None]
    wa = jnp.stack([ma_f.real, ma_f.imag, ma_b.real, ma_b.imag], axis=0)
    wa = wa.reshape(4, t, s, gl, p, h).transpose(2, 1, 3, 5, 0, 4).reshape(s, t * gl * h, 4 * p)

    kf = jnp.einsum('gyp,dgp,gph->dgyh', cmat[0], pw[0, :t], bbar[0]).real
    kb = jnp.einsum('gyp,dgp,gph->dgyh', cmat[1], pw[1, :t], bbar[1]).real
    st = jnp.arange(t)
    lag = st[None, None, :] - st[None, :, None]
    sel_f = (lag == st[:, None, None]).astype(F32)
    sel_b = (-lag == st[:, None, None]).astype(F32)
    coef = jnp.einsum('dat,dgyh->atgyh', sel_f, kf) + jnp.einsum('dat,dgyh->atgyh', sel_b, kb)
    w_u = coef.reshape(t, t, s, gl, h, h).transpose(2, 0, 3, 5, 1, 4).reshape(s, t * gl * h, t * h)

    cl_f = cmat[0][None] * pw[0, 1:][:, :, None, :]
    cl_b = cmat[1][None] * pw_down[1, :t][:, :, None, :]
    def state_rows(x):
        return x.reshape(t, s, gl, h, p).transpose(1, 2, 4, 0, 3).reshape(s, gl * p, t * h)
    wc = jnp.concatenate([w_u, state_rows(cl_f.real), state_rows(-cl_f.imag),
                          state_rows(cl_b.real), state_rows(-cl_b.imag)], axis=1)

    def lanes(x):
        x = x.reshape(2, s, gl * p)
        return jnp.stack([x[0].real, x[0].imag, x[1].real, x[1].imag], axis=1)
    a1 = lanes(pw[:, t])
    aseg = lanes(jnp.exp(zed * float(t * seg_rows)))
    return wa.astype(BF16), wc.astype(BF16), a1, aseg


def _s5_out_body(a_ref, w_ref, g_ref, side_ref, x_hbm, o_hbm, side_out, xin, xout, isem, osem, *, cb):
    side_out[...] = side_ref[...].astype(side_out.dtype)
    nj = pl.num_programs(1)
    step = pl.program_id(0) * nj + pl.program_id(1)
    total = pl.num_programs(0) * nj
    tm, tn = a_ref.shape[1], w_ref.shape[1]
    kb = tm // SEGMENTS
    slot = step % 2

    def window(st, t, s):
        cols = pl.ds(pl.multiple_of((st % nj) * tn, tn), tn)
        return (_plane_rows(st // nj, s, cb=cb, kb=kb), t, cols)

    def fetch(st, sl, t, s):
        return pltpu.make_async_copy(x_hbm.at[window(st, t, s)], xin.at[sl, t, :, s], isem.at[sl, t])

    def put(st, sl, t, s):
        return pltpu.make_async_copy(xout.at[sl, t, :, s], o_hbm.at[window(st, t, s)], osem.at[sl, t])

    def each(fn):
        for t in range(CHUNK):
            for s in range(SEGMENTS):
                fn(t, s)

    @pl.when(step == 0)
    def _():
        each(lambda t, s: fetch(0, 0, t, s).start())

    @pl.when(step + 1 < total)
    def _():
        each(lambda t, s: fetch(step + 1, 1 - slot, t, s).start())

    a = a_ref[...].reshape(CHUNK * tm, a_ref.shape[2])
    acc = jnp.dot(a, w_ref[...], preferred_element_type=F32)

    @pl.when(step >= 2)
    def _():
        each(lambda t, s: put(step - 2, slot, t, s).wait())

    for t in range(CHUNK):
        for s in range(SEGMENTS):
            fetch(step, slot, t, s).wait()
        xt = xin[slot, t].reshape(tm, tn) + g_ref[...] * acc[t * tm:(t + 1) * tm]
        xout[slot, t] = xt.reshape(kb, SEGMENTS, tn)
    each(lambda t, s: put(step, slot, t, s).start())

    @pl.when(step == total - 1)
    def _():
        each(lambda t, s: put(step, slot, t, s).wait())

        @pl.when(total >= 2)
        def _():
            each(lambda t, s: put(step - 1, 1 - slot, t, s).wait())


def _s5_out(z8, w, x3, gate, side, *, chunks_per_batch):
    _, nc, k = z8.shape
    d = w.shape[1]
    tm = min(128, chunks_per_batch)
    tn = min(512, d)
    per = chunks_per_batch // tm
    kb = tm // SEGMENTS
    nj = d // tn
    side_in, side_out, side_shape = _side_cast_specs(side[0], side[1], (nc // tm) * nj, lambda i, j: i * nj + j)
    return pl.pallas_call(
        functools.partial(_s5_out_body, cb=chunks_per_batch),
        grid=(nc // tm, nj),
        in_specs=[
            pl.BlockSpec((CHUNK, tm, k), lambda i, j: (0, i, 0)),
            pl.BlockSpec((k, tn), lambda i, j: (0, j)),
            pl.BlockSpec((None, 1, tn), lambda i, j: (i // per, 0, j)),
            side_in,
            pl.BlockSpec(memory_space=pl.ANY),
        ],
        out_specs=[pl.BlockSpec(memory_space=pl.ANY), side_out],
        out_shape=[jax.ShapeDtypeStruct(x3.shape, F32), side_shape],
        scratch_shapes=[
            pltpu.VMEM((2, CHUNK, kb, SEGMENTS, tn), F32),
            pltpu.VMEM((2, CHUNK, kb, SEGMENTS, tn), F32),
            pltpu.SemaphoreType.DMA((2, CHUNK)),
            pltpu.SemaphoreType.DMA((2, CHUNK)),
        ],
        compiler_params=_params(("arbitrary", "arbitrary")),
        name="s5_out",
    )(z8, w, gate, side[0], x3)


def _gm_gate_body(z_ref, lng_ref, lnb_ref, ws_ref, bs_ref, o_ref, *, chunk):
    tm, w2 = z_ref.shape
    w = w2 // 2
    heads = ws_ref.shape[0]
    hd = w // heads
    v = z_ref[:, w:].astype(F32)
    mu = jnp.mean(v, axis=-1, keepdims=True)
    vc = v - mu
    var = jnp.mean(vc * vc, axis=-1, keepdims=True)
    vn = (vc * lax.rsqrt(var + EPS) * lng_ref[...] + lnb_ref[...]).astype(BF16)
    for c in range(tm // chunk):
        rows = slice(c * chunk, (c + 1) * chunk)
        for hh in range(heads):
            cols = slice(hh * hd, (hh + 1) * hd)
            sv = jnp.dot(ws_ref[hh], vn[rows, cols], preferred_element_type=F32) + bs_ref[:, hh:hh + 1]
            o_ref[rows, cols] = (z_ref[rows, cols].astype(F32) * sv).astype(o_ref.dtype)


def _gm_gate(z, ln_g, ln_b, ws, bs, *, chunk):
    n, w2 = z.shape
    w = w2 // 2
    tm = 2 * chunk
    full = lambda a: pl.BlockSpec(a.shape, lambda i: (0,) * a.ndim)
    return pl.pallas_call(
        functools.partial(_gm_gate_body, chunk=chunk),
        grid=(n // tm,),
        in_specs=[pl.BlockSpec((tm, w2), lambda i: (i, 0)), full(ln_g), full(ln_b), full(ws), full(bs)],
        out_specs=pl.BlockSpec((tm, w), lambda i: (i, 0)),
        out_shape=jax.ShapeDtypeStruct((n, w), BF16),
        compiler_params=_params(("arbitrary",)),
        name="gmlp_gate",
    )(z, ln_g, ln_b, ws, bs)


def _pack_bf16_pairs(x):
    half = x.shape[1] // 2
    bits = lambda v: lax.bitcast_convert_type(v.astype(F32), jnp.int32)
    return bits(x[:, half:]) | lax.shift_right_logical(bits(x[:, :half]), 16)


def _unpack_bf16_pairs(w):
    lo = lax.bitcast_convert_type(lax.shift_left(w, 16), F32)
    hi = lax.bitcast_convert_type(w & jnp.int32(-65536), F32)
    return jnp.concatenate([lo.astype(BF16), hi.astype(BF16)], axis=1)


def _top2_sum(a, b, c, d):
    m1, n1 = jnp.maximum(a, b), jnp.minimum(a, b)
    m2, n2 = jnp.maximum(c, d), jnp.minimum(c, d)
    return jnp.maximum(m1, m2) + jnp.maximum(jnp.minimum(m1, m2), jnp.maximum(n1, n2))


def _first_argmax(vals):
    best, idx = vals[0], jnp.zeros(vals[0].shape, jnp.int32)
    for j in range(1, len(vals)):
        upd = vals[j] > best
        idx = jnp.where(upd, j, idx)
        best = jnp.where(upd, vals[j], best)
    return idx, best


def _ffn_prep_body(x_ref, g_ref, s_ref, rwt_ref, rb_ref, hn_ref, e_ref, gw_ref, rank_ref, cnt_ref, carry_ref):
    i = pl.program_id(0)
    n_e = rwt_ref.shape[1]
    epg = n_e // N_GROUPS_MOE
    tm = x_ref.shape[0]

    @pl.when(i == 0)
    def _():
        carry_ref[...] = jnp.zeros_like(carry_ref)

    hn = _rms_mod(x_ref[...], g_ref[...], s_ref[...])
    hn_hi = hn.astype(BF16)
    hn_lo = (hn - hn_hi.astype(F32)).astype(BF16)
    hn_ref[...] = _pack_bf16_pairs(hn_hi)
    nt_dot = lambda a, b: lax.dot_general(a, b, (((1,), (1,)), ((), ())), preferred_element_type=F32)
    logits = nt_dot(rwt_ref[0], hn_hi) + (nt_dot(rwt_ref[1], hn_hi) + nt_dot(rwt_ref[0], hn_lo))
    score = jax.nn.sigmoid(logits)
    sel = score + rb_ref[...]

    row = lambda a, r: a[r:r + 1, :]
    gscore = [_top2_sum(*[row(sel, g * epg + j) for j in range(epg)]) for g in range(N_GROUPS_MOE)]
    bg, _ = _first_argmax(gscore)
    pick = lambda a, j: functools.reduce(
        lambda acc, g: jnp.where(bg == g, row(a, g * epg + j), acc), range(1, N_GROUPS_MOE), row(a, j))
    sel_in = [pick(sel, j) for j in range(epg)]
    sc_in = [pick(score, j) for j in range(epg)]
    i1, _ = _first_argmax(sel_in)
    neg = jnp.full_like(sel_in[0], -jnp.inf)
    i2, _ = _first_argmax([jnp.where(i1 == j, neg, sel_in[j]) for j in range(epg)])
    take = lambda vals, idx: functools.reduce(lambda acc, j: jnp.where(idx == j, vals[j], acc), range(1, epg), vals[0])
    w1, w2 = take(sc_in, i1), take(sc_in, i2)
    den = w1 + w2
    e1, e2 = bg * epg + i1, bg * epg + i2
    e_ref[...] = jnp.concatenate([e1, e2], axis=0)
    gw_ref[...] = jnp.concatenate([w1 / den, w2 / den], axis=0)

    eid = lax.broadcasted_iota(jnp.int32, (n_e, tm), 0)
    oh1 = (eid == e1).astype(F32)
    oh2 = (eid == e2).astype(F32)
    oh = oh1 + oh2
    tri = (lax.broadcasted_iota(jnp.int32, (tm, tm), 0) <= lax.broadcasted_iota(jnp.int32, (tm, tm), 1)).astype(BF16)
    incl = jnp.dot(oh.astype(BF16), tri, preferred_element_type=F32)
    before = carry_ref[:, 0:1] + incl - oh
    r1 = jnp.sum(oh1 * before, axis=0, keepdims=True)
    r2 = jnp.sum(oh2 * before, axis=0, keepdims=True)
    rank_ref[...] = jnp.concatenate([r1, r2], axis=0).astype(jnp.int32)
    carry_ref[...] = carry_ref[...] + incl[:, tm - 1:tm]
    cnt_ref[...] = carry_ref[...].astype(jnp.int32)


def _ffn_prep(x, geff, shift, rwt, rb, *, seq):
    n, d = x.shape
    n_e = rwt.shape[1]
    tm = min(256, seq)
    per = seq // tm
    vec = pl.BlockSpec((None, 1, d), lambda i: (i // per, 0, 0))
    row2 = pl.BlockSpec((TOP_K, tm), lambda i: (0, i))
    hn, e, gw, rank, cnt = pl.pallas_call(
        _ffn_prep_body,
        grid=(n // tm,),
        in_specs=[pl.BlockSpec((tm, d), lambda i: (i, 0)), vec, vec,
                  pl.BlockSpec((2, n_e, d), lambda i: (0, 0, 0)), pl.BlockSpec((n_e, 1), lambda i: (0, 0))],
        out_specs=[pl.BlockSpec((tm, d // 2), lambda i: (i, 0)), row2, row2, row2,
                   pl.BlockSpec((n_e, LANES), lambda i: (0, 0))],
        out_shape=[jax.ShapeDtypeStruct((n, d // 2), jnp.int32), jax.ShapeDtypeStruct((TOP_K, n), jnp.int32),
                   jax.ShapeDtypeStruct((TOP_K, n), F32), jax.ShapeDtypeStruct((TOP_K, n), jnp.int32),
                   jax.ShapeDtypeStruct((n_e, LANES), jnp.int32)],
        scratch_shapes=[pltpu.VMEM((n_e, LANES), F32)],
        compiler_params=_params(("arbitrary",)),
        name="ffn_prep",
    )(x, geff, shift, rwt, rb)
    return hn, e, gw, rank, cnt[:, 0]


def _for_rows(lo, hi, fn, unroll=1):
    def body(r, c):
        fn(r)
        return c
    lax.fori_loop(lo, hi, body, 0, unroll=unroll)


def _dispatch_body(pos_ref, fill_ref, hn_ref, xs_hbm, zero_ref, sem, zsem, *, n_tokens):
    i = pl.program_id(0)
    tm = hn_ref.shape[0]

    def row_copy(r, k):
        return pltpu.make_async_copy(hn_ref.at[pl.ds(r, 1)], xs_hbm.at[pl.ds(pos_ref[k * n_tokens + i * tm + r], 1)], sem)

    def start(r):
        for k in range(TOP_K):
            row_copy(r, k).start()

    for r in range(tm):
        start(r)

    @pl.when(i == 0)
    def _():
        zero_ref[...] = jnp.zeros_like(zero_ref)
        n_fill = fill_ref.shape[0] // 2

        def zero_copy(row):
            return pltpu.make_async_copy(zero_ref, xs_hbm.at[pl.ds(row, 1)], zsem)

        for e in range(n_fill):
            _for_rows(fill_ref[2 * e], fill_ref[2 * e + 1], lambda row: zero_copy(row).start())
        for e in range(n_fill):
            _for_rows(fill_ref[2 * e], fill_ref[2 * e + 1], lambda row: zero_copy(row).wait())

    for k in range(TOP_K):
        pltpu.make_async_copy(hn_ref, xs_hbm.at[pl.ds(0, tm)], sem).wait()


def _moe_dispatch(hn, pos_flat, fill, *, rows):
    n, dh = hn.shape
    tm = min(512, n)
    grid_spec = pltpu.PrefetchScalarGridSpec(
        num_scalar_prefetch=2,
        grid=(n // tm,),
        in_specs=[pl.BlockSpec((tm, dh), lambda i, p, f: (i, 0))],
        out_specs=pl.BlockSpec(memory_space=pl.ANY),
        scratch_shapes=[pltpu.VMEM((1, dh), jnp.int32), pltpu.SemaphoreType.DMA(()), pltpu.SemaphoreType.DMA(())],
    )
    return pl.pallas_call(
        functools.partial(_dispatch_body, n_tokens=n),
        grid_spec=grid_spec,
        out_shape=jax.ShapeDtypeStruct((rows, dh), jnp.int32),
        compiler_params=_params(("arbitrary",)),
        name="moe_dispatch",
    )(pos_flat, fill, hn)


def _moe_body(te_ref, nrows_ref, x_ref, w1_ref, w3_ref, w2_ref, o_ref):
    i = pl.program_id(0)
    used = nrows_ref[i] > 0

    @pl.when(used)
    def _():
        xb = _unpack_bf16_pairs(x_ref[...])
        h1 = jnp.dot(xb, w1_ref[...], preferred_element_type=F32)
        h3 = jnp.dot(xb, w3_ref[...], preferred_element_type=F32)
        hh = (h1 * jax.nn.sigmoid(h1) * h3).astype(BF16)
        out = jnp.dot(hh, w2_ref[...], preferred_element_type=F32)
        o_ref[...] = _pack_bf16_pairs(out.astype(BF16))

    @pl.when(jnp.logical_not(used))
    def _():
        o_ref[...] = jnp.zeros_like(o_ref)


def _moe_experts(xs, w1, w3, w2, tile_expert, tile_rows):
    rows, dh = xs.shape
    d = 2 * dh
    fdim = w1.shape[-1]
    tm = MOE_TILE
    resident = pl.Buffered(1)
    grid_spec = pltpu.PrefetchScalarGridSpec(
        num_scalar_prefetch=2,
        grid=(rows // tm,),
        in_specs=[
            pl.BlockSpec((tm, dh), lambda i, te, nr: (i, 0)),
            pl.BlockSpec((None, d, fdim), lambda i, te, nr: (te[i], 0, 0)),
            pl.BlockSpec((None, d, fdim), lambda i, te, nr: (te[i], 0, 0)),
            pl.BlockSpec((None, fdim, d), lambda i, te, nr: (te[i], 0, 0), pipeline_mode=resident),
        ],
        out_specs=pl.BlockSpec((tm, dh), lambda i, te, nr: (i, 0)),
    )
    return pl.pallas_call(
        _moe_body,
        grid_spec=grid_spec,
        out_shape=jax.ShapeDtypeStruct((rows, dh), jnp.int32),
        compiler_params=_params(("arbitrary",)),
        name="moe_experts",
    )(tile_expert, tile_rows, xs, w1, w3, w2)


def _moe_plan(e, rank, cnt, n_tiles_max):
    tm = MOE_TILE
    n_e = cnt.shape[0]
    tiles_per = (cnt + tm - 1) // tm
    ends = jnp.cumsum(tiles_per)
    off = (ends - tiles_per) * tm
    onehot = e[..., None] == jnp.arange(n_e, dtype=jnp.int32)
    pos = jnp.sum(jnp.where(onehot, off, 0), axis=-1) + rank
    tile = jnp.arange(n_tiles_max, dtype=jnp.int32)
    tile_expert = jnp.minimum(jnp.sum(tile[:, None] >= ends[None, :], axis=1), n_e - 1).astype(jnp.int32)
    mine = tile_expert[:, None] == jnp.arange(n_e, dtype=jnp.int32)
    tile_rows = jnp.clip(jnp.sum(jnp.where(mine, off + cnt, 0), axis=1) - tile * tm, 0, tm)
    tile_rows = jnp.where(tile < ends[-1], tile_rows, 0).astype(jnp.int32)
    fill_end = jnp.where(jnp.arange(n_e) == n_e - 1, n_tiles_max * tm, ends * tm)
    fill = jnp.stack([off + cnt, fill_end], axis=1).reshape(-1).astype(jnp.int32)
    return pos.astype(jnp.int32), tile_expert, tile_rows, fill


COMBINE_ROW_CHUNKS = 8


def _combine_rows(pos_ref, x_ref, y_hbm, gw_ref, gate_ref, ybuf, sem, finish, *, n_tokens):
    i = pl.program_id(0)
    nsteps = pl.num_programs(0)
    tm = x_ref.shape[0]
    slot = i % 2
    nxt = (i + 1) % nsteps

    def row_copy(step, sl, r, k):
        row = pos_ref[k * n_tokens + step * tm + r]
        return pltpu.make_async_copy(y_hbm.at[pl.ds(row, 1)], ybuf.at[sl, k, pl.ds(r, 1)], sem.at[sl])

    def wait_tile(sl):
        for k in range(TOP_K):
            pltpu.make_async_copy(y_hbm.at[pl.ds(0, tm)], ybuf.at[sl, k], sem.at[sl]).wait()

    @pl.when(i == 0)
    def _():
        def one(r):
            for k in range(TOP_K):
                row_copy(0, 0, r, k).start()
        _for_rows(0, tm, one)

    wait_tile(slot)
    rc = tm // COMBINE_ROW_CHUNKS
    for c in range(COMBINE_ROW_CHUNKS):
        for r in range(c * rc, (c + 1) * rc):
            for k in range(TOP_K):
                row_copy(nxt, 1 - slot, r, k).start()
        rows = pl.ds(c * rc, rc)
        gw = gw_ref[rows, :]
        y0, y1 = ybuf[slot, 0, rows, :], ybuf[slot, 1, rows, :]
        halves = []
        for part in (lambda w: lax.shift_left(w, 16), lambda w: w & jnp.int32(-65536)):
            f = lambda w: lax.bitcast_convert_type(part(w), F32)
            halves.append(gw[:, 0:1] * f(y0) + gw[:, 1:2] * f(y1))
        moe = jnp.concatenate(halves, axis=1)
        finish(rows, x_ref[rows, :] + gate_ref[...] * moe)

    @pl.when(i == nsteps - 1)
    def _():
        wait_tile(1 - slot)


def _combine_norm_body(pos_ref, x_ref, y_hbm, gw_ref, gate_ref, g_ref, s_ref, xo_ref, hn_ref, ybuf, sem, *, n_tokens):
    def finish(rows, xn):
        xo_ref[rows, :] = xn
        hn_ref[rows, :] = _rms_mod(xn, g_ref[...], s_ref[...]).astype(hn_ref.dtype)
    _combine_rows(pos_ref, x_ref, y_hbm, gw_ref, gate_ref, ybuf, sem, finish, n_tokens=n_tokens)


def _combine_final_body(pos_ref, x_ref, y_hbm, gw_ref, gate_ref, g_ref, o_ref, ybuf, sem, *, n_tokens):
    def finish(rows, xn):
        ms = jnp.mean(xn * xn, axis=-1, keepdims=True)
        o_ref[rows, :] = xn * lax.rsqrt(ms + EPS) * g_ref[...]
    _combine_rows(pos_ref, x_ref, y_hbm, gw_ref, gate_ref, ybuf, sem, finish, n_tokens=n_tokens)


def _ffn_combine(x, ys, pos_flat, gw_t, gate, norm_args, *, seq, final):
    n, d = x.shape
    tm = min(256, seq)
    per = seq // tm
    tile = pl.BlockSpec((tm, d), lambda i, p: (i, 0))
    vec = pl.BlockSpec((None, 1, d), lambda i, p: (i // per, 0, 0))
    in_specs = [tile, pl.BlockSpec(memory_space=pl.ANY), pl.BlockSpec((tm, TOP_K), lambda i, p: (i, 0)), vec]
    scratch = [pltpu.VMEM((2, TOP_K, tm, ys.shape[1]), ys.dtype), pltpu.SemaphoreType.DMA((2,))]
    if final:
        body, name = _combine_final_body, "ffn_combine_final"
        in_specs.append(pl.BlockSpec((1, d), lambda i, p: (0, 0)))
        out_specs, out_shape = tile, jax.ShapeDtypeStruct((n, d), F32)
    else:
        body, name = _combine_norm_body, "ffn_combine_norm"
        in_specs += [vec, vec]
        out_specs = [tile, tile]
        out_shape = [jax.ShapeDtypeStruct((n, d), F32), jax.ShapeDtypeStruct((n, d), BF16)]
    grid_spec = pltpu.PrefetchScalarGridSpec(num_scalar_prefetch=1, grid=(n // tm,), in_specs=in_specs,
                                             out_specs=out_specs, scratch_shapes=scratch)
    return pl.pallas_call(
        functools.partial(body, n_tokens=n), grid_spec=grid_spec, out_shape=out_shape,
        compiler_params=_params(("arbitrary",)), name=name,
    )(pos_flat, x, ys, gw_t, gate, *norm_args)


def _moe_layer(x, geff, shift, rwt, rb, w1, w3, w2, *, seq):
    n, _ = x.shape
    n_e = rwt.shape[1]
    hn, e, gw, rank, cnt = _ffn_prep(x, geff, shift, rwt, rb, seq=seq)
    n_tiles_max = (TOP_K * n) // MOE_TILE + n_e
    pos, tile_expert, tile_rows, fill = _moe_plan(e, rank, cnt, n_tiles_max)
    pos_flat = pos.reshape(-1)
    xs = _moe_dispatch(hn, pos_flat, fill, rows=n_tiles_max * MOE_TILE)
    ys = _moe_experts(xs, w1, w3, w2, tile_expert, tile_rows)
    return ys, pos_flat, gw.T


def kernel(x, c, norm_mix_g, norm_ffn_g, norm_final_g, ada_mix_w, ada_mix_b, ada_ffn_w, ada_ffn_b, s5_w_in, s5_lam_re, s5_lam_im, s5_log_step, s5_b_re, s5_b_im, s5_c_re, s5_c_im, s5_d, s5_w_glu, s5_b_glu, s5_w_out, gm_w_in, gm_b_in, gm_ln_g, gm_ln_b, gm_ws, gm_bs, gm_w_out, router_w, router_bias, moe_w1, moe_w3, moe_w2):
    bsz, seq, d = x.shape
    n = bsz * seq
    depth = norm_mix_g.shape[0]
    assert depth == 2 and seq % (CHUNK * SEGMENTS) == 0
    cb = seq // CHUNK
    nc = n // CHUNK

    mods_mix = _adaln(c, ada_mix_w, ada_mix_b)
    mods_ffn = _adaln(c, ada_ffn_w, ada_ffn_b)

    def split(m, g):
        shift, scale, gate = jnp.split(m, 3, axis=-1)
        return (g[None] * (1.0 + scale))[:, None], shift[:, None], gate[:, None]

    rw_hi = router_w.T.astype(BF16)
    rwt = jnp.stack([rw_hi, (router_w.T - rw_hi.astype(F32)).astype(BF16)])
    rb = router_bias.reshape(-1, 1)
    xf = x.reshape(n, d)

    geff, shift, gate = split(mods_mix[0], norm_mix_g[0])
    x3 = xf.reshape(nc, CHUNK, d)
    hn8 = _norm_to_planes(x3, geff, shift, chunks_per_batch=cb)
    u8, w1_l0 = _matmul(hn8, s5_w_in[0].astype(BF16), tm=1024, tn=1024, out_dtype=BF16, epilogue=_ep_identity,
                        side=[(moe_w1, 0)], name="s5_in")
    wa, wc, a1, aseg = _s5_weights(s5_lam_re[0], s5_lam_im[0], s5_log_step[0], s5_b_re[0], s5_b_im[0],
                                   s5_c_re[0], s5_c_im[0], seg_rows=cb // SEGMENTS)
    y8 = _s5_ssm(u8, wa, wc, a1, aseg, s5_d[0].reshape(1, -1), chunks_per_batch=cb,
                 group=s5_b_re.shape[-1], state=s5_b_re.shape[-2])
    w = y8.shape[-1]
    tmg, tng = min(1024, nc), min(512, w)
    z8, w3_l0 = _matmul(y8, s5_w_glu[0].astype(BF16), tm=tmg, tn=tng, out_dtype=BF16, epilogue=_ep_glu,
                        extras=[(y8, pl.BlockSpec((None, tmg, tng), lambda q, i, j: (q, i, j))),
                                (s5_b_glu[0].reshape(1, -1), pl.BlockSpec((1, tng), lambda q, i, j: (0, j)))],
                        side=[(moe_w3, 0)], name="s5_glu")
    x1, w2_l0 = _s5_out(z8, s5_w_out[0].astype(BF16), x3, gate, (moe_w2, 0), chunks_per_batch=cb)
    x1 = x1.reshape(n, d)

    geff, shift, gate = split(mods_ffn[0], norm_ffn_g[0])
    ys, pos, gw_t = _moe_layer(x1, geff, shift, rwt, rb, w1_l0, w3_l0, w2_l0, seq=seq)
    geff1, shift1, gate1 = split(mods_mix[1], norm_mix_g[1])
    x2, hn = _ffn_combine(x1, ys, pos, gw_t, gate, (geff1, shift1), seq=seq, final=False)

    w2n = gm_w_in.shape[-1]
    tm_in, tn_in = min(1024, n), min(1024, w2n)
    zz, w1_l1 = _matmul(
        hn[None], gm_w_in[0].astype(BF16), tm=tm_in, tn=tn_in, out_dtype=BF16, epilogue=_ep_bias_gelu,
        extras=[(gm_b_in[0].reshape(1, -1), pl.BlockSpec((1, tn_in), lambda q, i, j: (0, j)))],
        side=[(moe_w1, 1)], name="gmlp_in")
    zz = zz[0]
    chunk = gm_ws.shape[-1]
    gated = _gm_gate(zz, gm_ln_g[0].reshape(1, -1), gm_ln_b[0].reshape(1, -1), gm_ws[0].astype(BF16),
                     gm_bs[0].T, chunk=chunk)
    tm_o, tn_o = min(1024, seq), min(512, d)
    per = seq // tm_o
    x3_, w3_l1, w2_l1 = _matmul(gated[None], gm_w_out[0].astype(BF16), tm=tm_o, tn=tn_o, out_dtype=F32,
                                epilogue=_ep_residual,
                                extras=[(x2[None], pl.BlockSpec((None, tm_o, tn_o), lambda q, i, j: (q, i, j))),
                                        (gate1, pl.BlockSpec((None, 1, tn_o), lambda q, i, j: (i // per, 0, j)))],
                                side=[(moe_w3, 1), (moe_w2, 1)], name="gmlp_out")
    x3_ = x3_[0]

    geff, shift, gate = split(mods_ffn[1], norm_ffn_g[1])
    ys, pos, gw_t = _moe_layer(x3_, geff, shift, rwt, rb, w1_l1, w3_l1, w2_l1, seq=seq)
    out = _ffn_combine(x3_, ys, pos, gw_t, gate, (norm_final_g.reshape(1, -1),), seq=seq, final=True)
    return out.reshape(bsz, seq, d)
```

```python
import functools

import jax
import jax.numpy as jnp
from jax import lax
from jax.experimental import pallas as pl
from jax.experimental.pallas import tpu as pltpu

EPS = 1e-6
CHUNK = 8
LANES = 128
SEGMENTS = 8
N_GROUPS_MOE = 4
TOP_K = 2
MOE_TILE = 256
VMEM_LIMIT = 56 * 1024 * 1024

F32 = jnp.float32
BF16 = jnp.bfloat16


def _params(sem):
    return pltpu.CompilerParams(dimension_semantics=sem, vmem_limit_bytes=VMEM_LIMIT)


def _rms_mod(x, geff, shift):
    ms = jnp.mean(x * x, axis=-1, keepdims=True)
    return x * lax.rsqrt(ms + EPS) * geff + shift


def _ada_body(ct_ref, w_ref, b_ref, o_ref):
    ct = ct_ref[...]
    st = ct * jax.nn.sigmoid(ct)
    w = w_ref[...]
    rows = [jnp.sum(st[:, b:b + 1] * w, axis=0, keepdims=True) for b in range(ct.shape[1])]
    o_ref[...] = jnp.concatenate(rows, axis=0) + b_ref[...]


def _adaln(c, w, b):
    depth, d, d3 = w.shape
    bsz = c.shape[0]
    tn = 1024 if d3 % 1024 == 0 else 512
    assert d3 % tn == 0
    return pl.pallas_call(
        _ada_body,
        grid=(depth, d3 // tn),
        in_specs=[
            pl.BlockSpec((d, bsz), lambda l, j: (0, 0)),
            pl.BlockSpec((None, d, tn), lambda l, j: (l, 0, j)),
            pl.BlockSpec((None, 1, tn), lambda l, j: (l, 0, j)),
        ],
        out_specs=pl.BlockSpec((None, bsz, tn), lambda l, j: (l, 0, j)),
        out_shape=jax.ShapeDtypeStruct((depth, bsz, d3), F32),
        compiler_params=_params(("arbitrary", "arbitrary")),
        name="adaln",
    )(c.T, w, b.reshape(depth, 1, d3))


def _side_cast_specs(src, layer, n_steps, step_of):
    _, n_e, r, c = src.shape
    per_step = (n_e * r) // n_steps
    assert per_step * n_steps == n_e * r and per_step % 16 == 0
    eb, rows = max(1, per_step // r), min(per_step, r)
    assert eb * rows == per_step and r % rows == 0 and n_e % eb == 0
    nb = r // rows
    in_spec = pl.BlockSpec((None, eb, rows, c), lambda *g: (layer, step_of(*g) // nb, step_of(*g) % nb, 0))
    out_spec = pl.BlockSpec((eb, rows, c), lambda *g: (step_of(*g) // nb, step_of(*g) % nb, 0))
    return in_spec, out_spec, jax.ShapeDtypeStruct((n_e, r, c), BF16)


def _mm_body(*refs, epilogue, n_extra, n_side):
    a_ref, w_ref = refs[0], refs[1]
    extra = refs[2:2 + n_extra]
    side_in = refs[2 + n_extra:2 + n_extra + n_side]
    o_ref = refs[2 + n_extra + n_side]
    side_out = refs[3 + n_extra + n_side:]
    acc = jnp.dot(a_ref[...], w_ref[...], preferred_element_type=F32)
    o_ref[...] = epilogue(acc, *extra).astype(o_ref.dtype)
    for s_in, s_out in zip(side_in, side_out):
        s_out[...] = s_in[...].astype(s_out.dtype)


def _matmul(a, w, *, tm, tn, out_dtype, epilogue, extras=(), side=(), name):
    p, m, k = a.shape
    n = w.shape[1]
    tm, tn = min(tm, m), min(tn, n)
    ni, nj = m // tm, n // tn
    side_specs = [_side_cast_specs(src, layer, p * ni * nj, lambda q, i, j: (q * ni + i) * nj + j)
                  for src, layer in side]
    in_specs = [
        pl.BlockSpec((None, tm, k), lambda q, i, j: (q, i, 0)),
        pl.BlockSpec((k, tn), lambda q, i, j: (0, j)),
    ] + [s for _, s in extras] + [s[0] for s in side_specs]
    out = pl.pallas_call(
        functools.partial(_mm_body, epilogue=epilogue, n_extra=len(extras), n_side=len(side)),
        grid=(p, ni, nj),
        in_specs=in_specs,
        out_specs=[pl.BlockSpec((None, tm, tn), lambda q, i, j: (q, i, j))] + [s[1] for s in side_specs],
        out_shape=[jax.ShapeDtypeStruct((p, m, n), out_dtype)] + [s[2] for s in side_specs],
        compiler_params=_params(("arbitrary", "arbitrary", "arbitrary")),
        name=name,
    )(a, w, *[x for x, _ in extras], *[src for src, _ in side])
    return out if side else out[0]


def _ep_identity(acc):
    return acc


def _ep_glu(acc, y_ref, b_ref):
    return y_ref[...].astype(F32) * jax.nn.sigmoid(acc + b_ref[...])


def _ep_bias_gelu(acc, b_ref):
    return jax.nn.gelu(acc + b_ref[...])


def _ep_residual(acc, x_ref, g_ref):
    return x_ref[...] + g_ref[...] * acc


def _plane_rows(block, seg, *, cb, kb):
    per = cb // (kb * SEGMENTS)
    return pl.ds((block // per) * cb + seg * (cb // SEGMENTS) + (block % per) * kb, kb)


def _norm_plane_body(x_hbm, g_ref, s_ref, o_ref, buf, sem, *, cb):
    i = pl.program_id(0)
    nsteps = pl.num_programs(0)
    kb = buf.shape[2]
    slot = i % 2

    def fetch(step, sl, t, s):
        return pltpu.make_async_copy(x_hbm.at[_plane_rows(step, s, cb=cb, kb=kb), t], buf.at[sl, t, :, s], sem.at[sl, t])

    def start_all(step, sl):
        for t in range(CHUNK):
            for s in range(SEGMENTS):
                fetch(step, sl, t, s).start()

    @pl.when(i == 0)
    def _():
        start_all(0, 0)

    @pl.when(i + 1 < nsteps)
    def _():
        start_all(i + 1, 1 - slot)

    for t in range(CHUNK):
        for s in range(SEGMENTS):
            fetch(i, slot, t, s).wait()
        xt = buf[slot, t].reshape(kb * SEGMENTS, buf.shape[-1])
        o_ref[t] = _rms_mod(xt, g_ref[...], s_ref[...]).astype(o_ref.dtype)


def _norm_to_planes(x3, geff, shift, *, chunks_per_batch):
    nc, _, d = x3.shape
    kb = 4
    tc = kb * SEGMENTS
    per = chunks_per_batch // tc
    vec = pl.BlockSpec((None, 1, d), lambda i: (i // per, 0, 0))
    return pl.pallas_call(
        functools.partial(_norm_plane_body, cb=chunks_per_batch),
        grid=(nc // tc,),
        in_specs=[pl.BlockSpec(memory_space=pl.ANY), vec, vec],
        out_specs=pl.BlockSpec((CHUNK, tc, d), lambda i: (0, i, 0)),
        out_shape=jax.ShapeDtypeStruct((CHUNK, nc, d), BF16),
        scratch_shapes=[pltpu.VMEM((2, CHUNK, kb, SEGMENTS, d), F32), pltpu.SemaphoreType.DMA((2, CHUNK))],
        compiler_params=_params(("arbitrary",)),
        name="s5_norm_planes",
    )(x3, geff, shift)


def _cmul_add(ar, ai, xr, xi, vr, vi):
    return ar * xr - ai * xi + vr, ar * xi + ai * xr + vi


def _ssm_body(u_ref, wac_ref, wcc_ref, a1_ref, aseg_ref, d_ref, y_ref, xf_ref, xb_ref, wa_ref, wc_ref, *,
              rows_per_dot, gl, hsz, psz):
    nb, cb = xf_ref.shape[0], xf_ref.shape[1]
    ks = cb // SEGMENTS
    half = xf_ref.shape[-1]
    q = half // 2

    if True:
        iota = lambda shape, ax: lax.broadcasted_iota(jnp.int32, shape, ax)
        ca, na = wac_ref.shape[1], wa_ref.shape[1]
        dcol = iota((ca, na), 1)
        spread = (iota((ca, na), 0) == (dcol // (gl * psz)) * psz + dcol % psz).astype(BF16)
        rep = jnp.dot(wac_ref[...], spread, preferred_element_type=F32)
        shp = rep.shape
        keep = (iota(shp, 0) // hsz) % gl == (iota(shp, 1) // psz) % gl
        wa_ref[...] = jnp.where(keep, rep, 0.0).astype(wa_ref.dtype)
        cc, nc_ = wcc_ref.shape[1], wc_ref.shape[1]
        dcol = iota((cc, nc_), 1)
        spread = (iota((cc, nc_), 0) == (dcol // (gl * hsz)) * hsz + dcol % hsz).astype(BF16)
        rep = jnp.dot(wcc_ref[...], spread, preferred_element_type=F32)
        shp = rep.shape
        n_tok_rows = CHUNK * gl * hsz
        r = iota(shp, 0)
        row_g = jnp.where(r < n_tok_rows, (r // hsz) % gl, ((r - n_tok_rows) // psz) % gl)
        keep = row_g == (iota(shp, 1) // hsz) % gl
        wc_ref[...] = jnp.where(keep, rep, 0.0).astype(wc_ref.dtype)

    def u_rows(r0, nr):
        return jnp.concatenate([u_ref[t, pl.ds(r0, nr), :] for t in range(CHUNK)], axis=1)

    nr = min(cb, rows_per_dot)
    afr, afi, abr, abi = (a1_ref[i:i + 1, :] for i in range(4))
    sfr, sfi, sbr, sbi = (aseg_ref[i:i + 1, :] for i in range(4))
    group = lambda j: slice(j * SEGMENTS, (j + 1) * SEGMENTS)
    lw = d_ref.shape[-1]

    def phase_a(b):
        for r0 in range(0, cb, nr):
            res = jnp.dot(u_rows(b * cb + r0, nr), wa_ref[...], preferred_element_type=F32)
            xf_ref[b, pl.ds(r0, nr), :] = res[:, :half]
            xb_ref[b, pl.ds(r0, nr), :] = res[:, half:]

    def phase_b(b):
        z = jnp.zeros((SEGMENTS, q), F32)
        cfr, cfi, cbr, cbi = z, z, z, z
        for j in range(ks):
            jb = ks - 1 - j
            vf = xf_ref[b, group(j), :]
            vb = xb_ref[b, group(jb), :]
            xf_ref[b, group(j), :] = jnp.concatenate([cfr, cfi], axis=1)
            xb_ref[b, group(jb), :] = jnp.concatenate([cbr, cbi], axis=1)
            cfr, cfi = _cmul_add(afr, afi, cfr, cfi, vf[:, :q], vf[:, q:])
            cbr, cbi = _cmul_add(abr, abi, cbr, cbi, vb[:, :q], vb[:, q:])
        efr, efi, ebr, ebi = cfr, cfi, cbr, cbi

        z1 = jnp.zeros((1, q), F32)
        gfr, gfi = [z1], [z1]
        for s in range(SEGMENTS - 1):
            r, i = _cmul_add(sfr, sfi, gfr[-1], gfi[-1], efr[s:s + 1], efi[s:s + 1])
            gfr.append(r)
            gfi.append(i)
        gbr, gbi = [z1], [z1]
        for s in range(SEGMENTS - 1, 0, -1):
            r, i = _cmul_add(sbr, sbi, gbr[0], gbi[0], ebr[s:s + 1], ebi[s:s + 1])
            gbr.insert(0, r)
            gbi.insert(0, i)
        gfr, gfi, gbr, gbi = (jnp.concatenate(g, axis=0) for g in (gfr, gfi, gbr, gbi))

        pfr = pbr = jnp.ones((1, q), F32)
        pfi = pbi = jnp.zeros((1, q), F32)
        for j in range(ks):
            jb = ks - 1 - j
            cf = xf_ref[b, group(j), :]
            cb_ = xb_ref[b, group(jb), :]
            xf_ref[b, group(j), :] = jnp.concatenate(
                [cf[:, :q] + pfr * gfr - pfi * gfi, cf[:, q:] + pfr * gfi + pfi * gfr], axis=1)
            xb_ref[b, group(jb), :] = jnp.concatenate(
                [cb_[:, :q] + pbr * gbr - pbi * gbi, cb_[:, q:] + pbr * gbi + pbi * gbr], axis=1)
            pfr, pfi = pfr * afr - pfi * afi, pfr * afi + pfi * afr
            pbr, pbi = pbr * abr - pbi * abi, pbr * abi + pbi * abr

    def phase_c(b):
        for r0 in range(0, cb, nr):
            rows = pl.ds(b * cb + r0, nr)
            xf = xf_ref[b, pl.ds(r0, nr), :]
            xb = xb_ref[b, pl.ds(r0, nr), :]
            lhs = jnp.concatenate([u_rows(b * cb + r0, nr), xf.astype(BF16), xb.astype(BF16)], axis=1)
            acc = jnp.dot(lhs, wc_ref[...], preferred_element_type=F32)
            for t in range(CHUNK):
                yt = acc[:, t * lw:(t + 1) * lw] + d_ref[...] * u_ref[t, rows, :].astype(F32)
                y_ref[t, rows, :] = jax.nn.gelu(yt).astype(y_ref.dtype)

    for b in range(nb):
        phase_a(b)
    for b in range(nb):
        phase_b(b)
    for b in range(nb):
        phase_c(b)


def _s5_ssm(u8, wa, wc, a1, aseg, dvec, *, chunks_per_batch, group, state):
    _, nc, w = u8.shape
    nslab = w // LANES
    nb = nc // chunks_per_batch
    cb = chunks_per_batch
    ks = cb // SEGMENTS
    gl = LANES // group
    half = 2 * gl * state
    tok = CHUNK * LANES
    body = functools.partial(_ssm_body, rows_per_dot=512, gl=gl, hsz=group, psz=state)
    return pl.pallas_call(
        body,
        grid=(nslab,),
        in_specs=[
            pl.BlockSpec((CHUNK, nc, LANES), lambda s: (0, 0, s)),
            pl.BlockSpec((None,) + wa.shape[1:], lambda s: (s, 0, 0)),
            pl.BlockSpec((None,) + wc.shape[1:], lambda s: (s, 0, 0)),
            pl.BlockSpec((None,) + a1.shape[1:], lambda s: (s, 0, 0)),
            pl.BlockSpec((None,) + aseg.shape[1:], lambda s: (s, 0, 0)),
            pl.BlockSpec((1, LANES), lambda s: (0, s)),
        ],
        out_specs=pl.BlockSpec((CHUNK, nc, LANES), lambda s: (0, 0, s)),
        out_shape=jax.ShapeDtypeStruct(u8.shape, BF16),
        scratch_shapes=[
            pltpu.VMEM((nb, cb, half), F32),
            pltpu.VMEM((nb, cb, half), F32),
            pltpu.VMEM((tok, 2 * half), BF16),
            pltpu.VMEM((tok + 2 * half, tok), BF16),
        ],
        compiler_params=_params(("arbitrary",)),
        name="s5_ssm",
    )(u8, wa, wc, a1, aseg, dvec)


def _s5_weights(lam_re, lam_im, log_step, b_re, b_im, c_re, c_im, *, seg_rows):
    _, g, p = lam_re.shape
    h = b_re.shape[-1]
    gl = LANES // h
    s = g // gl
    t = CHUNK
    lam = lax.complex(lam_re.astype(F32), lam_im.astype(F32))
    zed = lam * jnp.exp(log_step.astype(F32))[..., None]
    lam_bar = jnp.exp(zed)
    bbar = ((lam_bar - 1.0) / lam)[..., None] * lax.complex(b_re.astype(F32), b_im.astype(F32))
    cmat = lax.complex(c_re.astype(F32), c_im.astype(F32))
    pw = jnp.exp(zed[:, None] * jnp.arange(t + 1, dtype=F32)[None, :, None, None])
    pw_down = jnp.exp(zed[:, None] * (t - jnp.arange(t + 1, dtype=F32))[None, :, None, None])

    ma_f = pw_down[0, 1:, :, :, None] * bbar[0][None]
    ma_b = pw[1, :t, :, :, None] * bbar[1][None]
    wa = jnp.stack([ma_f.real, ma_f.imag, ma_b.real, ma_b.imag], axis=0)
    wa = wa.reshape(4, t, s, gl, p, h).transpose(2, 1, 3, 5, 0, 4).reshape(s, t * gl * h, 4 * p)

    kf = jnp.einsum('gyp,dgp,gph->dgyh', cmat[0], pw[0, :t], bbar[0]).real
    kb = jnp.einsum('gyp,dgp,gph->dgyh', cmat[1], pw[1, :t], bbar[1]).real
    st = jnp.arange(t)
    lag = st[None, None, :] - st[None, :, None]
    sel_f = (lag == st[:, None, None]).astype(F32)
    sel_b = (-lag == st[:, None, None]).astype(F32)
    coef = jnp.einsum('dat,dgyh->atgyh', sel_f, kf) + jnp.einsum('dat,dgyh->atgyh', sel_b, kb)
    w_u = coef.reshape(t, t, s, gl, h, h).transpose(2, 0, 3, 5, 1, 4).reshape(s, t * gl * h, t * h)

    cl_f = cmat[0][None] * pw[0, 1:][:, :, None, :]
    cl_b = cmat[1][None] * pw_down[1, :t][:, :, None, :]
    def state_rows(x):
        return x.reshape(t, s, gl, h, p).transpose(1, 2, 4, 0, 3).reshape(s, gl * p, t * h)
    wc = jnp.concatenate([w_u, state_rows(cl_f.real), state_rows(-cl_f.imag),
                          state_rows(cl_b.real), state_rows(-cl_b.imag)], axis=1)

    def lanes(x):
        x = x.reshape(2, s, gl * p)
        return jnp.stack([x[0].real, x[0].imag, x[1].real, x[1].imag], axis=1)
    a1 = lanes(pw[:, t])
    aseg = lanes(jnp.exp(zed * float(t * seg_rows)))
    return wa.astype(BF16), wc.astype(BF16), a1, aseg


def _s5_out_body(a_ref, w_ref, g_ref, side_ref, x_hbm, o_hbm, side_out, xin, xout, isem, osem, *, cb):
    side_out[...] = side_ref[...].astype(side_out.dtype)
    nj = pl.num_programs(1)
    step = pl.program_id(0) * nj + pl.program_id(1)
    total = pl.num_programs(0) * nj
    tm, tn = a_ref.shape[1], w_ref.shape[1]
    kb = tm // SEGMENTS
    slot = step % 2

    def window(st, t, s):
        cols = pl.ds(pl.multiple_of((st % nj) * tn, tn), tn)
        return (_plane_rows(st // nj, s, cb=cb, kb=kb), t, cols)

    def fetch(st, sl, t, s):
        return pltpu.make_async_copy(x_hbm.at[window(st, t, s)], xin.at[sl, t, :, s], isem.at[sl, t])

    def put(st, sl, t, s):
        return pltpu.make_async_copy(xout.at[sl, t, :, s], o_hbm.at[window(st, t, s)], osem.at[sl, t])

    def each(fn):
        for t in range(CHUNK):
            for s in range(SEGMENTS):
                fn(t, s)

    @pl.when(step == 0)
    def _():
        each(lambda t, s: fetch(0, 0, t, s).start())

    @pl.when(step + 1 < total)
    def _():
        each(lambda t, s: fetch(step + 1, 1 - slot, t, s).start())

    a = a_ref[...].reshape(CHUNK * tm, a_ref.shape[2])
    acc = jnp.dot(a, w_ref[...], preferred_element_type=F32)

    @pl.when(step >= 2)
    def _():
        each(lambda t, s: put(step - 2, slot, t, s).wait())

    for t in range(CHUNK):
        for s in range(SEGMENTS):
            fetch(step, slot, t, s).wait()
        xt = xin[slot, t].reshape(tm, tn) + g_ref[...] * acc[t * tm:(t + 1) * tm]
        xout[slot, t] = xt.reshape(kb, SEGMENTS, tn)
    each(lambda t, s: put(step, slot, t, s).start())

    @pl.when(step == total - 1)
    def _():
        each(lambda t, s: put(step, slot, t, s).wait())

        @pl.when(total >= 2)
        def _():
            each(lambda t, s: put(step - 1, 1 - slot, t, s).wait())


def _s5_out(z8, w, x3, gate, side, *, chunks_per_batch):
    _, nc, k = z8.shape
    d = w.shape[1]
    tm = min(128, chunks_per_batch)
    tn = min(512, d)
    per = chunks_per_batch // tm
    kb = tm // SEGMENTS
    nj = d // tn
    side_in, side_out, side_shape = _side_cast_specs(side[0], side[1], (nc // tm) * nj, lambda i, j: i * nj + j)
    return pl.pallas_call(
        functools.partial(_s5_out_body, cb=chunks_per_batch),
        grid=(nc // tm, nj),
        in_specs=[
            pl.BlockSpec((CHUNK, tm, k), lambda i, j: (0, i, 0)),
            pl.BlockSpec((k, tn), lambda i, j: (0, j)),
            pl.BlockSpec((None, 1, tn), lambda i, j: (i // per, 0, j)),
            side_in,
            pl.BlockSpec(memory_space=pl.ANY),
        ],
        out_specs=[pl.BlockSpec(memory_space=pl.ANY), side_out],
        out_shape=[jax.ShapeDtypeStruct(x3.shape, F32), side_shape],
        scratch_shapes=[
            pltpu.VMEM((2, CHUNK, kb, SEGMENTS, tn), F32),
            pltpu.VMEM((2, CHUNK, kb, SEGMENTS, tn), F32),
            pltpu.SemaphoreType.DMA((2, CHUNK)),
            pltpu.SemaphoreType.DMA((2, CHUNK)),
        ],
        compiler_params=_params(("arbitrary", "arbitrary")),
        name="s5_out",
    )(z8, w, gate, side[0], x3)


def _gm_gate_body(z_ref, lng_ref, lnb_ref, ws_ref, bs_ref, o_ref, *, chunk):
    tm, w2 = z_ref.shape
    w = w2 // 2
    heads = ws_ref.shape[0]
    hd = w // heads
    v = z_ref[:, w:].astype(F32)
    mu = jnp.mean(v, axis=-1, keepdims=True)
    vc = v - mu
    var = jnp.mean(vc * vc, axis=-1, keepdims=True)
    vn = (vc * lax.rsqrt(var + EPS) * lng_ref[...] + lnb_ref[...]).astype(BF16)
    for c in range(tm // chunk):
        rows = slice(c * chunk, (c + 1) * chunk)
        for hh in range(heads):
            cols = slice(hh * hd, (hh + 1) * hd)
            sv = jnp.dot(ws_ref[hh], vn[rows, cols], preferred_element_type=F32) + bs_ref[:, hh:hh + 1]
            o_ref[rows, cols] = (z_ref[rows, cols].astype(F32) * sv).astype(o_ref.dtype)


def _gm_gate(z, ln_g, ln_b, ws, bs, *, chunk):
    n, w2 = z.shape
    w = w2 // 2
    tm = 2 * chunk
    full = lambda a: pl.BlockSpec(a.shape, lambda i: (0,) * a.ndim)
    return pl.pallas_call(
        functools.partial(_gm_gate_body, chunk=chunk),
        grid=(n // tm,),
        in_specs=[pl.BlockSpec((tm, w2), lambda i: (i, 0)), full(ln_g), full(ln_b), full(ws), full(bs)],
        out_specs=pl.BlockSpec((tm, w), lambda i: (i, 0)),
        out_shape=jax.ShapeDtypeStruct((n, w), BF16),
        compiler_params=_params(("arbitrary",)),
        name="gmlp_gate",
    )(z, ln_g, ln_b, ws, bs)


def _pack_bf16_pairs(x):
    half = x.shape[1] // 2
    bits = lambda v: lax.bitcast_convert_type(v.astype(F32), jnp.int32)
    return bits(x[:, half:]) | lax.shift_right_logical(bits(x[:, :half]), 16)


def _unpack_bf16_pairs(w):
    lo = lax.bitcast_convert_type(lax.shift_left(w, 16), F32)
    hi = lax.bitcast_convert_type(w & jnp.int32(-65536), F32)
    return jnp.concatenate([lo.astype(BF16), hi.astype(BF16)], axis=1)


def _top2_sum(a, b, c, d):
    m1, n1 = jnp.maximum(a, b), jnp.minimum(a, b)
    m2, n2 = jnp.maximum(c, d), jnp.minimum(c, d)
    return jnp.maximum(m1, m2) + jnp.maximum(jnp.minimum(m1, m2), jnp.maximum(n1, n2))


def _first_argmax(vals):
    best, idx = vals[0], jnp.zeros(vals[0].shape, jnp.int32)
    for j in range(1, len(vals)):
        upd = vals[j] > best
        idx = jnp.where(upd, j, idx)
        best = jnp.where(upd, vals[j], best)
    return idx, best


def _ffn_prep_body(x_ref, g_ref, s_ref, rwt_ref, rb_ref, hn_ref, e_ref, gw_ref, rank_ref, cnt_ref, carry_ref):
    i = pl.program_id(0)
    n_e = rwt_ref.shape[1]
    epg = n_e // N_GROUPS_MOE
    tm = x_ref.shape[0]

    @pl.when(i == 0)
    def _():
        carry_ref[...] = jnp.zeros_like(carry_ref)

    hn = _rms_mod(x_ref[...], g_ref[...], s_ref[...])
    hn_hi = hn.astype(BF16)
    hn_lo = (hn - hn_hi.astype(F32)).astype(BF16)
    hn_ref[...] = _pack_bf16_pairs(hn_hi)
    nt_dot = lambda a, b: lax.dot_general(a, b, (((1,), (1,)), ((), ())), preferred_element_type=F32)
    logits = nt_dot(rwt_ref[0], hn_hi) + (nt_dot(rwt_ref[1], hn_hi) + nt_dot(rwt_ref[0], hn_lo))
    score = jax.nn.sigmoid(logits)
    sel = score + rb_ref[...]

    row = lambda a, r: a[r:r + 1, :]
    gscore = [_top2_sum(*[row(sel, g * epg + j) for j in range(epg)]) for g in range(N_GROUPS_MOE)]
    bg, _ = _first_argmax(gscore)
    pick = lambda a, j: functools.reduce(
        lambda acc, g: jnp.where(bg == g, row(a, g * epg + j), acc), range(1, N_GROUPS_MOE), row(a, j))
    sel_in = [pick(sel, j) for j in range(epg)]
    sc_in = [pick(score, j) for j in range(epg)]
    i1, _ = _first_argmax(sel_in)
    neg = jnp.full_like(sel_in[0], -jnp.inf)
    i2, _ = _first_argmax([jnp.where(i1 == j, neg, sel_in[j]) for j in range(epg)])
    take = lambda vals, idx: functools.reduce(lambda acc, j: jnp.where(idx == j, vals[j], acc), range(1, epg), vals[0])
    w1, w2 = take(sc_in, i1), take(sc_in, i2)
    den = w1 + w2
    e1, e2 = bg * epg + i1, bg * epg + i2
    e_ref[...] = jnp.concatenate([e1, e2], axis=0)
    gw_ref[...] = jnp.concatenate([w1 / den, w2 / den], axis=0)

    eid = lax.broadcasted_iota(jnp.int32, (n_e, tm), 0)
    oh1 = (eid == e1).astype(F32)
    oh2 = (eid == e2).astype(F32)
    oh = oh1 + oh2
    tri = (lax.broadcasted_iota(jnp.int32, (tm, tm), 0) <= lax.broadcasted_iota(jnp.int32, (tm, tm), 1)).astype(BF16)
    incl = jnp.dot(oh.astype(BF16), tri, preferred_element_type=F32)
    before = carry_ref[:, 0:1] + incl - oh
    r1 = jnp.sum(oh1 * before, axis=0, keepdims=True)
    r2 = jnp.sum(oh2 * before, axis=0, keepdims=True)
    rank_ref[...] = jnp.concatenate([r1, r2], axis=0).astype(jnp.int32)
    carry_ref[...] = carry_ref[...] + incl[:, tm - 1:tm]
    cnt_ref[...] = carry_ref[...].astype(jnp.int32)


def _ffn_prep(x, geff, shift, rwt, rb, *, seq):
    n, d = x.shape
    n_e = rwt.shape[1]
    tm = min(256, seq)
    per = seq // tm
    vec = pl.BlockSpec((None, 1, d), lambda i: (i // per, 0, 0))
    row2 = pl.BlockSpec((TOP_K, tm), lambda i: (0, i))
    hn, e, gw, rank, cnt = pl.pallas_call(
        _ffn_prep_body,
        grid=(n // tm,),
        in_specs=[pl.BlockSpec((tm, d), lambda i: (i, 0)), vec, vec,
                  pl.BlockSpec((2, n_e, d), lambda i: (0, 0, 0)), pl.BlockSpec((n_e, 1), lambda i: (0, 0))],
        out_specs=[pl.BlockSpec((tm, d // 2), lambda i: (i, 0)), row2, row2, row2,
                   pl.BlockSpec((n_e, LANES), lambda i: (0, 0))],
        out_shape=[jax.ShapeDtypeStruct((n, d // 2), jnp.int32), jax.ShapeDtypeStruct((TOP_K, n), jnp.int32),
                   jax.ShapeDtypeStruct((TOP_K, n), F32), jax.ShapeDtypeStruct((TOP_K, n), jnp.int32),
                   jax.ShapeDtypeStruct((n_e, LANES), jnp.int32)],
        scratch_shapes=[pltpu.VMEM((n_e, LANES), F32)],
        compiler_params=_params(("arbitrary",)),
        name="ffn_prep",
    )(x, geff, shift, rwt, rb)
    return hn, e, gw, rank, cnt[:, 0]


def _for_rows(lo, hi, fn, unroll=1):
    def body(r, c):
        fn(r)
        return c
    lax.fori_loop(lo, hi, body, 0, unroll=unroll)


def _dispatch_body(pos_ref, fill_ref, hn_ref, xs_hbm, zero_ref, sem, zsem, *, n_tokens):
    i = pl.program_id(0)
    tm = hn_ref.shape[0]

    def row_copy(r, k):
        return pltpu.make_async_copy(hn_ref.at[pl.ds(r, 1)], xs_hbm.at[pl.ds(pos_ref[k * n_tokens + i * tm + r], 1)], sem)

    def start(r):
        for k in range(TOP_K):
            row_copy(r, k).start()

    for r in range(tm):
        start(r)

    @pl.when(i == 0)
    def _():
        zero_ref[...] = jnp.zeros_like(zero_ref)
        n_fill = fill_ref.shape[0] // 2

        def zero_copy(row):
            return pltpu.make_async_copy(zero_ref, xs_hbm.at[pl.ds(row, 1)], zsem)

        for e in range(n_fill):
            _for_rows(fill_ref[2 * e], fill_ref[2 * e + 1], lambda row: zero_copy(row).start())
        for e in range(n_fill):
            _for_rows(fill_ref[2 * e], fill_ref[2 * e + 1], lambda row: zero_copy(row).wait())

    for k in range(TOP_K):
        pltpu.make_async_copy(hn_ref, xs_hbm.at[pl.ds(0, tm)], sem).wait()


def _moe_dispatch(hn, pos_flat, fill, *, rows):
    n, dh = hn.shape
    tm = min(512, n)
    grid_spec = pltpu.PrefetchScalarGridSpec(
        num_scalar_prefetch=2,
        grid=(n // tm,),
        in_specs=[pl.BlockSpec((tm, dh), lambda i, p, f: (i, 0))],
        out_specs=pl.BlockSpec(memory_space=pl.ANY),
        scratch_shapes=[pltpu.VMEM((1, dh), jnp.int32), pltpu.SemaphoreType.DMA(()), pltpu.SemaphoreType.DMA(())],
    )
    return pl.pallas_call(
        functools.partial(_dispatch_body, n_tokens=n),
        grid_spec=grid_spec,
        out_shape=jax.ShapeDtypeStruct((rows, dh), jnp.int32),
        compiler_params=_params(("arbitrary",)),
        name="moe_dispatch",
    )(pos_flat, fill, hn)


def _moe_body(te_ref, nrows_ref, x_ref, w1_ref, w3_ref, w2_ref, o_ref):
    i = pl.program_id(0)
    used = nrows_ref[i] > 0

    @pl.when(used)
    def _():
        xb = _unpack_bf16_pairs(x_ref[...])
        h1 = jnp.dot(xb, w1_ref[...], preferred_element_type=F32)
        h3 = jnp.dot(xb, w3_ref[...], preferred_element_type=F32)
        hh = (h1 * jax.nn.sigmoid(h1) * h3).astype(BF16)
        out = jnp.dot(hh, w2_ref[...], preferred_element_type=F32)
        o_ref[...] = _pack_bf16_pairs(out.astype(BF16))

    @pl.when(jnp.logical_not(used))
    def _():
        o_ref[...] = jnp.zeros_like(o_ref)


def _moe_experts(xs, w1, w3, w2, tile_expert, tile_rows):
    rows, dh = xs.shape
    d = 2 * dh
    fdim = w1.shape[-1]
    tm = MOE_TILE
    resident = pl.Buffered(1)
    grid_spec = pltpu.PrefetchScalarGridSpec(
        num_scalar_prefetch=2,
        grid=(rows // tm,),
        in_specs=[
            pl.BlockSpec((tm, dh), lambda i, te, nr: (i, 0)),
            pl.BlockSpec((None, d, fdim), lambda i, te, nr: (te[i], 0, 0)),
            pl.BlockSpec((None, d, fdim), lambda i, te, nr: (te[i], 0, 0)),
            pl.BlockSpec((None, fdim, d), lambda i, te, nr: (te[i], 0, 0), pipeline_mode=resident),
        ],
        out_specs=pl.BlockSpec((tm, dh), lambda i, te, nr: (i, 0)),
    )
    return pl.pallas_call(
        _moe_body,
        grid_spec=grid_spec,
        out_shape=jax.ShapeDtypeStruct((rows, dh), jnp.int32),
        compiler_params=_params(("arbitrary",)),
        name="moe_experts",
    )(tile_expert, tile_rows, xs, w1, w3, w2)


def _moe_plan(e, rank, cnt, n_tiles_max):
    tm = MOE_TILE
    n_e = cnt.shape[0]
    tiles_per = (cnt + tm - 1) // tm
    ends = jnp.cumsum(tiles_per)
    off = (ends - tiles_per) * tm
    onehot = e[..., None] == jnp.arange(n_e, dtype=jnp.int32)
    pos = jnp.sum(jnp.where(onehot, off, 0), axis=-1) + rank
    tile = jnp.arange(n_tiles_max, dtype=jnp.int32)
    tile_expert = jnp.minimum(jnp.sum(tile[:, None] >= ends[None, :], axis=1), n_e - 1).astype(jnp.int32)
    mine = tile_expert[:, None] == jnp.arange(n_e, dtype=jnp.int32)
    tile_rows = jnp.clip(jnp.sum(jnp.where(mine, off + cnt, 0), axis=1) - tile * tm, 0, tm)
    tile_rows = jnp.where(tile < ends[-1], tile_rows, 0).astype(jnp.int32)
    fill_end = jnp.where(jnp.arange(n_e) == n_e - 1, n_tiles_max * tm, ends * tm)
    fill = jnp.stack([off + cnt, fill_end], axis=1).reshape(-1).astype(jnp.int32)
    return pos.astype(jnp.int32), tile_expert, tile_rows, fill


COMBINE_ROW_CHUNKS = 8


def _combine_rows(pos_ref, x_ref, y_hbm, gw_ref, gate_ref, ybuf, sem, finish, *, n_tokens):
    i = pl.program_id(0)
    nsteps = pl.num_programs(0)
    tm = x_ref.shape[0]
    slot = i % 2
    nxt = (i + 1) % nsteps

    def row_copy(step, sl, r, k):
        row = pos_ref[k * n_tokens + step * tm + r]
        return pltpu.make_async_copy(y_hbm.at[pl.ds(row, 1)], ybuf.at[sl, k, pl.ds(r, 1)], sem.at[sl])

    def wait_tile(sl):
        for k in range(TOP_K):
            pltpu.make_async_copy(y_hbm.at[pl.ds(0, tm)], ybuf.at[sl, k], sem.at[sl]).wait()

    @pl.when(i == 0)
    def _():
        def one(r):
            for k in range(TOP_K):
                row_copy(0, 0, r, k).start()
        _for_rows(0, tm, one)

    wait_tile(slot)
    rc = tm // COMBINE_ROW_CHUNKS
    for c in range(COMBINE_ROW_CHUNKS):
        for r in range(c * rc, (c + 1) * rc):
            for k in range(TOP_K):
                row_copy(nxt, 1 - slot, r, k).start()
        rows = pl.ds(c * rc, rc)
        gw = gw_ref[rows, :]
        y0, y1 = ybuf[slot, 0, rows, :], ybuf[slot, 1, rows, :]
        halves = []
        for part in (lambda w: lax.shift_left(w, 16), lambda w: w & jnp.int32(-65536)):
            f = lambda w: lax.bitcast_convert_type(part(w), F32)
            halves.append(gw[:, 0:1] * f(y0) + gw[:, 1:2] * f(y1))
        moe = jnp.concatenate(halves, axis=1)
        finish(rows, x_ref[rows, :] + gate_ref[...] * moe)

    @pl.when(i == nsteps - 1)
    def _():
        wait_tile(1 - slot)


def _combine_norm_body(pos_ref, x_ref, y_hbm, gw_ref, gate_ref, g_ref, s_ref, xo_ref, hn_ref, ybuf, sem, *, n_tokens):
    def finish(rows, xn):
        xo_ref[rows, :] = xn
        hn_ref[rows, :] = _rms_mod(xn, g_ref[...], s_ref[...]).astype(hn_ref.dtype)
    _combine_rows(pos_ref, x_ref, y_hbm, gw_ref, gate_ref, ybuf, sem, finish, n_tokens=n_tokens)


def _combine_final_body(pos_ref, x_ref, y_hbm, gw_ref, gate_ref, g_ref, o_ref, ybuf, sem, *, n_tokens):
    def finish(rows, xn):
        ms = jnp.mean(xn * xn, axis=-1, keepdims=True)
        o_ref[rows, :] = xn * lax.rsqrt(ms + EPS) * g_ref[...]
    _combine_rows(pos_ref, x_ref, y_hbm, gw_ref, gate_ref, ybuf, sem, finish, n_tokens=n_tokens)


def _ffn_combine(x, ys, pos_flat, gw_t, gate, norm_args, *, seq, final):
    n, d = x.shape
    tm = min(256, seq)
    per = seq // tm
    tile = pl.BlockSpec((tm, d), lambda i, p: (i, 0))
    vec = pl.BlockSpec((None, 1, d), lambda i, p: (i // per, 0, 0))
    in_specs = [tile, pl.BlockSpec(memory_space=pl.ANY), pl.BlockSpec((tm, TOP_K), lambda i, p: (i, 0)), vec]
    scratch = [pltpu.VMEM((2, TOP_K, tm, ys.shape[1]), ys.dtype), pltpu.SemaphoreType.DMA((2,))]
    if final:
        body, name = _combine_final_body, "ffn_combine_final"
        in_specs.append(pl.BlockSpec((1, d), lambda i, p: (0, 0)))
        out_specs, out_shape = tile, jax.ShapeDtypeStruct((n, d), F32)
    else:
        body, name = _combine_norm_body, "ffn_combine_norm"
        in_specs += [vec, vec]
        out_specs = [tile, tile]
        out_shape = [jax.ShapeDtypeStruct((n, d), F32), jax.ShapeDtypeStruct((n, d), BF16)]
    grid_spec = pltpu.PrefetchScalarGridSpec(num_scalar_prefetch=1, grid=(n // tm,), in_specs=in_specs,
                                             out_specs=out_specs, scratch_shapes=scratch)
    return pl.pallas_call(
        functools.partial(body, n_tokens=n), grid_spec=grid_spec, out_shape=out_shape,
        compiler_params=_params(("arbitrary",)), name=name,
    )(pos_flat, x, ys, gw_t, gate, *norm_args)


def _moe_layer(x, geff, shift, rwt, rb, w1, w3, w2, *, seq):
    n, _ = x.shape
    n_e = rwt.shape[1]
    hn, e, gw, rank, cnt = _ffn_prep(x, geff, shift, rwt, rb, seq=seq)
    n_tiles_max = (TOP_K * n) // MOE_TILE + n_e
    pos, tile_expert, tile_rows, fill = _moe_plan(e, rank, cnt, n_tiles_max)
    pos_flat = pos.reshape(-1)
    xs = _moe_dispatch(hn, pos_flat, fill, rows=n_tiles_max * MOE_TILE)
    ys = _moe_experts(xs, w1, w3, w2, tile_expert, tile_rows)
    return ys, pos_flat, gw.T


def kernel(x, c, norm_mix_g, norm_ffn_g, norm_final_g, ada_mix_w, ada_mix_b, ada_ffn_w, ada_ffn_b, s5_w_in, s5_lam_re, s5_lam_im, s5_log_step, s5_b_re, s5_b_im, s5_c_re, s5_c_im, s5_d, s5_w_glu, s5_b_glu, s5_w_out, gm_w_in, gm_b_in, gm_ln_g, gm_ln_b, gm_ws, gm_bs, gm_w_out, router_w, router_bias, moe_w1, moe_w3, moe_w2):
    bsz, seq, d = x.shape
    n = bsz * seq
    depth = norm_mix_g.shape[0]
    assert depth == 2 and seq % (CHUNK * SEGMENTS) == 0
    cb = seq // CHUNK
    nc = n // CHUNK

    mods_mix = _adaln(c, ada_mix_w, ada_mix_b)
    mods_ffn = _adaln(c, ada_ffn_w, ada_ffn_b)

    def split(m, g):
        shift, scale, gate = jnp.split(m, 3, axis=-1)
        return (g[None] * (1.0 + scale))[:, None], shift[:, None], gate[:, None]

    rw_hi = router_w.T.astype(BF16)
    rwt = jnp.stack([rw_hi, (router_w.T - rw_hi.astype(F32)).astype(BF16)])
    rb = router_bias.reshape(-1, 1)
    xf = x.reshape(n, d)

    geff, shift, gate = split(mods_mix[0], norm_mix_g[0])
    x3 = xf.reshape(nc, CHUNK, d)
    hn8 = _norm_to_planes(x3, geff, shift, chunks_per_batch=cb)
    u8, w1_l0 = _matmul(hn8, s5_w_in[0].astype(BF16), tm=1024, tn=1024, out_dtype=BF16,
                        epilogue=_ep_identity, side=[(moe_w1, 0)], name="s5_in")
    wa, wc, a1, aseg = _s5_weights(s5_lam_re[0], s5_lam_im[0], s5_log_step[0], s5_b_re[0], s5_b_im[0],
                                   s5_c_re[0], s5_c_im[0], seg_rows=cb // SEGMENTS)
    y8 = _s5_ssm(u8, wa, wc, a1, aseg, s5_d[0].reshape(1, -1), chunks_per_batch=cb,
                 group=s5_b_re.shape[-1], state=s5_b_re.shape[-2])
    w = y8.shape[-1]
    tmg, tng = min(1024, nc), min(512, w)
    z8, w3_l0, w_out, w_gm_in, w_gm_out = _matmul(
        y8, s5_w_glu[0].astype(BF16), tm=tmg, tn=tng, out_dtype=BF16, epilogue=_ep_glu,
        extras=[(y8, pl.BlockSpec((None, tmg, tng), lambda q, i, j: (q, i, j))),
                (s5_b_glu[0].reshape(1, -1), pl.BlockSpec((1, tng), lambda q, i, j: (0, j)))],
        side=[(moe_w3, 0), (s5_w_out[None], 0), (gm_w_in[None], 0), (gm_w_out[None], 0)], name="s5_glu")
    x1, w2_l0 = _s5_out(z8, w_out[0], x3, gate, (moe_w2, 0), chunks_per_batch=cb)
    x1 = x1.reshape(n, d)

    geff, shift, gate = split(mods_ffn[0], norm_ffn_g[0])
    ys, pos, gw_t = _moe_layer(x1, geff, shift, rwt, rb, w1_l0, w3_l0, w2_l0, seq=seq)
    geff1, shift1, gate1 = split(mods_mix[1], norm_mix_g[1])
    x2, hn = _ffn_combine(x1, ys, pos, gw_t, gate, (geff1, shift1), seq=seq, final=False)

    w2n = gm_w_in.shape[-1]
    tm_in, tn_in = min(1024, n), min(1024, w2n)
    zz, w1_l1 = _matmul(
        hn[None], w_gm_in[0], tm=tm_in, tn=tn_in, out_dtype=BF16, epilogue=_ep_bias_gelu,
        extras=[(gm_b_in[0].reshape(1, -1), pl.BlockSpec((1, tn_in), lambda q, i, j: (0, j)))],
        side=[(moe_w1, 1)], name="gmlp_in")
    zz = zz[0]
    chunk = gm_ws.shape[-1]
    gated = _gm_gate(zz, gm_ln_g[0].reshape(1, -1), gm_ln_b[0].reshape(1, -1), gm_ws[0].astype(BF16),
                     gm_bs[0].T, chunk=chunk)
    tm_o, tn_o = min(1024, seq), min(512, d)
    per = seq // tm_o
    x3_, w3_l1, w2_l1 = _matmul(gated[None], w_gm_out[0], tm=tm_o, tn=tn_o, out_dtype=F32,
                                epilogue=_ep_residual,
                                extras=[(x2[None], pl.BlockSpec((None, tm_o, tn_o), lambda q, i, j: (q, i, j))),
                                        (gate1, pl.BlockSpec((None, 1, tn_o), lambda q, i, j: (i // per, 0, j)))],
                                side=[(moe_w3, 1), (moe_w2, 1)], name="gmlp_out")
    x3_ = x3_[0]

    geff, shift, gate = split(mods_ffn[1], norm_ffn_g[1])
    ys, pos, gw_t = _moe_layer(x3_, geff, shift, rwt, rb, w1_l1, w3_l1, w2_l1, seq=seq)
    out = _ffn_combine(x3_, ys, pos, gw_t, gate, (norm_final_g.reshape(1, -1),), seq=seq, final=True)
    return out.reshape(bsz, seq, d)
```
